```python
import jax, jax.numpy as jnp
from jax import lax
import numpy as np

D_MODEL = 1024
BATCH = 32
SEQ = 256
DEPTH = 2
DEC_BATCH = 4
DEC_SEQ = 4096
PAST_LEN = 256

GRID_W = 64
EPS = 1e-6
NEG = -1e30
N_BRANCH = 4
LRU_WIDTH = 256
LRU_BLOCKS = 4
LRU_BW = LRU_WIDTH // LRU_BLOCKS
LRU_CONV = 4
LRU_C = 8.0
NA_HEADS = 4
NA_HD = 64
NA_WR = 8
NA_WC = 16
NA_QB = 16
NA_HALO = NA_QB + NA_WC
CM_WIDTH = 256
CM_KERNEL = 31
MLA_HEADS = 4
MLA_Q_LORA = 256
MLA_KV_LORA = 128
MLA_NOPE = 64
MLA_ROPE = 32
MLA_VD = 64
ROPE_BASE = 10000.0
ATTN_QBLOCK = 128
MOE_GROUPS = 4
MOE_PER_GROUP = 4
MOE_EXPERTS = MOE_GROUPS * MOE_PER_GROUP
MOE_FF = 256
MOE_TOPK = 2
N_IN_SPLITS = (LRU_WIDTH, LRU_WIDTH, 3 * NA_HEADS * NA_HD, 2 * CM_WIDTH, MLA_Q_LORA, MLA_KV_LORA, MLA_ROPE, N_BRANCH * D_MODEL)
D_IN = 2 * LRU_WIDTH + 3 * NA_HEADS * NA_HD + 2 * CM_WIDTH + MLA_Q_LORA + MLA_KV_LORA + MLA_ROPE + N_BRANCH * D_MODEL

kernel_name = 'hybrid_prefix_flow_trunk_step'


def _rmsnorm(x, g):
    xf = x.astype(jnp.float32)
    y = xf * lax.rsqrt(jnp.mean(xf * xf, axis=-1, keepdims=True) + EPS)
    return (y * g.astype(jnp.float32)).astype(x.dtype)


def _layernorm(x, g, b):
    xf = x.astype(jnp.float32)
    mu = jnp.mean(xf, axis=-1, keepdims=True)
    var = jnp.mean(jnp.square(xf - mu), axis=-1, keepdims=True)
    y = (xf - mu) * lax.rsqrt(var + EPS)
    return (y * g.astype(jnp.float32) + b.astype(jnp.float32)).astype(x.dtype)


def _split_in(z):
    idx, acc = [], 0
    for w in N_IN_SPLITS[:-1]:
        acc += w
        idx.append(acc)
    return jnp.split(z, idx, axis=-1)


def _modulation(cvec, w_ada, b_ada):
    m = jax.nn.silu(cvec) @ w_ada + b_ada
    return jnp.split(m, 6, axis=-1)


def _dwconv(x, w, b, pad_l, pad_r):
    C = x.shape[-1]
    y = lax.conv_general_dilated(x, w[:, None, :].astype(x.dtype), window_strides=(1,),
                                 padding=((pad_l, pad_r),), dimension_numbers=('NWC', 'WIO', 'NWC'),
                                 feature_group_count=C)
    return y + b


def _lin_combine(left, right):
    a1, b1 = left
    a2, b2 = right
    return a1 * a2, a2 * b1 + b2


def _rglru(xc, wa, ba, wx, bx, lam, h0, reverse):
    B, T, W = xc.shape
    xf = xc.astype(jnp.float32)
    xb = xf.reshape(B, T, LRU_BLOCKS, LRU_BW)
    r = jax.nn.sigmoid(jnp.einsum('btnd,nde->btne', xb, wa.astype(jnp.float32)).reshape(B, T, W) + ba.astype(jnp.float32))
    i = jax.nn.sigmoid(jnp.einsum('btnd,nde->btne', xb, wx.astype(jnp.float32)).reshape(B, T, W) + bx.astype(jnp.float32))
    log_a = -LRU_C * r * jax.nn.softplus(-lam.astype(jnp.float32))
    a = jnp.exp(log_a)
    b = jnp.sqrt(-jnp.expm1(2.0 * log_a)) * (i * xf)
    if reverse:
        a, b = a[:, ::-1], b[:, ::-1]
    b = b.at[:, 0].add(a[:, 0] * h0.astype(jnp.float32))
    _, h = lax.associative_scan(_lin_combine, (a, b), axis=1)
    h_last = h[:, -1]
    if reverse:
        h = h[:, ::-1]
    return h.astype(xc.dtype), h_last.astype(xc.dtype)


def _lru_branch(lx, ly, p, h0f, h0b):
    xc = _dwconv(lx, p['lru_conv_w'], p['lru_conv_b'], LRU_CONV // 2, LRU_CONV - 1 - LRU_CONV // 2)
    hf, sf = _rglru(xc, p['lru_wa'][0], p['lru_ba'][0], p['lru_wx'][0], p['lru_bx'][0], p['lru_lam'][0], h0f, False)
    hb, sb = _rglru(xc, p['lru_wa'][1], p['lru_ba'][1], p['lru_wx'][1], p['lru_bx'][1], p['lru_lam'][1], h0b, True)
    y = (hf + hb) * jax.nn.gelu(ly)
    return y @ p['w_lru_out'], jnp.stack([sf, sb], axis=1)


def _attend(q, key_sets, scale):
    B, T, H, dq = q.shape
    nb = T // ATTN_QBLOCK
    qs = jnp.moveaxis(q.reshape(B, nb, ATTN_QBLOCK, H, dq), 1, 0)

    def one_block(qb):
        s = jnp.concatenate([jnp.einsum('bqhd,bkhd->bhqk', qb, k) for k, _ in key_sets], axis=-1)
        pr = jax.nn.softmax(s.astype(jnp.float32) * scale, axis=-1)
        o, off = 0, 0
        for k, v in key_sets:
            n = k.shape[1]
            o = o + jnp.einsum('bhqk,bkhd->bqhd', pr[..., off:off + n].astype(v.dtype), v)
            off += n
        return o

    out = lax.map(one_block, qs)
    dv = key_sets[0][1].shape[-1]
    return jnp.moveaxis(out, 0, 1).reshape(B, T, H * dv)


def _na_latent(q, k, v, ck, cv, rpb):
    B, T, H, hd = q.shape
    rows = T // GRID_W
    kr = min(NA_WR, rows)
    ncb = GRID_W // NA_QB
    r = jnp.arange(rows)
    row_idx = jnp.clip(r - kr // 2, 0, rows - kr)[:, None] + jnp.arange(kr)[None, :]
    cb = jnp.arange(ncb)
    col_idx = jnp.clip(cb * NA_QB - NA_WC // 2, 0, GRID_W - NA_HALO)[:, None] + jnp.arange(NA_HALO)[None, :]
    qcol = cb[:, None] * NA_QB + jnp.arange(NA_QB)[None, :]
    qstart = jnp.clip(qcol - NA_WC // 2, 0, GRID_W - NA_WC)
    kcol = col_idx[:, None, :]
    valid = (kcol >= qstart[:, :, None]) & (kcol < qstart[:, :, None] + NA_WC)
    rel_c = jnp.clip(kcol - qcol[:, :, None] + NA_WC - 1, 0, 2 * NA_WC - 2)
    rel_r = row_idx - r[:, None] + NA_WR - 1
    bias = rpb[:, rel_r[:, None, None, :, None], rel_c[None, :, :, None, :]]
    qg = q.reshape(B, rows, ncb, NA_QB, H, hd)
    kg = k.reshape(B, rows, GRID_W, H, hd)[:, :, col_idx][:, row_idx]
    vg = v.reshape(B, rows, GRID_W, H, hd)[:, :, col_idx][:, row_idx]
    scale = hd ** -0.5
    s_loc = jnp.einsum('brjqhd,brkjchd->bhrjqkc', qg, kg).astype(jnp.float32) * scale + bias[None].astype(jnp.float32)
    s_loc = jnp.where(valid[:, :, None, :][None, None, None], s_loc, NEG)
    n_loc = kr * NA_HALO
    s_loc = s_loc.reshape(B, H, rows, ncb, NA_QB, n_loc)
    s_ctx = jnp.einsum('brjqhd,bphd->bhrjqp', qg, ck).astype(jnp.float32) * scale
    pr = jax.nn.softmax(jnp.concatenate([s_loc, s_ctx], axis=-1), axis=-1)
    p_loc = pr[..., :n_loc].reshape(B, H, rows, ncb, NA_QB, kr, NA_HALO).astype(v.dtype)
    p_ctx = pr[..., n_loc:].astype(v.dtype)
    o = jnp.einsum('bhrjqkc,brkjchd->brjqhd', p_loc, vg) + jnp.einsum('bhrjqp,bphd->brjqhd', p_ctx, cv)
    return o.reshape(B, T, H * hd)


def _conformer(glu, p):
    a, g = jnp.split(glu, 2, axis=-1)
    z = _dwconv(a * jax.nn.sigmoid(g), p['cm_dw_w'], p['cm_dw_b'], CM_KERNEL // 2, CM_KERNEL // 2)
    z = jax.nn.silu(_layernorm(z, p['cm_ln_g'], p['cm_ln_b']))
    return z @ p['w_cm_out']


def _axial_rope_tables(T):
    t = jnp.arange(T)
    pos = jnp.stack([t // GRID_W, t % GRID_W], axis=-1).astype(jnp.float32)
    n_freq = MLA_ROPE // 4
    inv = ROPE_BASE ** (-jnp.arange(n_freq, dtype=jnp.float32) / n_freq)
    ang = pos[:, :, None] * inv
    return jnp.cos(ang), jnp.sin(ang)


def _apply_rope(x, cos, sin):
    xs = x.reshape(x.shape[:-1] + (2, 2, MLA_ROPE // 4))
    x1, x2 = xs[..., 0, :], xs[..., 1, :]
    c, s = cos.astype(x.dtype), sin.astype(x.dtype)
    return jnp.stack([x1 * c - x2 * s, x2 * c + x1 * s], axis=-2).reshape(x.shape)


def _mla_query(cq, p):
    B, T, _ = cq.shape
    q = (_rmsnorm(cq, p['mla_q_norm']) @ p['mla_wq_b']).reshape(B, T, MLA_HEADS, MLA_NOPE + MLA_ROPE)
    return q[..., :MLA_NOPE], q[..., MLA_NOPE:]


def _mla_expand(ckv_n, p):
    B, T, _ = ckv_n.shape
    kv = (ckv_n @ p['mla_wkv_b']).reshape(B, T, MLA_HEADS, MLA_NOPE + MLA_VD)
    return kv[..., :MLA_NOPE], kv[..., MLA_NOPE:]


def _mla_keys(k_nope, k_rope):
    kr = jnp.broadcast_to(k_rope[:, :, None, :], k_nope.shape[:3] + (MLA_ROPE,))
    return jnp.concatenate([k_nope, kr], axis=-1)


def _merge(ys, gates, w_out):
    g = jax.nn.sigmoid(gates).reshape(gates.shape[:-1] + (N_BRANCH, D_MODEL))
    y = g[..., 0, :] * ys[0] + g[..., 1, :] * ys[1] + g[..., 2, :] * ys[2] + g[..., 3, :] * ys[3]
    return y @ w_out


def _mixer_context(u, p):
    B, S, _ = u.shape
    lx, ly, qkv, glu, cq, ckv, kr, gates = _split_in(u @ p['w_in'])
    zeros = jnp.zeros((B, LRU_WIDTH), u.dtype)
    ya, lru_state = _lru_branch(lx, ly, p, zeros, zeros)
    qkv = qkv.reshape(B, S, 3, NA_HEADS, NA_HD)
    q, k, v = qkv[:, :, 0], qkv[:, :, 1], qkv[:, :, 2]
    yb = _attend(q, ((k, v),), NA_HD ** -0.5) @ p['w_na_out']
    yc = _conformer(glu, p)
    ckv_n = _rmsnorm(ckv, p['mla_kv_norm'])
    qn, qr = _mla_query(cq, p)
    kn, vm = _mla_expand(ckv_n, p)
    yd = _attend(jnp.concatenate([qn, qr], axis=-1), ((_mla_keys(kn, kr), vm),),
                 (MLA_NOPE + MLA_ROPE) ** -0.5) @ p['w_mla_out']
    y = _merge((ya, yb, yc, yd), gates, p['w_out'])
    return y, k, v, ckv_n, kr, lru_state


def _mixer_latent(u, p, na_k, na_v, mla_ckv, mla_krope, lru_state):
    B, T, _ = u.shape
    lx, ly, qkv, glu, cq, ckv, kr, gates = _split_in(u @ p['w_in'])
    ya, _ = _lru_branch(lx, ly, p, lru_state[:, 0], lru_state[:, 1])
    qkv = qkv.reshape(B, T, 3, NA_HEADS, NA_HD)
    yb = _na_latent(qkv[:, :, 0], qkv[:, :, 1], qkv[:, :, 2], na_k, na_v, p['na_rpb']) @ p['w_na_out']
    yc = _conformer(glu, p)
    cos, sin = _axial_rope_tables(T)
    qn, qr = _mla_query(cq, p)
    qr = _apply_rope(qr, cos[:, None], sin[:, None])
    kr = _apply_rope(kr, cos, sin)
    kn, vm = _mla_expand(_rmsnorm(ckv, p['mla_kv_norm']), p)
    kn_c, vm_c = _mla_expand(mla_ckv, p)
    yd = _attend(jnp.concatenate([qn, qr], axis=-1),
                 ((_mla_keys(kn, kr), vm), (_mla_keys(kn_c, mla_krope), vm_c)),
                 (MLA_NOPE + MLA_ROPE) ** -0.5) @ p['w_mla_out']
    return _merge((ya, yb, yc, yd), gates, p['w_out'])


def _moe(x, p):
    B, T, _ = x.shape
    gl = jnp.einsum('btd,dg->btg', x, p['moe_w_group']).astype(jnp.float32)
    gsel = jnp.argmax(gl, axis=-1)
    gp = jnp.take_along_axis(jax.nn.softmax(gl, axis=-1), gsel[..., None], axis=-1)
    el = jnp.einsum('btd,de->bte', x, p['moe_w_expert']).astype(jnp.float32).reshape(B, T, MOE_GROUPS, MOE_PER_GROUP)
    el = jnp.take_along_axis(el, gsel[..., None, None], axis=2)[:, :, 0]
    tp, ti = lax.top_k(jax.nn.softmax(el, axis=-1), MOE_TOPK)
    w = gp * tp / jnp.sum(tp, axis=-1, keepdims=True)
    eid = gsel[..., None] * MOE_PER_GROUP + ti
    gate = jnp.sum(jax.nn.one_hot(eid, MOE_EXPERTS, dtype=jnp.float32) * w[..., None], axis=-2).astype(x.dtype)
    h = jax.nn.silu(jnp.einsum('btd,edf->btef', x, p['moe_w1'])) * jnp.einsum('btd,edf->btef', x, p['moe_w3'])
    return jnp.einsum('btef,efd->btd', h * gate[..., None], p['moe_w2'])


def setup_inputs(seed: int = 0) -> dict:
    key = jax.random.key(seed)
    ks = iter(jax.random.split(key, 64))

    def nrm(shape, scale=1.0):
        return jax.random.normal(next(ks), shape, jnp.float32) * scale

    def gain(shape):
        return 1.0 + nrm(shape, 0.02)

    a0 = jax.random.uniform(next(ks), (DEPTH, 2, LRU_WIDTH), jnp.float32, 0.9, 0.999)
    return {
        'x_prompt': nrm((BATCH, SEQ, D_MODEL)),
        'x_sample': nrm((DEC_BATCH, DEC_SEQ, D_MODEL)),
        'cache_na_k': nrm((DEC_BATCH, DEPTH, PAST_LEN, NA_HEADS, NA_HD)),
        'cache_na_v': nrm((DEC_BATCH, DEPTH, PAST_LEN, NA_HEADS, NA_HD)),
        'cache_mla_ckv': nrm((DEC_BATCH, DEPTH, PAST_LEN, MLA_KV_LORA)),
        'cache_mla_krope': nrm((DEC_BATCH, DEPTH, PAST_LEN, MLA_ROPE)),
        'state_lru': nrm((DEC_BATCH, DEPTH, 2, LRU_WIDTH), 0.5),
        'c': nrm((DEC_BATCH, D_MODEL)),
        'c_ctx': nrm((D_MODEL,)),
        'w_ada': nrm((DEPTH, D_MODEL, 6 * D_MODEL), 0.5 * D_MODEL ** -0.5),
        'b_ada': nrm((DEPTH, 6 * D_MODEL), 0.02),
        'norm1_g': gain((DEPTH, D_MODEL)),
        'w_in': nrm((DEPTH, D_MODEL, D_IN), D_MODEL ** -0.5),
        'lru_conv_w': nrm((DEPTH, LRU_CONV, LRU_WIDTH), LRU_CONV ** -0.5),
        'lru_conv_b': nrm((DEPTH, LRU_WIDTH), 0.02),
        'lru_wa': nrm((DEPTH, 2, LRU_BLOCKS, LRU_BW, LRU_BW), LRU_BW ** -0.5),
        'lru_ba': nrm((DEPTH, 2, LRU_WIDTH), 0.02),
        'lru_wx': nrm((DEPTH, 2, LRU_BLOCKS, LRU_BW, LRU_BW), LRU_BW ** -0.5),
        'lru_bx': nrm((DEPTH, 2, LRU_WIDTH), 0.02),
        'lru_lam': jnp.log(a0) - jnp.log1p(-a0),
        'w_lru_out': nrm((DEPTH, LRU_WIDTH, D_MODEL), LRU_WIDTH ** -0.5),
        'na_rpb': nrm((DEPTH, NA_HEADS, 2 * NA_WR - 1, 2 * NA_WC - 1), 0.1),
        'w_na_out': nrm((DEPTH, NA_HEADS * NA_HD, D_MODEL), (NA_HEADS * NA_HD) ** -0.5),
        'cm_dw_w': nrm((DEPTH, CM_KERNEL, CM_WIDTH), CM_KERNEL ** -0.5),
        'cm_dw_b': nrm((DEPTH, CM_WIDTH), 0.02),
        'cm_ln_g': gain((DEPTH, CM_WIDTH)),
        'cm_ln_b': nrm((DEPTH, CM_WIDTH), 0.02),
        'w_cm_out': nrm((DEPTH, CM_WIDTH, D_MODEL), CM_WIDTH ** -0.5),
        'mla_q_norm': gain((DEPTH, MLA_Q_LORA)),
        'mla_wq_b': nrm((DEPTH, MLA_Q_LORA, MLA_HEADS * (MLA_NOPE + MLA_ROPE)), MLA_Q_LORA ** -0.5),
        'mla_kv_norm': gain((DEPTH, MLA_KV_LORA)),
        'mla_wkv_b': nrm((DEPTH, MLA_KV_LORA, MLA_HEADS * (MLA_NOPE + MLA_VD)), MLA_KV_LORA ** -0.5),
        'w_mla_out': nrm((DEPTH, MLA_HEADS * MLA_VD, D_MODEL), (MLA_HEADS * MLA_VD) ** -0.5),
        'w_out': nrm((DEPTH, D_MODEL, D_MODEL), D_MODEL ** -0.5),
        'norm2_g': gain((DEPTH, D_MODEL)),
        'moe_w_group': nrm((DEPTH, D_MODEL, MOE_GROUPS), D_MODEL ** -0.5),
        'moe_w_expert': nrm((DEPTH, D_MODEL, MOE_EXPERTS), D_MODEL ** -0.5),
        'moe_w1': nrm((DEPTH, MOE_EXPERTS, D_MODEL, MOE_FF), D_MODEL ** -0.5),
        'moe_w3': nrm((DEPTH, MOE_EXPERTS, D_MODEL, MOE_FF), D_MODEL ** -0.5),
        'moe_w2': nrm((DEPTH, MOE_EXPERTS, MOE_FF, D_MODEL), MOE_FF ** -0.5),
        'final_g': gain((D_MODEL,)),
    }


def reference(x_prompt, x_sample, cache_na_k, cache_na_v, cache_mla_ckv, cache_mla_krope, state_lru, c, c_ctx,
              w_ada, b_ada, norm1_g, w_in, lru_conv_w, lru_conv_b, lru_wa, lru_ba, lru_wx, lru_bx, lru_lam,
              w_lru_out, na_rpb, w_na_out, cm_dw_w, cm_dw_b, cm_ln_g, cm_ln_b, w_cm_out, mla_q_norm, mla_wq_b,
              mla_kv_norm, mla_wkv_b, w_mla_out, w_out, norm2_g, moe_w_group, moe_w_expert, moe_w1, moe_w3,
              moe_w2, final_g):
    hc = x_prompt
    hl = x_sample
    st_k, st_v, st_ckv, st_kr, st_lru = [], [], [], [], []
    for l in range(DEPTH):
        p = dict(w_in=w_in[l], lru_conv_w=lru_conv_w[l], lru_conv_b=lru_conv_b[l], lru_wa=lru_wa[l],
                 lru_ba=lru_ba[l], lru_wx=lru_wx[l], lru_bx=lru_bx[l], lru_lam=lru_lam[l], w_lru_out=w_lru_out[l],
                 na_rpb=na_rpb[l], w_na_out=w_na_out[l], cm_dw_w=cm_dw_w[l], cm_dw_b=cm_dw_b[l],
                 cm_ln_g=cm_ln_g[l], cm_ln_b=cm_ln_b[l], w_cm_out=w_cm_out[l], mla_q_norm=mla_q_norm[l],
                 mla_wq_b=mla_wq_b[l], mla_kv_norm=mla_kv_norm[l], mla_wkv_b=mla_wkv_b[l],
                 w_mla_out=w_mla_out[l], w_out=w_out[l], moe_w_group=moe_w_group[l],
                 moe_w_expert=moe_w_expert[l], moe_w1=moe_w1[l], moe_w3=moe_w3[l], moe_w2=moe_w2[l])
        sh1, sc1, g1, sh2, sc2, g2 = _modulation(c_ctx, w_ada[l], b_ada[l])
        u = _rmsnorm(hc, norm1_g[l]) * (1.0 + sc1) + sh1
        y, k_c, v_c, ckv_c, kr_c, lru_c = _mixer_context(u, p)
        hc = hc + g1 * y
        u = _rmsnorm(hc, norm2_g[l]) * (1.0 + sc2) + sh2
        hc = hc + g2 * _moe(u, p)
        st_k.append(k_c); st_v.append(v_c); st_ckv.append(ckv_c); st_kr.append(kr_c); st_lru.append(lru_c)
        sh1, sc1, g1, sh2, sc2, g2 = [m[:, None, :] for m in _modulation(c, w_ada[l], b_ada[l])]
        u = _rmsnorm(hl, norm1_g[l]) * (1.0 + sc1) + sh1
        y = _mixer_latent(u, p, cache_na_k[:, l], cache_na_v[:, l], cache_mla_ckv[:, l], cache_mla_krope[:, l],
                          state_lru[:, l])
        hl = hl + g1 * y
        u = _rmsnorm(hl, norm2_g[l]) * (1.0 + sc2) + sh2
        hl = hl + g2 * _moe(u, p)
    y_prompt = _rmsnorm(hc, final_g)
    y_sample = _rmsnorm(hl, final_g)
    new_na_k = jnp.stack(st_k, axis=1)
    new_na_v = jnp.stack(st_v, axis=1)
    new_mla_ckv = jnp.stack(st_ckv, axis=1)
    new_mla_krope = jnp.stack(st_kr, axis=1)
    new_lru_state = jnp.stack(st_lru, axis=1)
    return (y_prompt, y_sample, new_na_k, new_na_v, new_mla_ckv, new_mla_krope, new_lru_state)
```

```python
import functools
import math

import numpy as np
import jax
import jax.numpy as jnp
from jax import lax
from jax.experimental import pallas as pl
from jax.experimental.pallas import tpu as pltpu

F32 = jnp.float32
BF16 = jnp.bfloat16

D_MODEL = 1024
GRID_W = 64
EPS = 1e-6
NEG = -1e30
N_BRANCH = 4
LRU_WIDTH = 256
LRU_BLOCKS = 4
LRU_CONV = 4
LRU_C = 8.0
NA_HEADS = 4
NA_HD = 64
NA_WR = 8
NA_WC = 16
NA_QROWS = 4
NA_KROWS = NA_QROWS + NA_WR
CM_WIDTH = 256
CM_KERNEL = 31
MLA_HEADS = 4
MLA_Q_LORA = 256
MLA_KV_LORA = 128
MLA_NOPE = 64
MLA_ROPE = 32
MLA_VD = 64
MLA_KDIM = 256
ROPE_BASE = 10000.0
MOE_GROUPS = 4
MOE_PER_GROUP = 4
MOE_EXPERTS = 16
MOE_FF = 256
ROUTER_LANES = 128

TM_TOKEN = 512
TM_MOE = 1024
TQ_MLA = 256
TK_MLA = 512
SEQ_CHUNK = 256
VMEM_LIMIT = 56 * 1024 * 1024


def _cparams(sem, vmem=VMEM_LIMIT):
    return pltpu.CompilerParams(dimension_semantics=sem, vmem_limit_bytes=vmem)


def _sigmoid(x):
    return 1.0 / (1.0 + jnp.exp(-x))


def _silu(x):
    return x * _sigmoid(x)


def _gelu_tanh(x):
    c = math.sqrt(2.0 / math.pi)
    return x * (0.5 * (1.0 + jnp.tanh(c * (x + 0.044715 * (x * x * x)))))


def _softplus(x):
    return jnp.maximum(x, 0.0) + jnp.log1p(jnp.exp(-jnp.abs(x)))


def _rms(x, g):
    return x * lax.rsqrt(jnp.mean(x * x, axis=-1, keepdims=True) + EPS) * g


def _norm_mod(h, g, scale, shift):
    return _rms(h, g) * (1.0 + scale) + shift


def _dot(a, b):
    return jnp.dot(a, b, preferred_element_type=F32)


def _dot_nt(a, b):
    return lax.dot_general(a, b, (((1,), (1,)), ((), ())), preferred_element_type=F32)


def _mod_kernel(c_ref, w_ref, b_ref, o_ref):
    s = _silu(c_ref[...])
    o_ref[0] = jnp.dot(s, w_ref[0], preferred_element_type=F32, precision=lax.Precision.HIGHEST) + b_ref[0]


def _modulation(cvec, w_ada, b_ada):
    depth, d, n = w_ada.shape
    rows = cvec.shape[0]
    tn = 1024
    return pl.pallas_call(
        _mod_kernel,
        grid=(depth, n // tn),
        in_specs=[pl.BlockSpec((rows, d), lambda l, j: (0, 0)),
                  pl.BlockSpec((1, d, tn), lambda l, j: (l, 0, j)),
                  pl.BlockSpec((1, 1, tn), lambda l, j: (l, 0, j))],
        out_specs=pl.BlockSpec((1, rows, tn), lambda l, j: (l, 0, j)),
        out_shape=jax.ShapeDtypeStruct((depth, rows, n), F32),
        compiler_params=_cparams(("parallel", "parallel")),
        name="modulation",
    )(cvec, w_ada, b_ada.reshape(depth, 1, n))


def _mod_row_map(n_ctx_tiles, tiles_per_lat):
    def index_map(i, *_):
        row = jnp.maximum(i - n_ctx_tiles, 0) // tiles_per_lat + (i >= n_ctx_tiles).astype(jnp.int32)
        return (row, 0, 0)
    return index_map


_IN_SEGS = (("lx", 256), ("ly", 256), ("q", 256), ("k", 256), ("v", 256), ("ga", 256), ("gg", 256),
            ("cq", 256), ("ckv", 128), ("krp", 128), ("krs", 128))


def _in_kernel(h_ref, mod_ref, g_ref, w_ref, *out_refs):
    mod = mod_ref[0]
    u = _norm_mod(h_ref[...], g_ref[...], mod[:, D_MODEL:2 * D_MODEL], mod[:, 0:D_MODEL]).astype(BF16)
    off = 0
    for (_, width), o_ref in zip(_IN_SEGS, out_refs):
        o_ref[...] = _dot(u, w_ref[:, off:off + width]).astype(o_ref.dtype)
        off += width


def _in_proj(h, mods, g, w_small, n_ctx_tiles, tiles_per_lat):
    t, d = h.shape
    tm = TM_TOKEN
    n = w_small.shape[1]
    out_shape = [jax.ShapeDtypeStruct((t, width), F32) for _, width in _IN_SEGS]
    out_specs = [pl.BlockSpec((tm, width), lambda i: (i, 0)) for _, width in _IN_SEGS]
    outs = pl.pallas_call(
        _in_kernel,
        grid=(t // tm,),
        in_specs=[pl.BlockSpec((tm, d), lambda i: (i, 0)),
                  pl.BlockSpec((1, 1, mods.shape[-1]), _mod_row_map(n_ctx_tiles, tiles_per_lat)),
                  pl.BlockSpec((1, d), lambda i: (0, 0)),
                  pl.BlockSpec((d, n), lambda i: (0, 0))],
        out_specs=out_specs,
        out_shape=out_shape,
        compiler_params=_cparams(("parallel",)),
        name="in_proj",
    )(h, mods, g, w_small)
    return dict(zip([s for s, _ in _IN_SEGS], outs))


def _shifted_windows(xw, offsets, length):
    n = xw.shape[0]
    rolled = {0: xw}
    out = {}
    for o in offsets:
        r = o % 8
        if r not in rolled:
            rolled[r] = pltpu.roll(xw, n - r, 0)
        base = o - r
        out[o] = rolled[r][base:base + length]
    return out


def _lru_kernel(lx_ref, ly_ref, h0_ref, cw_ref, cb_ref, wa_ref, ba_ref, wx_ref, bx_ref, lam_ref,
                pa_ref, st_ref, pad_ref, af_ref, bf_ref, ab_ref, bb_ref):
    t = lx_ref.shape[0]
    ch = min(SEQ_CHUNK, t)
    halo = 8
    zeros = jnp.zeros((halo, LRU_WIDTH), F32)
    pad_ref[0:halo, :] = zeros
    pad_ref[halo + t:2 * halo + t, :] = zeros
    pad_ref[halo:halo + t, :] = lx_ref[...]
    pad_l = LRU_CONV // 2
    taps = [halo - pad_l + k for k in range(LRU_CONV)]
    a_refs = (af_ref, ab_ref)
    b_refs = (bf_ref, bb_ref)

    def gates(c, carry):
        r0 = pl.multiple_of(c * ch, ch)
        xw = pad_ref[pl.ds(r0, ch + 2 * halo), :]
        win = _shifted_windows(xw, taps, ch)
        xc = cb_ref[...] + sum(cw_ref[k:k + 1, :] * win[taps[k]] for k in range(LRU_CONV))
        xcb = xc.astype(BF16)
        for d in range(2):
            r = _sigmoid(_dot(xcb, wa_ref[d]) + ba_ref[d])
            i = _sigmoid(_dot(xcb, wx_ref[d]) + bx_ref[d])
            log_a = (-LRU_C) * r * _softplus(-lam_ref[d])
            a = jnp.exp(log_a)
            one_minus_a2 = -jnp.tanh(log_a) * (a * a + 1.0)
            a_refs[d][pl.ds(r0, ch), :] = a
            b_refs[d][pl.ds(r0, ch), :] = jnp.sqrt(one_minus_a2) * (i * xc)
        return carry

    lax.fori_loop(0, t // ch, gates, 0)

    def scan(s, carry):
        hf, hb = carry
        tf = s
        tb = t - 1 - s
        hf = af_ref[pl.ds(tf, 1), :] * hf + bf_ref[pl.ds(tf, 1), :]
        bf_ref[pl.ds(tf, 1), :] = hf
        hb = ab_ref[pl.ds(tb, 1), :] * hb + bb_ref[pl.ds(tb, 1), :]
        bb_ref[pl.ds(tb, 1), :] = hb
        return hf, hb

    hf, hb = lax.fori_loop(0, t, scan, (h0_ref[0, 0:1, :], h0_ref[0, 1:2, :]), unroll=8)
    st_ref[0, 0:1, :] = hf
    st_ref[0, 1:2, :] = hb

    def emit(c, carry):
        r0 = pl.multiple_of(c * ch, ch)
        y = (bf_ref[pl.ds(r0, ch), :] + bb_ref[pl.ds(r0, ch), :]) * _gelu_tanh(ly_ref[pl.ds(r0, ch), :])
        pa_ref[pl.ds(r0, ch), :] = y.astype(pa_ref.dtype)
        return carry

    lax.fori_loop(0, t // ch, emit, 0)


def _lru_branch(lx, ly, h0, p, seq, blk0, nb):
    w = LRU_WIDTH
    const2 = lambda b: (0, 0)
    const3 = lambda b: (0, 0, 0)
    return pl.pallas_call(
        _lru_kernel,
        grid=(nb,),
        in_specs=[pl.BlockSpec((seq, w), lambda b: (blk0 + b, 0)),
                  pl.BlockSpec((seq, w), lambda b: (blk0 + b, 0)),
                  pl.BlockSpec((1, 2, w), lambda b: (b, 0, 0)),
                  pl.BlockSpec((LRU_CONV, w), const2),
                  pl.BlockSpec((1, w), const2),
                  pl.BlockSpec((2, w, w), const3),
                  pl.BlockSpec((2, 1, w), const3),
                  pl.BlockSpec((2, w, w), const3),
                  pl.BlockSpec((2, 1, w), const3),
                  pl.BlockSpec((2, 1, w), const3)],
        out_specs=[pl.BlockSpec((seq, w), lambda b: (b, 0)),
                   pl.BlockSpec((1, 2, w), lambda b: (b, 0, 0))],
        out_shape=[jax.ShapeDtypeStruct((nb * seq, w), BF16),
                   jax.ShapeDtypeStruct((nb, 2, w), F32)],
        scratch_shapes=[pltpu.VMEM((seq + 16, w), F32)] + [pltpu.VMEM((seq, w), F32)] * 4,
        compiler_params=_cparams(("parallel",)),
        name="rglru",
    )(lx, ly, h0, p["lru_conv_w"], p["lru_conv_b"], p["lru_wa"], p["lru_ba"], p["lru_wx"], p["lru_bx"],
      p["lru_lam"])


def _cm_kernel(ga_ref, gg_ref, w_ref, b_ref, lg_ref, lb_ref, pc_ref, pad_ref):
    t = ga_ref.shape[0]
    ch = min(SEQ_CHUNK, t)
    halo = 16
    zeros = jnp.zeros((halo, CM_WIDTH), F32)
    pad_ref[0:halo, :] = zeros
    pad_ref[halo + t:2 * halo + t, :] = zeros

    def glu(c, carry):
        r0 = pl.multiple_of(c * ch, ch)
        pad_ref[pl.ds(halo + r0, ch), :] = ga_ref[pl.ds(r0, ch), :] * _sigmoid(gg_ref[pl.ds(r0, ch), :])
        return carry

    lax.fori_loop(0, t // ch, glu, 0)
    taps = [halo - CM_KERNEL // 2 + k for k in range(CM_KERNEL)]

    def conv(c, carry):
        r0 = pl.multiple_of(c * ch, ch)
        xw = pad_ref[pl.ds(r0, ch + 2 * halo), :]
        win = _shifted_windows(xw, taps, ch)
        z = b_ref[...] + sum(w_ref[k:k + 1, :] * win[taps[k]] for k in range(CM_KERNEL))
        mu = jnp.mean(z, axis=-1, keepdims=True)
        zc = z - mu
        var = jnp.mean(zc * zc, axis=-1, keepdims=True)
        y = zc * lax.rsqrt(var + EPS) * lg_ref[...] + lb_ref[...]
        pc_ref[pl.ds(r0, ch), :] = _silu(y).astype(pc_ref.dtype)
        return carry

    lax.fori_loop(0, t // ch, conv, 0)


def _conformer_branch(ga, gg, p, seq, blk0, nb):
    w = CM_WIDTH
    const2 = lambda b: (0, 0)
    return pl.pallas_call(
        _cm_kernel,
        grid=(nb,),
        in_specs=[pl.BlockSpec((seq, w), lambda b: (blk0 + b, 0)),
                  pl.BlockSpec((seq, w), lambda b: (blk0 + b, 0)),
                  pl.BlockSpec((CM_KERNEL, w), const2),
                  pl.BlockSpec((1, w), const2),
                  pl.BlockSpec((1, w), const2),
                  pl.BlockSpec((1, w), const2)],
        out_specs=pl.BlockSpec((seq, w), lambda b: (b, 0)),
        out_shape=jax.ShapeDtypeStruct((nb * seq, w), BF16),
        scratch_shapes=[pltpu.VMEM((seq + 32, w), F32)],
        compiler_params=_cparams(("parallel",)),
        name="conformer",
    )(ga, gg, p["cm_dw_w"], p["cm_dw_b"], p["cm_ln_g"], p["cm_ln_b"])


def _softmax_pv(scores, values):
    m = functools.reduce(jnp.maximum, [jnp.max(s, axis=-1, keepdims=True) for s in scores])
    ps = [jnp.exp(s - m) for s in scores]
    l = sum(jnp.sum(p, axis=-1, keepdims=True) for p in ps)
    o = sum(_dot(p.astype(BF16), v) for p, v in zip(ps, values))
    return o / l


def _ctx_attn_kernel(q_ref, k_ref, v_ref, cq_ref, ckv_ref, krp_ref, qn_ref, wq_ref, kvn_ref, wkv_ref,
                     pb_ref, pd_ref, ckvn_ref):
    q = q_ref[...].astype(BF16)
    k = k_ref[...].astype(BF16)
    v = v_ref[...].astype(BF16)
    na_scale = NA_HD ** -0.5
    for h in range(NA_HEADS):
        sl = slice(h * NA_HD, (h + 1) * NA_HD)
        s = _dot_nt(q[:, sl], k[:, sl]) * na_scale
        pb_ref[:, sl] = _softmax_pv([s], [v[:, sl]]).astype(pb_ref.dtype)

    ckv_n = _rms(ckv_ref[...], kvn_ref[...])
    ckvn_ref[...] = ckv_n
    qf = _dot(_rms(cq_ref[...], qn_ref[...]).astype(BF16), wq_ref[...]).astype(BF16)
    kv = _dot(ckv_n.astype(BF16), wkv_ref[...]).astype(BF16)
    kr = krp_ref[:, 0:MLA_ROPE].astype(BF16)
    mla_scale = (MLA_NOPE + MLA_ROPE) ** -0.5
    qd = MLA_NOPE + MLA_ROPE
    kd = MLA_NOPE + MLA_VD
    for h in range(MLA_HEADS):
        qn = qf[:, h * qd:h * qd + MLA_NOPE]
        qr = qf[:, h * qd + MLA_NOPE:(h + 1) * qd]
        kn = kv[:, h * kd:h * kd + MLA_NOPE]
        vm = kv[:, h * kd + MLA_NOPE:(h + 1) * kd]
        s = (_dot_nt(qn, kn) + _dot_nt(qr, kr)) * mla_scale
        pd_ref[:, h * MLA_VD:(h + 1) * MLA_VD] = _softmax_pv([s], [vm]).astype(pd_ref.dtype)


def _ctx_attention(z, p, seq, nb):
    const2 = lambda b: (0, 0)
    row = lambda b: (b, 0)
    return pl.pallas_call(
        _ctx_attn_kernel,
        grid=(nb,),
        in_specs=[pl.BlockSpec((seq, 256), row), pl.BlockSpec((seq, 256), row), pl.BlockSpec((seq, 256), row),
                  pl.BlockSpec((seq, MLA_Q_LORA), row), pl.BlockSpec((seq, MLA_KV_LORA), row),
                  pl.BlockSpec((seq, 128), row),
                  pl.BlockSpec((1, MLA_Q_LORA), const2),
                  pl.BlockSpec(p["mla_wq_b"].shape, const2),
                  pl.BlockSpec((1, MLA_KV_LORA), const2),
                  pl.BlockSpec(p["mla_wkv_b"].shape, const2)],
        out_specs=[pl.BlockSpec((seq, 256), row), pl.BlockSpec((seq, 256), row),
                   pl.BlockSpec((seq, MLA_KV_LORA), row)],
        out_shape=[jax.ShapeDtypeStruct((nb * seq, 256), BF16),
                   jax.ShapeDtypeStruct((nb * seq, 256), BF16),
                   jax.ShapeDtypeStruct((nb * seq, MLA_KV_LORA), F32)],
        compiler_params=_cparams(("parallel",)),
        name="ctx_attention",
    )(z["q"], z["k"], z["v"], z["cq"], z["ckv"], z["krp"], p["mla_q_norm"], p["mla_wq_b"],
      p["mla_kv_norm"], p["mla_wkv_b"])


def _na_bias_tables(rpb, rows):
    n_steps = rows // NA_QROWS
    tables = []
    for step in (0, 1, n_steps - 1):
        start = int(np.clip(NA_QROWS * step - NA_WR // 2, 0, rows - NA_KROWS))
        qr = NA_QROWS * step + np.arange(NA_QROWS)
        kr = start + np.arange(NA_KROWS)
        rs = np.clip(qr - NA_WR // 2, 0, rows - NA_WR)
        row_ok = (kr[None, :] >= rs[:, None]) & (kr[None, :] < rs[:, None] + NA_WR)
        rel_r = np.clip(kr[None, :] - qr[:, None] + NA_WR - 1, 0, 2 * NA_WR - 2)
        qc = np.arange(GRID_W)
        kc = np.arange(GRID_W)
        cs = np.clip(qc - NA_WC // 2, 0, GRID_W - NA_WC)
        col_ok = (kc[None, :] >= cs[:, None]) & (kc[None, :] < cs[:, None] + NA_WC)
        rel_c = np.clip(kc[None, :] - qc[:, None] + NA_WC - 1, 0, 2 * NA_WC - 2)
        ok = row_ok[:, None, :, None] & col_ok[None, :, None, :]
        rr = np.broadcast_to(rel_r[:, None, :, None], ok.shape)
        cc = np.broadcast_to(rel_c[None, :, None, :], ok.shape)
        bias = rpb[:, rr, cc]
        bias = jnp.where(ok[None], bias, NEG)
        tables.append(bias.reshape(rpb.shape[0], NA_QROWS * GRID_W, NA_KROWS * GRID_W))
    return jnp.stack(tables, axis=0)


def _na_lat_kernel(q_ref, k_ref, v_ref, ck_ref, cv_ref, bias_ref, pb_ref):
    i = pl.program_id(1)
    rows = k_ref.shape[0] // GRID_W
    start = jnp.clip(NA_QROWS * i - NA_WR // 2, 0, rows - NA_KROWS) * GRID_W
    start = pl.multiple_of(start, GRID_W)
    nk = NA_KROWS * GRID_W
    q = q_ref[...].astype(BF16)
    kw = k_ref[pl.ds(start, nk), :].astype(BF16)
    vw = v_ref[pl.ds(start, nk), :].astype(BF16)
    ck = ck_ref[0].astype(BF16)
    cv = cv_ref[0].astype(BF16)
    scale = NA_HD ** -0.5
    for h in range(NA_HEADS):
        sl = slice(h * NA_HD, (h + 1) * NA_HD)
        s_loc = _dot_nt(q[:, sl], kw[:, sl]) * scale + bias_ref[0, h]
        s_ctx = _dot_nt(q[:, sl], ck[:, sl]) * scale
        pb_ref[:, sl] = _softmax_pv([s_loc, s_ctx], [vw[:, sl], cv[:, sl]]).astype(pb_ref.dtype)


def _na_latent(z, cache_k, cache_v, bias, seq, blk0, nb):
    tq = NA_QROWS * GRID_W
    steps = seq // tq
    nk = NA_KROWS * GRID_W
    past = cache_k.shape[1]

    def bias_map(b, i):
        return (jnp.where(i == 0, 0, jnp.where(i == steps - 1, 2, 1)), 0, 0, 0)

    return pl.pallas_call(
        _na_lat_kernel,
        grid=(nb, steps),
        in_specs=[pl.BlockSpec((tq, 256), lambda b, i: (blk0 * steps + b * steps + i, 0)),
                  pl.BlockSpec((seq, 256), lambda b, i: (blk0 + b, 0)),
                  pl.BlockSpec((seq, 256), lambda b, i: (blk0 + b, 0)),
                  pl.BlockSpec((1, past, 256), lambda b, i: (b, 0, 0)),
                  pl.BlockSpec((1, past, 256), lambda b, i: (b, 0, 0)),
                  pl.BlockSpec((1, NA_HEADS, tq, nk), bias_map)],
        out_specs=pl.BlockSpec((tq, 256), lambda b, i: (b * steps + i, 0)),
        out_shape=jax.ShapeDtypeStruct((nb * seq, 256), BF16),
        compiler_params=_cparams(("parallel", "arbitrary")),
        name="na_latent",
    )(z["q"], z["k"], z["v"], cache_k, cache_v, bias)


def _rope_tables(seq):
    t = np.arange(seq)
    pos = np.stack([t // GRID_W, t % GRID_W], axis=-1).astype(np.float32)
    n_freq = MLA_ROPE // 4
    inv = jnp.asarray(ROPE_BASE, F32) ** (-jnp.arange(n_freq, dtype=F32) / n_freq)
    ang = jnp.asarray(pos)[:, :, None] * inv
    cos, sin = jnp.cos(ang), jnp.sin(ang)
    c = jnp.stack([cos, cos], axis=2).reshape(seq, MLA_ROPE)
    s = jnp.stack([-sin, sin], axis=2).reshape(seq, MLA_ROPE)
    pad = ((0, 0), (0, 128 - MLA_ROPE))
    return jnp.pad(c, pad), jnp.pad(s, pad)


def _swap_rope_halves(w):
    w4 = w.reshape(w.shape[:-1] + (2, 2, MLA_ROPE // 4))
    return w4[..., ::-1, :].reshape(w.shape)


def _mla_prep_kernel(cq_ref, ckv_ref, krp_ref, krs_ref, cos_ref, sin_ref, qn_ref, wqn_ref, wqr_ref, wqs_ref,
                     kvn_ref, wkt_ref, qm_ref, kl_ref):
    tq = cq_ref.shape[0]
    cos = cos_ref[...]
    sin = sin_ref[...]
    ckv_n = _rms(ckv_ref[...], kvn_ref[...])
    kl_ref[0, :, 0:128] = ckv_n.astype(kl_ref.dtype)
    kl_ref[0, :, 128:256] = (krp_ref[...] * cos + krs_ref[...] * sin).astype(kl_ref.dtype)
    cqn = _rms(cq_ref[...], qn_ref[...]).astype(BF16)
    qn = _dot(cqn, wqn_ref[...]).astype(BF16)
    scale = (MLA_NOPE + MLA_ROPE) ** -0.5
    for h in range(MLA_HEADS):
        qa = _dot(qn[:, h * MLA_NOPE:(h + 1) * MLA_NOPE], wkt_ref[h])
        qr = _dot(cqn, wqr_ref[h]) * cos + _dot(cqn, wqs_ref[h]) * sin
        qm_ref[0, 0, h * tq:(h + 1) * tq, 0:128] = (qa * scale).astype(qm_ref.dtype)
        qm_ref[0, 0, h * tq:(h + 1) * tq, 128:256] = (qr * scale).astype(qm_ref.dtype)


def _mla_prep(z, cos, sin, p, seq, blk0, nb):
    tq = TQ_MLA
    steps = seq // tq
    tok = lambda b, i: (blk0 * steps + b * steps + i, 0)
    pos = lambda b, i: (i, 0)
    const2 = lambda b, i: (0, 0)
    const3 = lambda b, i: (0, 0, 0)
    return pl.pallas_call(
        _mla_prep_kernel,
        grid=(nb, steps),
        in_specs=[pl.BlockSpec((tq, MLA_Q_LORA), tok), pl.BlockSpec((tq, MLA_KV_LORA), tok),
                  pl.BlockSpec((tq, 128), tok), pl.BlockSpec((tq, 128), tok),
                  pl.BlockSpec((tq, 128), pos), pl.BlockSpec((tq, 128), pos),
                  pl.BlockSpec((1, MLA_Q_LORA), const2),
                  pl.BlockSpec((MLA_Q_LORA, MLA_HEADS * MLA_NOPE), const2),
                  pl.BlockSpec((MLA_HEADS, MLA_Q_LORA, 128), const3),
                  pl.BlockSpec((MLA_HEADS, MLA_Q_LORA, 128), const3),
                  pl.BlockSpec((1, MLA_KV_LORA), const2),
                  pl.BlockSpec((MLA_HEADS, MLA_NOPE, MLA_KV_LORA), const3)],
        out_specs=[pl.BlockSpec((1, 1, MLA_HEADS * tq, MLA_KDIM), lambda b, i: (b, i, 0, 0)),
                   pl.BlockSpec((1, tq, MLA_KDIM), lambda b, i: (b, i, 0))],
        out_shape=[jax.ShapeDtypeStruct((nb, steps, MLA_HEADS * tq, MLA_KDIM), BF16),
                   jax.ShapeDtypeStruct((nb, seq, MLA_KDIM), BF16)],
        compiler_params=_cparams(("parallel", "parallel")),
        name="mla_prep",
    )(z["cq"], z["ckv"], z["krp"], z["krs"], cos, sin, p["mla_q_norm"], p["mla_wqn"], p["mla_wqr"],
      p["mla_wqs"], p["mla_kv_norm"], p["mla_wkt"])


def _mla_lat_kernel(qm_ref, kl_ref, kc_ref, wv_ref, pd_ref, m_ref, l_ref, acc_ref):
    q = qm_ref[0, 0]
    rows = q.shape[0]
    tq = rows // MLA_HEADS
    m_ref[...] = jnp.full((rows, 1), NEG, F32)
    l_ref[...] = jnp.zeros((rows, 1), F32)
    acc_ref[...] = jnp.zeros((rows, MLA_KV_LORA), F32)

    def update(kblk):
        s = _dot_nt(q, kblk)
        m_old = m_ref[...]
        m_new = jnp.maximum(m_old, jnp.max(s, axis=-1, keepdims=True))
        alpha = jnp.exp(m_old - m_new)
        p = jnp.exp(s - m_new)
        l_ref[...] = alpha * l_ref[...] + jnp.sum(p, axis=-1, keepdims=True)
        acc_ref[...] = alpha * acc_ref[...] + _dot(p.astype(BF16), kblk[:, 0:MLA_KV_LORA])
        m_ref[...] = m_new

    def body(c, carry):
        update(kl_ref[0, pl.ds(pl.multiple_of(c * TK_MLA, TK_MLA), TK_MLA), :])
        return carry

    lax.fori_loop(0, kl_ref.shape[1] // TK_MLA, body, 0)
    update(kc_ref[0])
    o = (acc_ref[...] / l_ref[...]).astype(BF16)
    for h in range(MLA_HEADS):
        pd_ref[:, h * MLA_VD:(h + 1) * MLA_VD] = _dot(o[h * tq:(h + 1) * tq], wv_ref[h]).astype(pd_ref.dtype)


def _mla_latent(qm, kl, kc, wv, seq, nb):
    tq = TQ_MLA
    steps = seq // tq
    rows = MLA_HEADS * tq
    past = kc.shape[1]
    return pl.pallas_call(
        _mla_lat_kernel,
        grid=(nb, steps),
        in_specs=[pl.BlockSpec((1, 1, rows, MLA_KDIM), lambda b, i: (b, i, 0, 0)),
                  pl.BlockSpec((1, seq, MLA_KDIM), lambda b, i: (b, 0, 0)),
                  pl.BlockSpec((1, past, MLA_KDIM), lambda b, i: (b, 0, 0)),
                  pl.BlockSpec((MLA_HEADS, MLA_KV_LORA, MLA_VD), lambda b, i: (0, 0, 0))],
        out_specs=pl.BlockSpec((tq, MLA_HEADS * MLA_VD), lambda b, i: (b * steps + i, 0)),
        out_shape=jax.ShapeDtypeStruct((nb * seq, MLA_HEADS * MLA_VD), BF16),
        scratch_shapes=[pltpu.VMEM((rows, 1), F32), pltpu.VMEM((rows, 1), F32),
                        pltpu.VMEM((rows, MLA_KV_LORA), F32)],
        compiler_params=_cparams(("parallel", "arbitrary")),
        name="mla_latent",
    )(qm, kl, kc, wv)


def _merge_kernel(h_ref, mod_ref, g_ref, pa_ref, pb_ref, pc_ref, pd_ref, wg_ref, wb_ref, wo_ref, o_ref):
    mod = mod_ref[0]
    h = h_ref[...]
    u = _norm_mod(h, g_ref[...], mod[:, D_MODEL:2 * D_MODEL], mod[:, 0:D_MODEL]).astype(BF16)
    y = None
    for j, p_ref in enumerate((pa_ref, pb_ref, pc_ref, pd_ref)):
        gate = _sigmoid(_dot(u, wg_ref[:, j * D_MODEL:(j + 1) * D_MODEL]))
        term = gate * _dot(p_ref[...], wb_ref[j])
        y = term if y is None else y + term
    out = _dot(y.astype(BF16), wo_ref[...])
    o_ref[...] = h + mod[:, 2 * D_MODEL:3 * D_MODEL] * out


def _merge(h, mods, g, pa, pb, pc, pd, p, n_ctx_tiles, tiles_per_lat):
    t, d = h.shape
    tm = TM_TOKEN
    row = lambda i: (i, 0)
    const2 = lambda i: (0, 0)
    return pl.pallas_call(
        _merge_kernel,
        grid=(t // tm,),
        in_specs=[pl.BlockSpec((tm, d), row),
                  pl.BlockSpec((1, 1, mods.shape[-1]), _mod_row_map(n_ctx_tiles, tiles_per_lat)),
                  pl.BlockSpec((1, d), const2),
                  pl.BlockSpec((tm, 256), row), pl.BlockSpec((tm, 256), row),
                  pl.BlockSpec((tm, 256), row), pl.BlockSpec((tm, 256), row),
                  pl.BlockSpec((d, N_BRANCH * d), const2),
                  pl.BlockSpec((N_BRANCH, 256, d), lambda i: (0, 0, 0)),
                  pl.BlockSpec((d, d), const2)],
        out_specs=pl.BlockSpec((tm, d), row),
        out_shape=jax.ShapeDtypeStruct((t, d), F32),
        compiler_params=_cparams(("parallel",)),
        name="merge",
    )(h, mods, g, pa, pb, pc, pd, p["w_gates"], p["w_branch_out"], p["w_out"])


def _router_gates(logits):
    lane_i = lax.broadcasted_iota(jnp.int32, logits.shape, 1)
    lane = lane_i.astype(F32)
    ninf = jnp.float32(-jnp.inf)

    def first_argmax(x):
        m = jnp.max(x, axis=-1, keepdims=True)
        idx = jnp.min(jnp.where(x == m, lane, jnp.float32(1e9)), axis=-1, keepdims=True)
        return m, idx

    gl = jnp.where(lane_i < MOE_GROUPS, logits, ninf)
    gmax, gsel = first_argmax(gl)
    gp = 1.0 / jnp.sum(jnp.exp(gl - gmax), axis=-1, keepdims=True)
    e_idx = lane_i - MOE_GROUPS
    e_group = lax.shift_right_arithmetic(e_idx, jnp.full_like(e_idx, 2)).astype(F32)
    in_group = (e_idx >= 0) & (e_idx < MOE_EXPERTS) & (e_group == gsel)
    el = jnp.where(in_group, logits, ninf)
    m1, i1 = first_argmax(el)
    m2, i2 = first_argmax(jnp.where(lane == i1, ninf, el))
    e2 = jnp.exp(m2 - m1)
    w1 = gp / (1.0 + e2)
    w2 = gp * e2 / (1.0 + e2)
    return jnp.where(lane == i1, w1, 0.0) + jnp.where(lane == i2, w2, 0.0)


def _moe_kernel(h_ref, mod_ref, g_ref, wr_ref, w13_ref, w2_ref, o_ref, u_ref, gate_ref, acc_ref):
    e = pl.program_id(1)
    mod = mod_ref[0]

    @pl.when(e == 0)
    def _():
        u = _norm_mod(h_ref[...], g_ref[...], mod[:, 4 * D_MODEL:5 * D_MODEL], mod[:, 3 * D_MODEL:4 * D_MODEL])
        ub = u.astype(BF16)
        u_ref[...] = ub
        gate_ref[...] = _router_gates(_dot(ub, wr_ref[...]))
        acc_ref[...] = jnp.zeros_like(acc_ref)

    ub = u_ref[...]
    lane = lax.broadcasted_iota(jnp.int32, gate_ref.shape, 1)
    gate = jnp.sum(jnp.where(lane == e + MOE_GROUPS, gate_ref[...], 0.0), axis=-1, keepdims=True)
    h13 = _dot(ub, w13_ref[0])
    hid = _silu(h13[:, 0:MOE_FF]) * h13[:, MOE_FF:2 * MOE_FF] * gate
    acc_ref[...] += _dot(hid.astype(BF16), w2_ref[0])

    @pl.when(e == MOE_EXPERTS - 1)
    def _():
        o_ref[...] = h_ref[...] + mod[:, 5 * D_MODEL:6 * D_MODEL] * acc_ref[...]


def _moe(h, mods, g, p, n_ctx_tiles, tiles_per_lat):
    t, d = h.shape
    tm = TM_MOE
    nct = n_ctx_tiles * TM_TOKEN // tm
    tpl = tiles_per_lat * TM_TOKEN // tm
    return pl.pallas_call(
        _moe_kernel,
        grid=(t // tm, MOE_EXPERTS),
        in_specs=[pl.BlockSpec((tm, d), lambda i, e: (i, 0)),
                  pl.BlockSpec((1, 1, mods.shape[-1]), _mod_row_map(nct, tpl)),
                  pl.BlockSpec((1, d), lambda i, e: (0, 0)),
                  pl.BlockSpec((d, ROUTER_LANES), lambda i, e: (0, 0)),
                  pl.BlockSpec((1, d, 2 * MOE_FF), lambda i, e: (e, 0, 0)),
                  pl.BlockSpec((1, MOE_FF, d), lambda i, e: (e, 0, 0))],
        out_specs=pl.BlockSpec((tm, d), lambda i, e: (i, 0)),
        out_shape=jax.ShapeDtypeStruct((t, d), F32),
        scratch_shapes=[pltpu.VMEM((tm, d), BF16), pltpu.VMEM((tm, ROUTER_LANES), F32),
                        pltpu.VMEM((tm, d), F32)],
        compiler_params=_cparams(("parallel", "arbitrary")),
        name="moe",
    )(h, mods, g, p["w_router"], p["w13"], p["w2"])


def _final_kernel(h_ref, g_ref, o_ref):
    o_ref[...] = _rms(h_ref[...], g_ref[...])


def _final_norm(h, g, blk0, rows):
    d = h.shape[1]
    tm = TM_TOKEN
    return pl.pallas_call(
        _final_kernel,
        grid=(rows // tm,),
        in_specs=[pl.BlockSpec((tm, d), lambda i: (blk0 + i, 0)), pl.BlockSpec((1, d), lambda i: (0, 0))],
        out_specs=pl.BlockSpec((tm, d), lambda i: (i, 0)),
        out_shape=jax.ShapeDtypeStruct((rows, d), F32),
        compiler_params=_cparams(("parallel",)),
        name="final_norm",
    )(h, g)


def _block_diag(w):
    nd, nb, bw, _ = w.shape
    eye = jnp.eye(nb, dtype=w.dtype)
    return jnp.einsum("dnij,nm->dnimj", w, eye).reshape(nd, nb * bw, nb * bw)


def _layer_params(l, w_in, lru_conv_w, lru_conv_b, lru_wa, lru_ba, lru_wx, lru_bx, lru_lam, w_lru_out, w_na_out,
                  cm_dw_w, cm_dw_b, cm_ln_g, cm_ln_b, w_cm_out, mla_q_norm, mla_wq_b, mla_kv_norm, mla_wkv_b,
                  w_mla_out, w_out, moe_w_group, moe_w_expert, moe_w1, moe_w3, moe_w2):
    wi = w_in[l]
    n_small = 2 * LRU_WIDTH + 3 * NA_HEADS * NA_HD + 2 * CM_WIDTH + MLA_Q_LORA + MLA_KV_LORA
    kr_cols = wi[:, n_small:n_small + MLA_ROPE]
    zpad = jnp.zeros((D_MODEL, 128 - MLA_ROPE), wi.dtype)
    w_small = jnp.concatenate([wi[:, :n_small], kr_cols, zpad, _swap_rope_halves(kr_cols), zpad], axis=1)
    w_gates = wi[:, n_small + MLA_ROPE:]

    qd = MLA_NOPE + MLA_ROPE
    wq = mla_wq_b[l].reshape(MLA_Q_LORA, MLA_HEADS, qd)
    wqn = wq[:, :, :MLA_NOPE].reshape(MLA_Q_LORA, MLA_HEADS * MLA_NOPE)
    wqr = jnp.moveaxis(wq[:, :, MLA_NOPE:], 1, 0)
    rpad = ((0, 0), (0, 0), (0, 128 - MLA_ROPE))
    wkv = mla_wkv_b[l].reshape(MLA_KV_LORA, MLA_HEADS, MLA_NOPE + MLA_VD)
    wkt = jnp.transpose(wkv[:, :, :MLA_NOPE], (1, 2, 0))
    wv = jnp.moveaxis(wkv[:, :, MLA_NOPE:], 1, 0)

    router = jnp.concatenate([moe_w_group[l], moe_w_expert[l]], axis=1)
    router = jnp.pad(router, ((0, 0), (0, ROUTER_LANES - router.shape[1])))
    row = lambda a: a.reshape(1, -1)
    return dict(
        w_small=w_small.astype(BF16), w_gates=w_gates.astype(BF16),
        lru_conv_w=lru_conv_w[l], lru_conv_b=row(lru_conv_b[l]),
        lru_wa=_block_diag(lru_wa[l]).astype(BF16), lru_ba=lru_ba[l][:, None, :],
        lru_wx=_block_diag(lru_wx[l]).astype(BF16), lru_bx=lru_bx[l][:, None, :],
        lru_lam=lru_lam[l][:, None, :],
        cm_dw_w=cm_dw_w[l], cm_dw_b=row(cm_dw_b[l]), cm_ln_g=row(cm_ln_g[l]), cm_ln_b=row(cm_ln_b[l]),
        mla_q_norm=row(mla_q_norm[l]), mla_kv_norm=row(mla_kv_norm[l]),
        mla_wq_b=mla_wq_b[l].astype(BF16), mla_wkv_b=mla_wkv_b[l].astype(BF16),
        mla_wqn=wqn.astype(BF16), mla_wqr=jnp.pad(wqr, rpad).astype(BF16),
        mla_wqs=jnp.pad(_swap_rope_halves(wqr), rpad).astype(BF16),
        mla_wkt=wkt.astype(BF16), mla_wv=wv.astype(BF16),
        w_branch_out=jnp.stack([w_lru_out[l], w_na_out[l], w_cm_out[l], w_mla_out[l]], axis=0).astype(BF16),
        w_out=w_out[l].astype(BF16),
        w_router=router.astype(BF16),
        w13=jnp.concatenate([moe_w1[l], moe_w3[l]], axis=-1).astype(BF16),
        w2=moe_w2[l].astype(BF16),
    )


def kernel(x_prompt, x_sample, cache_na_k, cache_na_v, cache_mla_ckv, cache_mla_krope, state_lru, c, c_ctx,
           w_ada, b_ada, norm1_g, w_in, lru_conv_w, lru_conv_b, lru_wa, lru_ba, lru_wx, lru_bx, lru_lam,
           w_lru_out, na_rpb, w_na_out, cm_dw_w, cm_dw_b, cm_ln_g, cm_ln_b, w_cm_out, mla_q_norm, mla_wq_b,
           mla_kv_norm, mla_wkv_b, w_mla_out, w_out, norm2_g, moe_w_group, moe_w_expert, moe_w1, moe_w3,
           moe_w2, final_g):
    nb_c, seq_c, d = x_prompt.shape
    nb_l, seq_l, _ = x_sample.shape
    depth = w_in.shape[0]
    past = cache_na_k.shape[2]
    t_ctx = nb_c * seq_c
    t_lat = nb_l * seq_l
    assert d == D_MODEL and seq_l % (GRID_W * NA_QROWS) == 0 and seq_l % seq_c == 0
    assert t_ctx % TM_MOE == 0 and seq_l % TM_MOE == 0 and seq_c % SEQ_CHUNK == 0 and seq_l % TK_MLA == 0
    n_ctx_tiles = t_ctx // TM_TOKEN
    tiles_per_lat = seq_l // TM_TOKEN
    lat_blk0 = t_ctx // seq_l
    assert lat_blk0 * seq_l == t_ctx

    h = jnp.concatenate([x_prompt.reshape(t_ctx, d), x_sample.reshape(t_lat, d)], axis=0)
    n_cond = 1 + nb_l
    cvec = jnp.concatenate([c_ctx[None, :], c, jnp.zeros((-n_cond % 8, d), F32)], axis=0)
    mods = _modulation(cvec, w_ada, b_ada)
    cos, sin = _rope_tables(seq_l)
    zero_state = jnp.zeros((nb_c, 2, LRU_WIDTH), F32)

    st_k, st_v, st_ckv, st_kr, st_lru = [], [], [], [], []
    for l in range(depth):
        p = _layer_params(l, w_in, lru_conv_w, lru_conv_b, lru_wa, lru_ba, lru_wx, lru_bx, lru_lam, w_lru_out,
                          w_na_out, cm_dw_w, cm_dw_b, cm_ln_g, cm_ln_b, w_cm_out, mla_q_norm, mla_wq_b,
                          mla_kv_norm, mla_wkv_b, w_mla_out, w_out, moe_w_group, moe_w_expert, moe_w1, moe_w3,
                          moe_w2)
        mod_l = mods[l].reshape(mods.shape[1], 1, mods.shape[2])
        g1 = norm1_g[l].reshape(1, d)
        g2 = norm2_g[l].reshape(1, d)
        z = _in_proj(h, mod_l, g1, p["w_small"], n_ctx_tiles, tiles_per_lat)

        pa_c, lru_c = _lru_branch(z["lx"], z["ly"], zero_state, p, seq_c, 0, nb_c)
        pc_c = _conformer_branch(z["ga"], z["gg"], p, seq_c, 0, nb_c)
        pb_c, pd_c, ckvn_c = _ctx_attention(z, p, seq_c, nb_c)

        pa_l, _ = _lru_branch(z["lx"], z["ly"], state_lru[:, l], p, seq_l, lat_blk0, nb_l)
        pc_l = _conformer_branch(z["ga"], z["gg"], p, seq_l, lat_blk0, nb_l)
        bias = _na_bias_tables(na_rpb[l], seq_l // GRID_W)
        pb_l = _na_latent(z, cache_na_k[:, l].reshape(nb_l, past, NA_HEADS * NA_HD),
                          cache_na_v[:, l].reshape(nb_l, past, NA_HEADS * NA_HD), bias, seq_l, lat_blk0, nb_l)
        qm, kl = _mla_prep(z, cos, sin, p, seq_l, lat_blk0, nb_l)
        kc = jnp.concatenate([cache_mla_ckv[:, l], cache_mla_krope[:, l],
                              jnp.zeros((nb_l, past, MLA_KDIM - MLA_KV_LORA - MLA_ROPE), F32)], axis=-1).astype(BF16)
        pd_l = _mla_latent(qm, kl, kc, p["mla_wv"], seq_l, nb_l)

        cat = lambda a, b: jnp.concatenate([a, b], axis=0)
        h = _merge(h, mod_l, g1, cat(pa_c, pa_l), cat(pb_c, pb_l), cat(pc_c, pc_l), cat(pd_c, pd_l), p,
                   n_ctx_tiles, tiles_per_lat)
        h = _moe(h, mod_l, g2, p, n_ctx_tiles, tiles_per_lat)

        st_k.append(z["k"][:t_ctx].reshape(nb_c, seq_c, NA_HEADS, NA_HD))
        st_v.append(z["v"][:t_ctx].reshape(nb_c, seq_c, NA_HEADS, NA_HD))
        st_ckv.append(ckvn_c.reshape(nb_c, seq_c, MLA_KV_LORA))
        st_kr.append(z["krp"][:t_ctx, :MLA_ROPE].reshape(nb_c, seq_c, MLA_ROPE))
        st_lru.append(lru_c)

    fg = final_g.reshape(1, d)
    y_prompt = _final_norm(h, fg, 0, t_ctx).reshape(nb_c, seq_c, d)
    y_sample = _final_norm(h, fg, n_ctx_tiles, t_lat).reshape(nb_l, seq_l, d)
    return (y_prompt, y_sample, jnp.stack(st_k, axis=1), jnp.stack(st_v, axis=1), jnp.stack(st_ckv, axis=1),
            jnp.stack(st_kr, axis=1), jnp.stack(st_lru, axis=1))
```

```python
import functools
import math

import numpy as np
import jax
import jax.numpy as jnp
from jax import lax
from jax.experimental import pallas as pl
from jax.experimental.pallas import tpu as pltpu

F32 = jnp.float32
BF16 = jnp.bfloat16

D_MODEL = 1024
GRID_W = 64
EPS = 1e-6
NEG = -1e30
N_BRANCH = 4
LRU_WIDTH = 256
LRU_BLOCKS = 4
LRU_CONV = 4
LRU_C = 8.0
NA_HEADS = 4
NA_HD = 64
NA_WR = 8
NA_WC = 16
NA_QROWS = 4
NA_KROWS = NA_QROWS + NA_WR
CM_WIDTH = 256
CM_KERNEL = 31
MLA_HEADS = 4
MLA_Q_LORA = 256
MLA_KV_LORA = 128
MLA_NOPE = 64
MLA_ROPE = 32
MLA_VD = 64
MLA_KDIM = 256
ROPE_BASE = 10000.0
MOE_GROUPS = 4
MOE_PER_GROUP = 4
MOE_EXPERTS = 16
MOE_FF = 256
ROUTER_LANES = 128

TM_TOKEN = 512
TM_MOE = 1024
TQ_MLA = 256
TK_MLA = 512
SEQ_CHUNK = 256
VMEM_LIMIT = 56 * 1024 * 1024


def _cparams(sem, vmem=VMEM_LIMIT):
    return pltpu.CompilerParams(dimension_semantics=sem, vmem_limit_bytes=vmem)


def _sigmoid(x):
    return 1.0 / (1.0 + jnp.exp(-x))


def _silu(x):
    return x * _sigmoid(x)


def _gelu_tanh(x):
    c = math.sqrt(2.0 / math.pi)
    return x * (0.5 * (1.0 + jnp.tanh(c * (x + 0.044715 * (x * x * x)))))


def _softplus(x):
    return jnp.maximum(x, 0.0) + jnp.log1p(jnp.exp(-jnp.abs(x)))


def _rms(x, g):
    return x * lax.rsqrt(jnp.mean(x * x, axis=-1, keepdims=True) + EPS) * g


def _norm_mod(h, g, scale, shift):
    return _rms(h, g) * (1.0 + scale) + shift


def _dot(a, b):
    return jnp.dot(a, b, preferred_element_type=F32)


def _dot_nt(a, b):
    return lax.dot_general(a, b, (((1,), (1,)), ((), ())), preferred_element_type=F32)


def _mod_kernel(c_ref, w_ref, b_ref, o_ref):
    s = _silu(c_ref[...])
    o_ref[0] = jnp.dot(s, w_ref[0], preferred_element_type=F32, precision=lax.Precision.HIGHEST) + b_ref[0]


def _modulation(cvec, w_ada, b_ada):
    depth, d, n = w_ada.shape
    rows = cvec.shape[0]
    tn = 1024
    return pl.pallas_call(
        _mod_kernel,
        grid=(depth, n // tn),
        in_specs=[pl.BlockSpec((rows, d), lambda l, j: (0, 0)),
                  pl.BlockSpec((1, d, tn), lambda l, j: (l, 0, j)),
                  pl.BlockSpec((1, 1, tn), lambda l, j: (l, 0, j))],
        out_specs=pl.BlockSpec((1, rows, tn), lambda l, j: (l, 0, j)),
        out_shape=jax.ShapeDtypeStruct((depth, rows, n), F32),
        compiler_params=_cparams(("parallel", "parallel")),
        name="modulation",
    )(cvec, w_ada, b_ada.reshape(depth, 1, n))


def _mod_row_map(n_ctx_tiles, tiles_per_lat):
    def index_map(i, *_):
        row = jnp.maximum(i - n_ctx_tiles, 0) // tiles_per_lat + (i >= n_ctx_tiles).astype(jnp.int32)
        return (row, 0, 0)
    return index_map


_IN_SEGS = (("lx", 256), ("ly", 256), ("q", 256), ("k", 256), ("v", 256), ("ga", 256), ("gg", 256),
            ("cq", 256), ("ckv", 128), ("krp", 128), ("krs", 128))


def _in_kernel(h_ref, mod_ref, g_ref, w_ref, *out_refs):
    mod = mod_ref[0]
    u = _norm_mod(h_ref[...], g_ref[...], mod[:, D_MODEL:2 * D_MODEL], mod[:, 0:D_MODEL]).astype(BF16)
    off = 0
    for (_, width), o_ref in zip(_IN_SEGS, out_refs):
        o_ref[...] = _dot(u, w_ref[:, off:off + width]).astype(o_ref.dtype)
        off += width


def _in_proj(h, mods, g, w_small, n_ctx_tiles, tiles_per_lat):
    t, d = h.shape
    tm = TM_TOKEN
    n = w_small.shape[1]
    out_shape = [jax.ShapeDtypeStruct((t, width), F32) for _, width in _IN_SEGS]
    out_specs = [pl.BlockSpec((tm, width), lambda i: (i, 0)) for _, width in _IN_SEGS]
    outs = pl.pallas_call(
        _in_kernel,
        grid=(t // tm,),
        in_specs=[pl.BlockSpec((tm, d), lambda i: (i, 0)),
                  pl.BlockSpec((1, 1, mods.shape[-1]), _mod_row_map(n_ctx_tiles, tiles_per_lat)),
                  pl.BlockSpec((1, d), lambda i: (0, 0)),
                  pl.BlockSpec((d, n), lambda i: (0, 0))],
        out_specs=out_specs,
        out_shape=out_shape,
        compiler_params=_cparams(("parallel",)),
        name="in_proj",
    )(h, mods, g, w_small)
    return dict(zip([s for s, _ in _IN_SEGS], outs))


def _shifted_windows(xw, offsets, length):
    n = xw.shape[0]
    rolled = {0: xw}
    out = {}
    for o in offsets:
        r = o % 8
        if r not in rolled:
            rolled[r] = pltpu.roll(xw, n - r, 0)
        base = o - r
        out[o] = rolled[r][base:base + length]
    return out


def _lru_kernel(lx_ref, ly_ref, h0_ref, cw_ref, cb_ref, wa_ref, ba_ref, wx_ref, bx_ref, lam_ref,
                pa_ref, st_ref, pad_ref, af_ref, bf_ref, ab_ref, bb_ref):
    t = lx_ref.shape[0]
    ch = min(SEQ_CHUNK, t)
    halo = 8
    zeros = jnp.zeros((halo, LRU_WIDTH), F32)
    pad_ref[0:halo, :] = zeros
    pad_ref[halo + t:2 * halo + t, :] = zeros
    pad_ref[halo:halo + t, :] = lx_ref[...]
    pad_l = LRU_CONV // 2
    taps = [halo - pad_l + k for k in range(LRU_CONV)]
    a_refs = (af_ref, ab_ref)
    b_refs = (bf_ref, bb_ref)

    def gates(c, carry):
        r0 = pl.multiple_of(c * ch, ch)
        xw = pad_ref[pl.ds(r0, ch + 2 * halo), :]
        win = _shifted_windows(xw, taps, ch)
        xc = cb_ref[...] + sum(cw_ref[k:k + 1, :] * win[taps[k]] for k in range(LRU_CONV))
        xcb = xc.astype(BF16)
        for d in range(2):
            r = _sigmoid(_dot(xcb, wa_ref[d]) + ba_ref[d])
            i = _sigmoid(_dot(xcb, wx_ref[d]) + bx_ref[d])
            log_a = (-LRU_C) * r * _softplus(-lam_ref[d])
            a = jnp.exp(log_a)
            one_minus_a2 = -jnp.tanh(log_a) * (a * a + 1.0)
            a_refs[d][pl.ds(r0, ch), :] = a
            b_refs[d][pl.ds(r0, ch), :] = jnp.sqrt(one_minus_a2) * (i * xc)
        return carry

    lax.fori_loop(0, t // ch, gates, 0)

    def scan(s, carry):
        hf, hb = carry
        tf = s
        tb = t - 1 - s
        hf = af_ref[pl.ds(tf, 1), :] * hf + bf_ref[pl.ds(tf, 1), :]
        bf_ref[pl.ds(tf, 1), :] = hf
        hb = ab_ref[pl.ds(tb, 1), :] * hb + bb_ref[pl.ds(tb, 1), :]
        bb_ref[pl.ds(tb, 1), :] = hb
        return hf, hb

    hf, hb = lax.fori_loop(0, t, scan, (h0_ref[0, 0:1, :], h0_ref[0, 1:2, :]), unroll=8)
    st_ref[0, 0:1, :] = hf
    st_ref[0, 1:2, :] = hb

    def emit(c, carry):
        r0 = pl.multiple_of(c * ch, ch)
        y = (bf_ref[pl.ds(r0, ch), :] + bb_ref[pl.ds(r0, ch), :]) * _gelu_tanh(ly_ref[pl.ds(r0, ch), :])
        pa_ref[pl.ds(r0, ch), :] = y.astype(pa_ref.dtype)
        return carry

    lax.fori_loop(0, t // ch, emit, 0)


def _lru_branch(lx, ly, h0, p, seq, blk0, nb):
    w = LRU_WIDTH
    const2 = lambda b: (0, 0)
    const3 = lambda b: (0, 0, 0)
    return pl.pallas_call(
        _lru_kernel,
        grid=(nb,),
        in_specs=[pl.BlockSpec((seq, w), lambda b: (blk0 + b, 0)),
                  pl.BlockSpec((seq, w), lambda b: (blk0 + b, 0)),
                  pl.BlockSpec((1, 2, w), lambda b: (b, 0, 0)),
                  pl.BlockSpec((LRU_CONV, w), const2),
                  pl.BlockSpec((1, w), const2),
                  pl.BlockSpec((2, w, w), const3),
                  pl.BlockSpec((2, 1, w), const3),
                  pl.BlockSpec((2, w, w), const3),
                  pl.BlockSpec((2, 1, w), const3),
                  pl.BlockSpec((2, 1, w), const3)],
        out_specs=[pl.BlockSpec((seq, w), lambda b: (b, 0)),
                   pl.BlockSpec((1, 2, w), lambda b: (b, 0, 0))],
        out_shape=[jax.ShapeDtypeStruct((nb * seq, w), BF16),
                   jax.ShapeDtypeStruct((nb, 2, w), F32)],
        scratch_shapes=[pltpu.VMEM((seq + 16, w), F32)] + [pltpu.VMEM((seq, w), F32)] * 4,
        compiler_params=_cparams(("parallel",)),
        name="rglru",
    )(lx, ly, h0, p["lru_conv_w"], p["lru_conv_b"], p["lru_wa"], p["lru_ba"], p["lru_wx"], p["lru_bx"],
      p["lru_lam"])


def _cm_kernel(ga_ref, gg_ref, w_ref, b_ref, lg_ref, lb_ref, pc_ref, pad_ref):
    t = ga_ref.shape[0]
    ch = min(SEQ_CHUNK, t)
    halo = 16
    zeros = jnp.zeros((halo, CM_WIDTH), F32)
    pad_ref[0:halo, :] = zeros
    pad_ref[halo + t:2 * halo + t, :] = zeros

    def glu(c, carry):
        r0 = pl.multiple_of(c * ch, ch)
        pad_ref[pl.ds(halo + r0, ch), :] = ga_ref[pl.ds(r0, ch), :] * _sigmoid(gg_ref[pl.ds(r0, ch), :])
        return carry

    lax.fori_loop(0, t // ch, glu, 0)
    taps = [halo - CM_KERNEL // 2 + k for k in range(CM_KERNEL)]

    def conv(c, carry):
        r0 = pl.multiple_of(c * ch, ch)
        xw = pad_ref[pl.ds(r0, ch + 2 * halo), :]
        win = _shifted_windows(xw, taps, ch)
        z = b_ref[...] + sum(w_ref[k:k + 1, :] * win[taps[k]] for k in range(CM_KERNEL))
        mu = jnp.mean(z, axis=-1, keepdims=True)
        zc = z - mu
        var = jnp.mean(zc * zc, axis=-1, keepdims=True)
        y = zc * lax.rsqrt(var + EPS) * lg_ref[...] + lb_ref[...]
        pc_ref[pl.ds(r0, ch), :] = _silu(y).astype(pc_ref.dtype)
        return carry

    lax.fori_loop(0, t // ch, conv, 0)


def _conformer_branch(ga, gg, p, seq, blk0, nb):
    w = CM_WIDTH
    const2 = lambda b: (0, 0)
    return pl.pallas_call(
        _cm_kernel,
        grid=(nb,),
        in_specs=[pl.BlockSpec((seq, w), lambda b: (blk0 + b, 0)),
                  pl.BlockSpec((seq, w), lambda b: (blk0 + b, 0)),
                  pl.BlockSpec((CM_KERNEL, w), const2),
                  pl.BlockSpec((1, w), const2),
                  pl.BlockSpec((1, w), const2),
                  pl.BlockSpec((1, w), const2)],
        out_specs=pl.BlockSpec((seq, w), lambda b: (b, 0)),
        out_shape=jax.ShapeDtypeStruct((nb * seq, w), BF16),
        scratch_shapes=[pltpu.VMEM((seq + 32, w), F32)],
        compiler_params=_cparams(("parallel",)),
        name="conformer",
    )(ga, gg, p["cm_dw_w"], p["cm_dw_b"], p["cm_ln_g"], p["cm_ln_b"])


def _softmax_pv(scores, values):
    m = functools.reduce(jnp.maximum, [jnp.max(s, axis=-1, keepdims=True) for s in scores])
    ps = [jnp.exp(s - m) for s in scores]
    l = sum(jnp.sum(p, axis=-1, keepdims=True) for p in ps)
    o = sum(_dot(p.astype(BF16), v) for p, v in zip(ps, values))
    return o / l


def _ctx_attn_kernel(q_ref, k_ref, v_ref, cq_ref, ckv_ref, krp_ref, qn_ref, wq_ref, kvn_ref, wkv_ref,
                     pb_ref, pd_ref, ckvn_ref):
    q = q_ref[...].astype(BF16)
    k = k_ref[...].astype(BF16)
    v = v_ref[...].astype(BF16)
    na_scale = NA_HD ** -0.5
    for h in range(NA_HEADS):
        sl = slice(h * NA_HD, (h + 1) * NA_HD)
        s = _dot_nt(q[:, sl], k[:, sl]) * na_scale
        pb_ref[:, sl] = _softmax_pv([s], [v[:, sl]]).astype(pb_ref.dtype)

    ckv_n = _rms(ckv_ref[...], kvn_ref[...])
    ckvn_ref[...] = ckv_n
    qf = _dot(_rms(cq_ref[...], qn_ref[...]).astype(BF16), wq_ref[...]).astype(BF16)
    kv = _dot(ckv_n.astype(BF16), wkv_ref[...]).astype(BF16)
    kr = krp_ref[:, 0:MLA_ROPE].astype(BF16)
    mla_scale = (MLA_NOPE + MLA_ROPE) ** -0.5
    qd = MLA_NOPE + MLA_ROPE
    kd = MLA_NOPE + MLA_VD
    for h in range(MLA_HEADS):
        qn = qf[:, h * qd:h * qd + MLA_NOPE]
        qr = qf[:, h * qd + MLA_NOPE:(h + 1) * qd]
        kn = kv[:, h * kd:h * kd + MLA_NOPE]
        vm = kv[:, h * kd + MLA_NOPE:(h + 1) * kd]
        s = (_dot_nt(qn, kn) + _dot_nt(qr, kr)) * mla_scale
        pd_ref[:, h * MLA_VD:(h + 1) * MLA_VD] = _softmax_pv([s], [vm]).astype(pd_ref.dtype)


def _ctx_attention(z, p, seq, nb):
    const2 = lambda b: (0, 0)
    row = lambda b: (b, 0)
    return pl.pallas_call(
        _ctx_attn_kernel,
        grid=(nb,),
        in_specs=[pl.BlockSpec((seq, 256), row), pl.BlockSpec((seq, 256), row), pl.BlockSpec((seq, 256), row),
                  pl.BlockSpec((seq, MLA_Q_LORA), row), pl.BlockSpec((seq, MLA_KV_LORA), row),
                  pl.BlockSpec((seq, 128), row),
                  pl.BlockSpec((1, MLA_Q_LORA), const2),
                  pl.BlockSpec(p["mla_wq_b"].shape, const2),
                  pl.BlockSpec((1, MLA_KV_LORA), const2),
                  pl.BlockSpec(p["mla_wkv_b"].shape, const2)],
        out_specs=[pl.BlockSpec((seq, 256), row), pl.BlockSpec((seq, 256), row),
                   pl.BlockSpec((seq, MLA_KV_LORA), row)],
        out_shape=[jax.ShapeDtypeStruct((nb * seq, 256), BF16),
                   jax.ShapeDtypeStruct((nb * seq, 256), BF16),
                   jax.ShapeDtypeStruct((nb * seq, MLA_KV_LORA), F32)],
        compiler_params=_cparams(("parallel",)),
        name="ctx_attention",
    )(z["q"], z["k"], z["v"], z["cq"], z["ckv"], z["krp"], p["mla_q_norm"], p["mla_wq_b"],
      p["mla_kv_norm"], p["mla_wkv_b"])


def _na_bias_tables(rpb, rows):
    heads, n_rel_r, n_rel_c = rpb.shape
    w = GRID_W
    u = jnp.concatenate([rpb[:, :, NA_WC - 1:], jnp.zeros((heads, n_rel_r, 2 * w - n_rel_c), rpb.dtype),
                         rpb[:, :, :NA_WC - 1]], axis=-1)
    toep = jnp.tile(u, (1, 1, w))[:, :, :w * (2 * w - 1)].reshape(heads, n_rel_r, w, 2 * w - 1)[..., :w]
    qc = np.arange(w)
    cs = np.clip(qc - NA_WC // 2, 0, w - NA_WC)
    col_ok = (qc[None, :] >= cs[:, None]) & (qc[None, :] < cs[:, None] + NA_WC)
    toep = jnp.where(col_ok, toep, NEG)
    masked = jnp.full((heads, w, w), NEG, rpb.dtype)
    n_steps = rows // NA_QROWS
    tables = []
    for step in (0, 1, n_steps - 1):
        start = int(np.clip(NA_QROWS * step - NA_WR // 2, 0, rows - NA_KROWS))
        block_rows = []
        for a in range(NA_QROWS):
            qr = NA_QROWS * step + a
            rs = int(np.clip(qr - NA_WR // 2, 0, rows - NA_WR))
            blocks = []
            for b in range(NA_KROWS):
                kr = start + b
                blocks.append(toep[:, kr - qr + NA_WR - 1] if rs <= kr < rs + NA_WR else masked)
            block_rows.append(jnp.concatenate(blocks, axis=-1))
        tables.append(jnp.concatenate(block_rows, axis=1))
    return jnp.stack(tables, axis=0)


def _na_lat_kernel(q_ref, k_ref, v_ref, ck_ref, cv_ref, bias_ref, pb_ref):
    i = pl.program_id(1)
    rows = k_ref.shape[0] // GRID_W
    start = jnp.clip(NA_QROWS * i - NA_WR // 2, 0, rows - NA_KROWS) * GRID_W
    start = pl.multiple_of(start, GRID_W)
    nk = NA_KROWS * GRID_W
    q = q_ref[...].astype(BF16)
    kw = k_ref[pl.ds(start, nk), :].astype(BF16)
    vw = v_ref[pl.ds(start, nk), :].astype(BF16)
    ck = ck_ref[0].astype(BF16)
    cv = cv_ref[0].astype(BF16)
    scale = NA_HD ** -0.5
    for h in range(NA_HEADS):
        sl = slice(h * NA_HD, (h + 1) * NA_HD)
        s_loc = _dot_nt(q[:, sl], kw[:, sl]) * scale + bias_ref[0, h]
        s_ctx = _dot_nt(q[:, sl], ck[:, sl]) * scale
        pb_ref[:, sl] = _softmax_pv([s_loc, s_ctx], [vw[:, sl], cv[:, sl]]).astype(pb_ref.dtype)


def _na_latent(z, cache_k, cache_v, bias, seq, blk0, nb):
    tq = NA_QROWS * GRID_W
    steps = seq // tq
    nk = NA_KROWS * GRID_W
    past = cache_k.shape[1]

    def bias_map(b, i):
        return (jnp.where(i == 0, 0, jnp.where(i == steps - 1, 2, 1)), 0, 0, 0)

    return pl.pallas_call(
        _na_lat_kernel,
        grid=(nb, steps),
        in_specs=[pl.BlockSpec((tq, 256), lambda b, i: (blk0 * steps + b * steps + i, 0)),
                  pl.BlockSpec((seq, 256), lambda b, i: (blk0 + b, 0)),
                  pl.BlockSpec((seq, 256), lambda b, i: (blk0 + b, 0)),
                  pl.BlockSpec((1, past, 256), lambda b, i: (b, 0, 0)),
                  pl.BlockSpec((1, past, 256), lambda b, i: (b, 0, 0)),
                  pl.BlockSpec((1, NA_HEADS, tq, nk), bias_map)],
        out_specs=pl.BlockSpec((tq, 256), lambda b, i: (b * steps + i, 0)),
        out_shape=jax.ShapeDtypeStruct((nb * seq, 256), BF16),
        compiler_params=_cparams(("parallel", "arbitrary")),
        name="na_latent",
    )(z["q"], z["k"], z["v"], cache_k, cache_v, bias)


def _rope_tables(seq):
    t = np.arange(seq)
    pos = np.stack([t // GRID_W, t % GRID_W], axis=-1).astype(np.float32)
    n_freq = MLA_ROPE // 4
    inv = jnp.asarray(ROPE_BASE, F32) ** (-jnp.arange(n_freq, dtype=F32) / n_freq)
    ang = jnp.asarray(pos)[:, :, None] * inv
    cos, sin = jnp.cos(ang), jnp.sin(ang)
    c = jnp.stack([cos, cos], axis=2).reshape(seq, MLA_ROPE)
    s = jnp.stack([-sin, sin], axis=2).reshape(seq, MLA_ROPE)
    pad = ((0, 0), (0, 128 - MLA_ROPE))
    return jnp.pad(c, pad), jnp.pad(s, pad)


def _swap_rope_halves(w):
    w4 = w.reshape(w.shape[:-1] + (2, 2, MLA_ROPE // 4))
    return w4[..., ::-1, :].reshape(w.shape)


def _mla_prep_kernel(cq_ref, ckv_ref, krp_ref, krs_ref, cos_ref, sin_ref, qn_ref, wqn_ref, wqr_ref, wqs_ref,
                     kvn_ref, wkt_ref, qm_ref, kl_ref, klt_ref):
    tq = cq_ref.shape[0]
    cos = cos_ref[...]
    sin = sin_ref[...]
    ckv_n = _rms(ckv_ref[...], kvn_ref[...])
    kl_ref[0, :, 0:128] = ckv_n.astype(kl_ref.dtype)
    kl_ref[0, :, 128:256] = (krp_ref[...] * cos + krs_ref[...] * sin).astype(kl_ref.dtype)
    klt_ref[0] = ckv_n.T.astype(klt_ref.dtype)
    cqn = _rms(cq_ref[...], qn_ref[...]).astype(BF16)
    qn = _dot(cqn, wqn_ref[...]).astype(BF16)
    scale = (MLA_NOPE + MLA_ROPE) ** -0.5 * math.log2(math.e)
    for h in range(MLA_HEADS):
        qa = _dot(qn[:, h * MLA_NOPE:(h + 1) * MLA_NOPE], wkt_ref[h])
        qr = _dot(cqn, wqr_ref[h]) * cos + _dot(cqn, wqs_ref[h]) * sin
        qm_ref[0, 0, h * tq:(h + 1) * tq, 0:128] = (qa * scale).astype(qm_ref.dtype)
        qm_ref[0, 0, h * tq:(h + 1) * tq, 128:256] = (qr * scale).astype(qm_ref.dtype)


def _mla_prep(z, cos, sin, p, seq, blk0, nb):
    tq = TQ_MLA
    steps = seq // tq
    tok = lambda b, i: (blk0 * steps + b * steps + i, 0)
    pos = lambda b, i: (i, 0)
    const2 = lambda b, i: (0, 0)
    const3 = lambda b, i: (0, 0, 0)
    return pl.pallas_call(
        _mla_prep_kernel,
        grid=(nb, steps),
        in_specs=[pl.BlockSpec((tq, MLA_Q_LORA), tok), pl.BlockSpec((tq, MLA_KV_LORA), tok),
                  pl.BlockSpec((tq, 128), tok), pl.BlockSpec((tq, 128), tok),
                  pl.BlockSpec((tq, 128), pos), pl.BlockSpec((tq, 128), pos),
                  pl.BlockSpec((1, MLA_Q_LORA), const2),
                  pl.BlockSpec((MLA_Q_LORA, MLA_HEADS * MLA_NOPE), const2),
                  pl.BlockSpec((MLA_HEADS, MLA_Q_LORA, 128), const3),
                  pl.BlockSpec((MLA_HEADS, MLA_Q_LORA, 128), const3),
                  pl.BlockSpec((1, MLA_KV_LORA), const2),
                  pl.BlockSpec((MLA_HEADS, MLA_NOPE, MLA_KV_LORA), const3)],
        out_specs=[pl.BlockSpec((1, 1, MLA_HEADS * tq, MLA_KDIM), lambda b, i: (b, i, 0, 0)),
                   pl.BlockSpec((1, tq, MLA_KDIM), lambda b, i: (b, i, 0)),
                   pl.BlockSpec((1, MLA_KV_LORA, tq), lambda b, i: (b, 0, i))],
        out_shape=[jax.ShapeDtypeStruct((nb, steps, MLA_HEADS * tq, MLA_KDIM), BF16),
                   jax.ShapeDtypeStruct((nb, seq, MLA_KDIM), BF16),
                   jax.ShapeDtypeStruct((nb, MLA_KV_LORA, seq), BF16)],
        compiler_params=_cparams(("parallel", "parallel")),
        name="mla_prep",
    )(z["cq"], z["ckv"], z["krp"], z["krs"], cos, sin, p["mla_q_norm"], p["mla_wqn"], p["mla_wqr"],
      p["mla_wqs"], p["mla_kv_norm"], p["mla_wkt"])


def _mla_lat_kernel(qm_ref, kl_ref, klt_ref, kc_ref, kct_ref, wvt_ref, pd_ref, m_ref, l_ref, acc_ref):
    q = qm_ref[0, 0]
    cols = q.shape[0]
    tq = cols // MLA_HEADS
    m_ref[...] = jnp.full((1, cols), NEG, F32)
    l_ref[...] = jnp.zeros((1, cols), F32)
    acc_ref[...] = jnp.zeros((MLA_KV_LORA, cols), F32)

    def update(kblk, vt):
        s = _dot_nt(kblk, q)
        m_old = m_ref[...]
        m_new = jnp.maximum(m_old, jnp.max(s, axis=0, keepdims=True))
        alpha = jnp.exp2(m_old - m_new)
        p = jnp.exp2(s - m_new)
        l_ref[...] = alpha * l_ref[...] + jnp.sum(p, axis=0, keepdims=True)
        acc_ref[...] = alpha * acc_ref[...] + _dot(vt, p.astype(BF16))
        m_ref[...] = m_new

    def body(c, carry):
        k0 = pl.multiple_of(c * TK_MLA, TK_MLA)
        update(kl_ref[0, pl.ds(k0, TK_MLA), :], klt_ref[0, :, pl.ds(k0, TK_MLA)])
        return carry

    lax.fori_loop(0, kl_ref.shape[1] // TK_MLA, body, 0)
    update(kc_ref[0], kct_ref[0])
    o = (acc_ref[...] / l_ref[...]).astype(BF16)
    out_t = jnp.concatenate([_dot(wvt_ref[h], o[:, h * tq:(h + 1) * tq]) for h in range(MLA_HEADS)], axis=0)
    pd_ref[...] = out_t.T.astype(pd_ref.dtype)


def _mla_latent(qm, kl, klt, kc, kct, wvt, seq, nb):
    tq = TQ_MLA
    steps = seq // tq
    cols = MLA_HEADS * tq
    past = kc.shape[1]
    return pl.pallas_call(
        _mla_lat_kernel,
        grid=(nb, steps),
        in_specs=[pl.BlockSpec((1, 1, cols, MLA_KDIM), lambda b, i: (b, i, 0, 0)),
                  pl.BlockSpec((1, seq, MLA_KDIM), lambda b, i: (b, 0, 0)),
                  pl.BlockSpec((1, MLA_KV_LORA, seq), lambda b, i: (b, 0, 0)),
                  pl.BlockSpec((1, past, MLA_KDIM), lambda b, i: (b, 0, 0)),
                  pl.BlockSpec((1, MLA_KV_LORA, past), lambda b, i: (b, 0, 0)),
                  pl.BlockSpec((MLA_HEADS, MLA_VD, MLA_KV_LORA), lambda b, i: (0, 0, 0))],
        out_specs=pl.BlockSpec((tq, MLA_HEADS * MLA_VD), lambda b, i: (b * steps + i, 0)),
        out_shape=jax.ShapeDtypeStruct((nb * seq, MLA_HEADS * MLA_VD), BF16),
        scratch_shapes=[pltpu.VMEM((1, cols), F32), pltpu.VMEM((1, cols), F32),
                        pltpu.VMEM((MLA_KV_LORA, cols), F32)],
        compiler_params=_cparams(("parallel", "arbitrary")),
        name="mla_latent",
    )(qm, kl, klt, kc, kct, wvt)


def _merge_kernel(h_ref, mod_ref, g_ref, pa_ref, pb_ref, pc_ref, pd_ref, wg_ref, wb_ref, wo_ref, o_ref):
    mod = mod_ref[0]
    h = h_ref[...]
    u = _norm_mod(h, g_ref[...], mod[:, D_MODEL:2 * D_MODEL], mod[:, 0:D_MODEL]).astype(BF16)
    y = None
    for j, p_ref in enumerate((pa_ref, pb_ref, pc_ref, pd_ref)):
        gate = _sigmoid(_dot(u, wg_ref[:, j * D_MODEL:(j + 1) * D_MODEL]))
        term = gate * _dot(p_ref[...], wb_ref[j])
        y = term if y is None else y + term
    out = _dot(y.astype(BF16), wo_ref[...])
    o_ref[...] = h + mod[:, 2 * D_MODEL:3 * D_MODEL] * out


def _merge(h, mods, g, pa, pb, pc, pd, p, n_ctx_tiles, tiles_per_lat):
    t, d = h.shape
    tm = TM_TOKEN
    row = lambda i: (i, 0)
    const2 = lambda i: (0, 0)
    return pl.pallas_call(
        _merge_kernel,
        grid=(t // tm,),
        in_specs=[pl.BlockSpec((tm, d), row),
                  pl.BlockSpec((1, 1, mods.shape[-1]), _mod_row_map(n_ctx_tiles, tiles_per_lat)),
                  pl.BlockSpec((1, d), const2),
                  pl.BlockSpec((tm, 256), row), pl.BlockSpec((tm, 256), row),
                  pl.BlockSpec((tm, 256), row), pl.BlockSpec((tm, 256), row),
                  pl.BlockSpec((d, N_BRANCH * d), const2),
                  pl.BlockSpec((N_BRANCH, 256, d), lambda i: (0, 0, 0)),
                  pl.BlockSpec((d, d), const2)],
        out_specs=pl.BlockSpec((tm, d), row),
        out_shape=jax.ShapeDtypeStruct((t, d), F32),
        compiler_params=_cparams(("parallel",)),
        name="merge",
    )(h, mods, g, pa, pb, pc, pd, p["w_gates"], p["w_branch_out"], p["w_out"])


def _router_gates(logits):
    lane_i = lax.broadcasted_iota(jnp.int32, logits.shape, 1)
    lane = lane_i.astype(F32)
    ninf = jnp.float32(-jnp.inf)

    def first_argmax(x):
        m = jnp.max(x, axis=-1, keepdims=True)
        idx = jnp.min(jnp.where(x == m, lane, jnp.float32(1e9)), axis=-1, keepdims=True)
        return m, idx

    gl = jnp.where(lane_i < MOE_GROUPS, logits, ninf)
    gmax, gsel = first_argmax(gl)
    gp = 1.0 / jnp.sum(jnp.exp(gl - gmax), axis=-1, keepdims=True)
    e_idx = lane_i - MOE_GROUPS
    e_group = lax.shift_right_arithmetic(e_idx, jnp.full_like(e_idx, 2)).astype(F32)
    in_group = (e_idx >= 0) & (e_idx < MOE_EXPERTS) & (e_group == gsel)
    el = jnp.where(in_group, logits, ninf)
    m1, i1 = first_argmax(el)
    m2, i2 = first_argmax(jnp.where(lane == i1, ninf, el))
    e2 = jnp.exp(m2 - m1)
    w1 = gp / (1.0 + e2)
    w2 = gp * e2 / (1.0 + e2)
    return jnp.where(lane == i1, w1, 0.0) + jnp.where(lane == i2, w2, 0.0)


def _moe_kernel(h_ref, mod_ref, g_ref, wr_ref, w13_ref, w2_ref, o_ref, u_ref, gate_ref, acc_ref):
    e = pl.program_id(1)
    mod = mod_ref[0]

    @pl.when(e == 0)
    def _():
        u = _norm_mod(h_ref[...], g_ref[...], mod[:, 4 * D_MODEL:5 * D_MODEL], mod[:, 3 * D_MODEL:4 * D_MODEL])
        ub = u.astype(BF16)
        u_ref[...] = ub
        gate_ref[...] = _router_gates(_dot(ub, wr_ref[...]))
        acc_ref[...] = jnp.zeros_like(acc_ref)

    ub = u_ref[...]
    lane = lax.broadcasted_iota(jnp.int32, gate_ref.shape, 1)
    gate = jnp.sum(jnp.where(lane == e + MOE_GROUPS, gate_ref[...], 0.0), axis=-1, keepdims=True)
    h13 = _dot(ub, w13_ref[0])
    hid = _silu(h13[:, 0:MOE_FF]) * h13[:, MOE_FF:2 * MOE_FF] * gate
    acc_ref[...] += _dot(hid.astype(BF16), w2_ref[0])

    @pl.when(e == MOE_EXPERTS - 1)
    def _():
        o_ref[...] = h_ref[...] + mod[:, 5 * D_MODEL:6 * D_MODEL] * acc_ref[...]


def _moe(h, mods, g, p, n_ctx_tiles, tiles_per_lat):
    t, d = h.shape
    tm = TM_MOE
    nct = n_ctx_tiles * TM_TOKEN // tm
    tpl = tiles_per_lat * TM_TOKEN // tm
    return pl.pallas_call(
        _moe_kernel,
        grid=(t // tm, MOE_EXPERTS),
        in_specs=[pl.BlockSpec((tm, d), lambda i, e: (i, 0)),
                  pl.BlockSpec((1, 1, mods.shape[-1]), _mod_row_map(nct, tpl)),
                  pl.BlockSpec((1, d), lambda i, e: (0, 0)),
                  pl.BlockSpec((d, ROUTER_LANES), lambda i, e: (0, 0)),
                  pl.BlockSpec((1, d, 2 * MOE_FF), lambda i, e: (e, 0, 0)),
                  pl.BlockSpec((1, MOE_FF, d), lambda i, e: (e, 0, 0))],
        out_specs=pl.BlockSpec((tm, d), lambda i, e: (i, 0)),
        out_shape=jax.ShapeDtypeStruct((t, d), F32),
        scratch_shapes=[pltpu.VMEM((tm, d), BF16), pltpu.VMEM((tm, ROUTER_LANES), F32),
                        pltpu.VMEM((tm, d), F32)],
        compiler_params=_cparams(("parallel", "arbitrary")),
        name="moe",
    )(h, mods, g, p["w_router"], p["w13"], p["w2"])


def _final_kernel(h_ref, g_ref, o_ref):
    o_ref[...] = _rms(h_ref[...], g_ref[...])


def _final_norm(h, g, blk0, rows):
    d = h.shape[1]
    tm = TM_TOKEN
    return pl.pallas_call(
        _final_kernel,
        grid=(rows // tm,),
        in_specs=[pl.BlockSpec((tm, d), lambda i: (blk0 + i, 0)), pl.BlockSpec((1, d), lambda i: (0, 0))],
        out_specs=pl.BlockSpec((tm, d), lambda i: (i, 0)),
        out_shape=jax.ShapeDtypeStruct((rows, d), F32),
        compiler_params=_cparams(("parallel",)),
        name="final_norm",
    )(h, g)


def _block_diag(w):
    nd, nb, bw, _ = w.shape
    eye = jnp.eye(nb, dtype=w.dtype)
    return jnp.einsum("dnij,nm->dnimj", w, eye).reshape(nd, nb * bw, nb * bw)


def _layer_params(l, w_in, lru_conv_w, lru_conv_b, lru_wa, lru_ba, lru_wx, lru_bx, lru_lam, w_lru_out, w_na_out,
                  cm_dw_w, cm_dw_b, cm_ln_g, cm_ln_b, w_cm_out, mla_q_norm, mla_wq_b, mla_kv_norm, mla_wkv_b,
                  w_mla_out, w_out, moe_w_group, moe_w_expert, moe_w1, moe_w3, moe_w2):
    wi = w_in[l]
    n_small = 2 * LRU_WIDTH + 3 * NA_HEADS * NA_HD + 2 * CM_WIDTH + MLA_Q_LORA + MLA_KV_LORA
    kr_cols = wi[:, n_small:n_small + MLA_ROPE]
    zpad = jnp.zeros((D_MODEL, 128 - MLA_ROPE), wi.dtype)
    w_small = jnp.concatenate([wi[:, :n_small], kr_cols, zpad, _swap_rope_halves(kr_cols), zpad], axis=1)
    w_gates = wi[:, n_small + MLA_ROPE:]

    qd = MLA_NOPE + MLA_ROPE
    wq = mla_wq_b[l].reshape(MLA_Q_LORA, MLA_HEADS, qd)
    wqn = wq[:, :, :MLA_NOPE].reshape(MLA_Q_LORA, MLA_HEADS * MLA_NOPE)
    wqr = jnp.moveaxis(wq[:, :, MLA_NOPE:], 1, 0)
    rpad = ((0, 0), (0, 0), (0, 128 - MLA_ROPE))
    wkv = mla_wkv_b[l].reshape(MLA_KV_LORA, MLA_HEADS, MLA_NOPE + MLA_VD)
    wkt = jnp.transpose(wkv[:, :, :MLA_NOPE], (1, 2, 0))
    wvt = jnp.transpose(wkv[:, :, MLA_NOPE:], (1, 2, 0))

    router = jnp.concatenate([moe_w_group[l], moe_w_expert[l]], axis=1)
    router = jnp.pad(router, ((0, 0), (0, ROUTER_LANES - router.shape[1])))
    row = lambda a: a.reshape(1, -1)
    return dict(
        w_small=w_small.astype(BF16), w_gates=w_gates.astype(BF16),
        lru_conv_w=lru_conv_w[l], lru_conv_b=row(lru_conv_b[l]),
        lru_wa=_block_diag(lru_wa[l]).astype(BF16), lru_ba=lru_ba[l][:, None, :],
        lru_wx=_block_diag(lru_wx[l]).astype(BF16), lru_bx=lru_bx[l][:, None, :],
        lru_lam=lru_lam[l][:, None, :],
        cm_dw_w=cm_dw_w[l], cm_dw_b=row(cm_dw_b[l]), cm_ln_g=row(cm_ln_g[l]), cm_ln_b=row(cm_ln_b[l]),
        mla_q_norm=row(mla_q_norm[l]), mla_kv_norm=row(mla_kv_norm[l]),
        mla_wq_b=mla_wq_b[l].astype(BF16), mla_wkv_b=mla_wkv_b[l].astype(BF16),
        mla_wqn=wqn.astype(BF16), mla_wqr=jnp.pad(wqr, rpad).astype(BF16),
        mla_wqs=jnp.pad(_swap_rope_halves(wqr), rpad).astype(BF16),
        mla_wkt=wkt.astype(BF16), mla_wvt=wvt.astype(BF16),
        w_branch_out=jnp.stack([w_lru_out[l], w_na_out[l], w_cm_out[l], w_mla_out[l]], axis=0).astype(BF16),
        w_out=w_out[l].astype(BF16),
        w_router=router.astype(BF16),
        w13=jnp.concatenate([moe_w1[l], moe_w3[l]], axis=-1).astype(BF16),
        w2=moe_w2[l].astype(BF16),
    )


def kernel(x_prompt, x_sample, cache_na_k, cache_na_v, cache_mla_ckv, cache_mla_krope, state_lru, c, c_ctx,
           w_ada, b_ada, norm1_g, w_in, lru_conv_w, lru_conv_b, lru_wa, lru_ba, lru_wx, lru_bx, lru_lam,
           w_lru_out, na_rpb, w_na_out, cm_dw_w, cm_dw_b, cm_ln_g, cm_ln_b, w_cm_out, mla_q_norm, mla_wq_b,
           mla_kv_norm, mla_wkv_b, w_mla_out, w_out, norm2_g, moe_w_group, moe_w_expert, moe_w1, moe_w3,
           moe_w2, final_g):
    nb_c, seq_c, d = x_prompt.shape
    nb_l, seq_l, _ = x_sample.shape
    depth = w_in.shape[0]
    past = cache_na_k.shape[2]
    t_ctx = nb_c * seq_c
    t_lat = nb_l * seq_l
    assert d == D_MODEL and seq_l % (GRID_W * NA_QROWS) == 0 and seq_l % seq_c == 0
    assert t_ctx % TM_MOE == 0 and seq_l % TM_MOE == 0 and seq_c % SEQ_CHUNK == 0 and seq_l % TK_MLA == 0
    n_ctx_tiles = t_ctx // TM_TOKEN
    tiles_per_lat = seq_l // TM_TOKEN
    lat_blk0 = t_ctx // seq_l
    assert lat_blk0 * seq_l == t_ctx

    h = jnp.concatenate([x_prompt.reshape(t_ctx, d), x_sample.reshape(t_lat, d)], axis=0)
    n_cond = 1 + nb_l
    cvec = jnp.concatenate([c_ctx[None, :], c, jnp.zeros((-n_cond % 8, d), F32)], axis=0)
    mods = _modulation(cvec, w_ada, b_ada)
    cos, sin = _rope_tables(seq_l)
    zero_state = jnp.zeros((nb_c, 2, LRU_WIDTH), F32)

    st_k, st_v, st_ckv, st_kr, st_lru = [], [], [], [], []
    for l in range(depth):
        p = _layer_params(l, w_in, lru_conv_w, lru_conv_b, lru_wa, lru_ba, lru_wx, lru_bx, lru_lam, w_lru_out,
                          w_na_out, cm_dw_w, cm_dw_b, cm_ln_g, cm_ln_b, w_cm_out, mla_q_norm, mla_wq_b,
                          mla_kv_norm, mla_wkv_b, w_mla_out, w_out, moe_w_group, moe_w_expert, moe_w1, moe_w3,
                          moe_w2)
        mod_l = mods[l].reshape(mods.shape[1], 1, mods.shape[2])
        g1 = norm1_g[l].reshape(1, d)
        g2 = norm2_g[l].reshape(1, d)
        z = _in_proj(h, mod_l, g1, p["w_small"], n_ctx_tiles, tiles_per_lat)

        pa_c, lru_c = _lru_branch(z["lx"], z["ly"], zero_state, p, seq_c, 0, nb_c)
        pc_c = _conformer_branch(z["ga"], z["gg"], p, seq_c, 0, nb_c)
        pb_c, pd_c, ckvn_c = _ctx_attention(z, p, seq_c, nb_c)

        pa_l, _ = _lru_branch(z["lx"], z["ly"], state_lru[:, l], p, seq_l, lat_blk0, nb_l)
        pc_l = _conformer_branch(z["ga"], z["gg"], p, seq_l, lat_blk0, nb_l)
        bias = _na_bias_tables(na_rpb[l], seq_l // GRID_W)
        pb_l = _na_latent(z, cache_na_k[:, l].reshape(nb_l, past, NA_HEADS * NA_HD),
                          cache_na_v[:, l].reshape(nb_l, past, NA_HEADS * NA_HD), bias, seq_l, lat_blk0, nb_l)
        qm, kl, klt = _mla_prep(z, cos, sin, p, seq_l, lat_blk0, nb_l)
        kc = jnp.concatenate([cache_mla_ckv[:, l], cache_mla_krope[:, l],
                              jnp.zeros((nb_l, past, MLA_KDIM - MLA_KV_LORA - MLA_ROPE), F32)], axis=-1).astype(BF16)
        kct = jnp.swapaxes(cache_mla_ckv[:, l], 1, 2).astype(BF16)
        pd_l = _mla_latent(qm, kl, klt, kc, kct, p["mla_wvt"], seq_l, nb_l)

        cat = lambda a, b: jnp.concatenate([a, b], axis=0)
        h = _merge(h, mod_l, g1, cat(pa_c, pa_l), cat(pb_c, pb_l), cat(pc_c, pc_l), cat(pd_c, pd_l), p,
                   n_ctx_tiles, tiles_per_lat)
        h = _moe(h, mod_l, g2, p, n_ctx_tiles, tiles_per_lat)

        st_k.append(z["k"][:t_ctx].reshape(nb_c, seq_c, NA_HEADS, NA_HD))
        st_v.append(z["v"][:t_ctx].reshape(nb_c, seq_c, NA_HEADS, NA_HD))
        st_ckv.append(ckvn_c.reshape(nb_c, seq_c, MLA_KV_LORA))
        st_kr.append(z["krp"][:t_ctx, :MLA_ROPE].reshape(nb_c, seq_c, MLA_ROPE))
        st_lru.append(lru_c)

    fg = final_g.reshape(1, d)
    y_prompt = _final_norm(h, fg, 0, t_ctx).reshape(nb_c, seq_c, d)
    y_sample = _final_norm(h, fg, n_ctx_tiles, t_lat).reshape(nb_l, seq_l, d)
    return (y_prompt, y_sample, jnp.stack(st_k, axis=1), jnp.stack(st_v, axis=1), jnp.stack(st_ckv, axis=1),
            jnp.stack(st_kr, axis=1), jnp.stack(st_lru, axis=1))
```

```python
import functools
import math

import numpy as np
import jax
import jax.numpy as jnp
from jax import lax
from jax.experimental import pallas as pl
from jax.experimental.pallas import tpu as pltpu

F32 = jnp.float32
BF16 = jnp.bfloat16

D_MODEL = 1024
GRID_W = 64
EPS = 1e-6
NEG = -1e30
N_BRANCH = 4
LRU_WIDTH = 256
LRU_BLOCKS = 4
LRU_CONV = 4
LRU_C = 8.0
NA_HEADS = 4
NA_HD = 64
NA_WR = 8
NA_WC = 16
NA_QROWS = 4
NA_KROWS = NA_QROWS + NA_WR
CM_WIDTH = 256
CM_KERNEL = 31
MLA_HEADS = 4
MLA_Q_LORA = 256
MLA_KV_LORA = 128
MLA_NOPE = 64
MLA_ROPE = 32
MLA_VD = 64
MLA_KDIM = 256
ROPE_BASE = 10000.0
MOE_GROUPS = 4
MOE_PER_GROUP = 4
MOE_EXPERTS = 16
MOE_FF = 256
ROUTER_LANES = 128

TM_TOKEN = 512
TM_MOE = 1024
TQ_MLA = 256
TK_MLA = 512
SEQ_CHUNK = 256
VMEM_LIMIT = 56 * 1024 * 1024


def _cparams(sem, vmem=VMEM_LIMIT):
    return pltpu.CompilerParams(dimension_semantics=sem, vmem_limit_bytes=vmem)


def _sigmoid(x):
    return 1.0 / (1.0 + jnp.exp(-x))


def _silu(x):
    return x * _sigmoid(x)


def _gelu_tanh(x):
    c = math.sqrt(2.0 / math.pi)
    return x * (0.5 * (1.0 + jnp.tanh(c * (x + 0.044715 * (x * x * x)))))


def _softplus(x):
    return jnp.maximum(x, 0.0) + jnp.log1p(jnp.exp(-jnp.abs(x)))


def _rms(x, g):
    return x * lax.rsqrt(jnp.mean(x * x, axis=-1, keepdims=True) + EPS) * g


def _norm_mod(h, g, scale, shift):
    return _rms(h, g) * (1.0 + scale) + shift


def _dot(a, b):
    return jnp.dot(a, b, preferred_element_type=F32)


def _dot_nt(a, b):
    return lax.dot_general(a, b, (((1,), (1,)), ((), ())), preferred_element_type=F32)


def _mod_kernel(c_ref, w_ref, b_ref, o_ref):
    s = _silu(c_ref[...])
    o_ref[0] = jnp.dot(s, w_ref[0], preferred_element_type=F32, precision=lax.Precision.HIGHEST) + b_ref[0]


def _modulation(cvec, w_ada, b_ada):
    depth, d, n = w_ada.shape
    rows = cvec.shape[0]
    tn = 1024
    return pl.pallas_call(
        _mod_kernel,
        grid=(depth, n // tn),
        in_specs=[pl.BlockSpec((rows, d), lambda l, j: (0, 0)),
                  pl.BlockSpec((1, d, tn), lambda l, j: (l, 0, j)),
                  pl.BlockSpec((1, 1, tn), lambda l, j: (l, 0, j))],
        out_specs=pl.BlockSpec((1, rows, tn), lambda l, j: (l, 0, j)),
        out_shape=jax.ShapeDtypeStruct((depth, rows, n), F32),
        compiler_params=_cparams(("parallel", "parallel")),
        name="modulation",
    )(cvec, w_ada, b_ada.reshape(depth, 1, n))


def _mod_row_map(n_ctx_tiles, tiles_per_lat):
    def index_map(i, *_):
        row = jnp.maximum(i - n_ctx_tiles, 0) // tiles_per_lat + (i >= n_ctx_tiles).astype(jnp.int32)
        return (row, 0, 0)
    return index_map


_IN_SEGS = (("lx", 256), ("ly", 256), ("q", 256), ("k", 256), ("v", 256), ("ga", 256), ("gg", 256),
            ("cq", 256), ("ckv", 128), ("krp", 128), ("krs", 128))


class _TwoSource:
    def __init__(self, a, b, n_a, b_tile0):
        self.a, self.b, self.n_a, self.b_tile0 = a, b, n_a, b_tile0

    def specs(self, tm):
        n_a, b0 = self.n_a, self.b_tile0
        width = self.a.shape[1]
        return [pl.BlockSpec((tm, width), lambda i: (jnp.minimum(i, n_a - 1), 0)),
                pl.BlockSpec((tm, width), lambda i: (jnp.maximum(i - n_a, 0) + b0, 0))]


def _pick(n_a, a_ref, b_ref):
    return jnp.where(pl.program_id(0) < n_a, a_ref[...], b_ref[...])


def _in_kernel(n_a, ha_ref, hb_ref, mod_ref, g_ref, w_ref, *out_refs):
    mod = mod_ref[0]
    h = _pick(n_a, ha_ref, hb_ref)
    u = _norm_mod(h, g_ref[...], mod[:, D_MODEL:2 * D_MODEL], mod[:, 0:D_MODEL]).astype(BF16)
    off = 0
    for (_, width), o_ref in zip(_IN_SEGS, out_refs):
        o_ref[...] = _dot(u, w_ref[:, off:off + width]).astype(o_ref.dtype)
        off += width


def _in_proj(h, t, mods, g, w_small, n_ctx_tiles, tiles_per_lat):
    d = D_MODEL
    tm = TM_TOKEN
    n = w_small.shape[1]
    out_shape = [jax.ShapeDtypeStruct((t, width), F32) for _, width in _IN_SEGS]
    out_specs = [pl.BlockSpec((tm, width), lambda i: (i, 0)) for _, width in _IN_SEGS]
    outs = pl.pallas_call(
        functools.partial(_in_kernel, h.n_a),
        grid=(t // tm,),
        in_specs=h.specs(tm) + [
            pl.BlockSpec((1, 1, mods.shape[-1]), _mod_row_map(n_ctx_tiles, tiles_per_lat)),
            pl.BlockSpec((1, d), lambda i: (0, 0)),
            pl.BlockSpec((d, n), lambda i: (0, 0))],
        out_specs=out_specs,
        out_shape=out_shape,
        compiler_params=_cparams(("parallel",)),
        name="in_proj",
    )(h.a, h.b, mods, g, w_small)
    return dict(zip([s for s, _ in _IN_SEGS], outs))


def _shifted_windows(xw, offsets, length):
    n = xw.shape[0]
    rolled = {0: xw}
    out = {}
    for o in offsets:
        r = o % 8
        if r not in rolled:
            rolled[r] = pltpu.roll(xw, n - r, 0)
        base = o - r
        out[o] = rolled[r][base:base + length]
    return out


def _lru_kernel(lx_ref, ly_ref, h0_ref, cw_ref, cb_ref, wa_ref, ba_ref, wx_ref, bx_ref, lam_ref,
                pa_ref, st_ref, pad_ref, af_ref, bf_ref, ab_ref, bb_ref):
    t = lx_ref.shape[0]
    ch = min(SEQ_CHUNK, t)
    halo = 8
    zeros = jnp.zeros((halo, LRU_WIDTH), F32)
    pad_ref[0:halo, :] = zeros
    pad_ref[halo + t:2 * halo + t, :] = zeros
    pad_ref[halo:halo + t, :] = lx_ref[...]
    pad_l = LRU_CONV // 2
    taps = [halo - pad_l + k for k in range(LRU_CONV)]
    a_refs = (af_ref, ab_ref)
    b_refs = (bf_ref, bb_ref)

    def gates(c, carry):
        r0 = pl.multiple_of(c * ch, ch)
        xw = pad_ref[pl.ds(r0, ch + 2 * halo), :]
        win = _shifted_windows(xw, taps, ch)
        xc = cb_ref[...] + sum(cw_ref[k:k + 1, :] * win[taps[k]] for k in range(LRU_CONV))
        xcb = xc.astype(BF16)
        for d in range(2):
            r = _sigmoid(_dot(xcb, wa_ref[d]) + ba_ref[d])
            i = _sigmoid(_dot(xcb, wx_ref[d]) + bx_ref[d])
            log_a = (-LRU_C) * r * _softplus(-lam_ref[d])
            a = jnp.exp(log_a)
            one_minus_a2 = -jnp.tanh(log_a) * (a * a + 1.0)
            a_refs[d][pl.ds(r0, ch), :] = a
            b_refs[d][pl.ds(r0, ch), :] = jnp.sqrt(one_minus_a2) * (i * xc)
        return carry

    lax.fori_loop(0, t // ch, gates, 0)

    sub = 8
    row = lax.broadcasted_iota(jnp.int32, (sub, LRU_WIDTH), 0)

    def bcast(x, j):
        return jnp.broadcast_to(x[j:j + 1, :], (sub, LRU_WIDTH))

    def scan(c, carry):
        hf, hb = carry
        rf = pl.multiple_of(c * sub, sub)
        rb = pl.multiple_of(t - sub - c * sub, sub)
        a_f, b_f = af_ref[pl.ds(rf, sub), :], bf_ref[pl.ds(rf, sub), :]
        a_b, b_b = ab_ref[pl.ds(rb, sub), :], bb_ref[pl.ds(rb, sub), :]
        out_f = jnp.zeros((sub, LRU_WIDTH), F32)
        out_b = jnp.zeros((sub, LRU_WIDTH), F32)
        for j in range(sub):
            hf = bcast(a_f, j) * hf + bcast(b_f, j)
            out_f = jnp.where(row == j, hf, out_f)
            jb = sub - 1 - j
            hb = bcast(a_b, jb) * hb + bcast(b_b, jb)
            out_b = jnp.where(row == jb, hb, out_b)
        bf_ref[pl.ds(rf, sub), :] = out_f
        bb_ref[pl.ds(rb, sub), :] = out_b
        return hf, hb

    h0 = (jnp.broadcast_to(h0_ref[0, 0:1, :], (sub, LRU_WIDTH)), jnp.broadcast_to(h0_ref[0, 1:2, :], (sub, LRU_WIDTH)))
    hf, hb = lax.fori_loop(0, t // sub, scan, h0)
    st_ref[0, 0:1, :] = hf[0:1, :]
    st_ref[0, 1:2, :] = hb[0:1, :]

    def emit(c, carry):
        r0 = pl.multiple_of(c * ch, ch)
        y = (bf_ref[pl.ds(r0, ch), :] + bb_ref[pl.ds(r0, ch), :]) * _gelu_tanh(ly_ref[pl.ds(r0, ch), :])
        pa_ref[pl.ds(r0, ch), :] = y.astype(pa_ref.dtype)
        return carry

    lax.fori_loop(0, t // ch, emit, 0)


def _lru_branch(lx, ly, h0, p, seq, blk0, nb):
    w = LRU_WIDTH
    const2 = lambda b: (0, 0)
    const3 = lambda b: (0, 0, 0)
    return pl.pallas_call(
        _lru_kernel,
        grid=(nb,),
        in_specs=[pl.BlockSpec((seq, w), lambda b: (blk0 + b, 0)),
                  pl.BlockSpec((seq, w), lambda b: (blk0 + b, 0)),
                  pl.BlockSpec((1, 2, w), lambda b: (b, 0, 0)),
                  pl.BlockSpec((LRU_CONV, w), const2),
                  pl.BlockSpec((1, w), const2),
                  pl.BlockSpec((2, w, w), const3),
                  pl.BlockSpec((2, 1, w), const3),
                  pl.BlockSpec((2, w, w), const3),
                  pl.BlockSpec((2, 1, w), const3),
                  pl.BlockSpec((2, 1, w), const3)],
        out_specs=[pl.BlockSpec((seq, w), lambda b: (b, 0)),
                   pl.BlockSpec((1, 2, w), lambda b: (b, 0, 0))],
        out_shape=[jax.ShapeDtypeStruct((nb * seq, w), BF16),
                   jax.ShapeDtypeStruct((nb, 2, w), F32)],
        scratch_shapes=[pltpu.VMEM((seq + 16, w), F32)] + [pltpu.VMEM((seq, w), F32)] * 4,
        compiler_params=_cparams(("parallel",)),
        name="rglru",
    )(lx, ly, h0, p["lru_conv_w"], p["lru_conv_b"], p["lru_wa"], p["lru_ba"], p["lru_wx"], p["lru_bx"],
      p["lru_lam"])


def _cm_kernel(ga_ref, gg_ref, w_ref, b_ref, lg_ref, lb_ref, pc_ref, pad_ref):
    t = ga_ref.shape[0]
    ch = min(SEQ_CHUNK, t)
    halo = 16
    zeros = jnp.zeros((halo, CM_WIDTH), F32)
    pad_ref[0:halo, :] = zeros
    pad_ref[halo + t:2 * halo + t, :] = zeros

    def glu(c, carry):
        r0 = pl.multiple_of(c * ch, ch)
        pad_ref[pl.ds(halo + r0, ch), :] = ga_ref[pl.ds(r0, ch), :] * _sigmoid(gg_ref[pl.ds(r0, ch), :])
        return carry

    lax.fori_loop(0, t // ch, glu, 0)
    taps = [halo - CM_KERNEL // 2 + k for k in range(CM_KERNEL)]

    def conv(c, carry):
        r0 = pl.multiple_of(c * ch, ch)
        xw = pad_ref[pl.ds(r0, ch + 2 * halo), :]
        win = _shifted_windows(xw, taps, ch)
        z = b_ref[...] + sum(w_ref[k:k + 1, :] * win[taps[k]] for k in range(CM_KERNEL))
        mu = jnp.mean(z, axis=-1, keepdims=True)
        zc = z - mu
        var = jnp.mean(zc * zc, axis=-1, keepdims=True)
        y = zc * lax.rsqrt(var + EPS) * lg_ref[...] + lb_ref[...]
        pc_ref[pl.ds(r0, ch), :] = _silu(y).astype(pc_ref.dtype)
        return carry

    lax.fori_loop(0, t // ch, conv, 0)


def _conformer_branch(ga, gg, p, seq, blk0, nb):
    w = CM_WIDTH
    const2 = lambda b: (0, 0)
    return pl.pallas_call(
        _cm_kernel,
        grid=(nb,),
        in_specs=[pl.BlockSpec((seq, w), lambda b: (blk0 + b, 0)),
                  pl.BlockSpec((seq, w), lambda b: (blk0 + b, 0)),
                  pl.BlockSpec((CM_KERNEL, w), const2),
                  pl.BlockSpec((1, w), const2),
                  pl.BlockSpec((1, w), const2),
                  pl.BlockSpec((1, w), const2)],
        out_specs=pl.BlockSpec((seq, w), lambda b: (b, 0)),
        out_shape=jax.ShapeDtypeStruct((nb * seq, w), BF16),
        scratch_shapes=[pltpu.VMEM((seq + 32, w), F32)],
        compiler_params=_cparams(("parallel",)),
        name="conformer",
    )(ga, gg, p["cm_dw_w"], p["cm_dw_b"], p["cm_ln_g"], p["cm_ln_b"])


def _softmax_pv(scores, values):
    m = functools.reduce(jnp.maximum, [jnp.max(s, axis=-1, keepdims=True) for s in scores])
    ps = [jnp.exp(s - m) for s in scores]
    l = sum(jnp.sum(p, axis=-1, keepdims=True) for p in ps)
    o = sum(_dot(p.astype(BF16), v) for p, v in zip(ps, values))
    return o / l


def _ctx_attn_kernel(q_ref, k_ref, v_ref, cq_ref, ckv_ref, krp_ref, qn_ref, wq_ref, kvn_ref, wkv_ref,
                     pb_ref, pd_ref, ckvn_ref):
    q = q_ref[...].astype(BF16)
    k = k_ref[...].astype(BF16)
    v = v_ref[...].astype(BF16)
    na_scale = NA_HD ** -0.5
    for h in range(NA_HEADS):
        sl = slice(h * NA_HD, (h + 1) * NA_HD)
        s = _dot_nt(q[:, sl], k[:, sl]) * na_scale
        pb_ref[:, sl] = _softmax_pv([s], [v[:, sl]]).astype(pb_ref.dtype)

    ckv_n = _rms(ckv_ref[...], kvn_ref[...])
    ckvn_ref[...] = ckv_n
    qf = _dot(_rms(cq_ref[...], qn_ref[...]).astype(BF16), wq_ref[...]).astype(BF16)
    kv = _dot(ckv_n.astype(BF16), wkv_ref[...]).astype(BF16)
    kr = krp_ref[:, 0:MLA_ROPE].astype(BF16)
    mla_scale = (MLA_NOPE + MLA_ROPE) ** -0.5
    qd = MLA_NOPE + MLA_ROPE
    kd = MLA_NOPE + MLA_VD
    for h in range(MLA_HEADS):
        qn = qf[:, h * qd:h * qd + MLA_NOPE]
        qr = qf[:, h * qd + MLA_NOPE:(h + 1) * qd]
        kn = kv[:, h * kd:h * kd + MLA_NOPE]
        vm = kv[:, h * kd + MLA_NOPE:(h + 1) * kd]
        s = (_dot_nt(qn, kn) + _dot_nt(qr, kr)) * mla_scale
        pd_ref[:, h * MLA_VD:(h + 1) * MLA_VD] = _softmax_pv([s], [vm]).astype(pd_ref.dtype)


def _ctx_attention(z, p, seq, nb):
    const2 = lambda b: (0, 0)
    row = lambda b: (b, 0)
    return pl.pallas_call(
        _ctx_attn_kernel,
        grid=(nb,),
        in_specs=[pl.BlockSpec((seq, 256), row), pl.BlockSpec((seq, 256), row), pl.BlockSpec((seq, 256), row),
                  pl.BlockSpec((seq, MLA_Q_LORA), row), pl.BlockSpec((seq, MLA_KV_LORA), row),
                  pl.BlockSpec((seq, 128), row),
                  pl.BlockSpec((1, MLA_Q_LORA), const2),
                  pl.BlockSpec(p["mla_wq_b"].shape, const2),
                  pl.BlockSpec((1, MLA_KV_LORA), const2),
                  pl.BlockSpec(p["mla_wkv_b"].shape, const2)],
        out_specs=[pl.BlockSpec((seq, 256), row), pl.BlockSpec((seq, 256), row),
                   pl.BlockSpec((seq, MLA_KV_LORA), row)],
        out_shape=[jax.ShapeDtypeStruct((nb * seq, 256), BF16),
                   jax.ShapeDtypeStruct((nb * seq, 256), BF16),
                   jax.ShapeDtypeStruct((nb * seq, MLA_KV_LORA), F32)],
        compiler_params=_cparams(("parallel",)),
        name="ctx_attention",
    )(z["q"], z["k"], z["v"], z["cq"], z["ckv"], z["krp"], p["mla_q_norm"], p["mla_wq_b"],
      p["mla_kv_norm"], p["mla_wkv_b"])


def _na_bias_tables(rpb, rows):
    heads, n_rel_r, n_rel_c = rpb.shape
    w = GRID_W
    u = jnp.concatenate([rpb[:, :, NA_WC - 1:], jnp.zeros((heads, n_rel_r, 2 * w - n_rel_c), rpb.dtype),
                         rpb[:, :, :NA_WC - 1]], axis=-1)
    toep = jnp.tile(u, (1, 1, w))[:, :, :w * (2 * w - 1)].reshape(heads, n_rel_r, w, 2 * w - 1)[..., :w]
    qc = np.arange(w)
    cs = np.clip(qc - NA_WC // 2, 0, w - NA_WC)
    col_ok = (qc[None, :] >= cs[:, None]) & (qc[None, :] < cs[:, None] + NA_WC)
    toep = jnp.where(col_ok, toep, NEG)
    masked = jnp.full((heads, w, w), NEG, rpb.dtype)
    n_steps = rows // NA_QROWS
    tables = []
    for step in (0, 1, n_steps - 1):
        start = int(np.clip(NA_QROWS * step - NA_WR // 2, 0, rows - NA_KROWS))
        block_rows = []
        for a in range(NA_QROWS):
            qr = NA_QROWS * step + a
            rs = int(np.clip(qr - NA_WR // 2, 0, rows - NA_WR))
            blocks = []
            for b in range(NA_KROWS):
                kr = start + b
                blocks.append(toep[:, kr - qr + NA_WR - 1] if rs <= kr < rs + NA_WR else masked)
            block_rows.append(jnp.concatenate(blocks, axis=-1))
        tables.append(jnp.concatenate(block_rows, axis=1))
    return jnp.stack(tables, axis=0)


def _na_lat_kernel(q_ref, k_ref, v_ref, ck_ref, cv_ref, bias_ref, pb_ref):
    i = pl.program_id(1)
    rows = k_ref.shape[0] // GRID_W
    start = jnp.clip(NA_QROWS * i - NA_WR // 2, 0, rows - NA_KROWS) * GRID_W
    start = pl.multiple_of(start, GRID_W)
    nk = NA_KROWS * GRID_W
    q = q_ref[...].astype(BF16)
    kw = k_ref[pl.ds(start, nk), :].astype(BF16)
    vw = v_ref[pl.ds(start, nk), :].astype(BF16)
    ck = ck_ref[0].astype(BF16)
    cv = cv_ref[0].astype(BF16)
    scale = NA_HD ** -0.5
    for h in range(NA_HEADS):
        sl = slice(h * NA_HD, (h + 1) * NA_HD)
        s_loc = _dot_nt(q[:, sl], kw[:, sl]) * scale + bias_ref[0, h]
        s_ctx = _dot_nt(q[:, sl], ck[:, sl]) * scale
        pb_ref[:, sl] = _softmax_pv([s_loc, s_ctx], [vw[:, sl], cv[:, sl]]).astype(pb_ref.dtype)


def _na_latent(z, cache_k, cache_v, bias, seq, blk0, nb):
    tq = NA_QROWS * GRID_W
    steps = seq // tq
    nk = NA_KROWS * GRID_W
    past = cache_k.shape[1]

    def bias_map(b, i):
        return (jnp.where(i == 0, 0, jnp.where(i == steps - 1, 2, 1)), 0, 0, 0)

    return pl.pallas_call(
        _na_lat_kernel,
        grid=(nb, steps),
        in_specs=[pl.BlockSpec((tq, 256), lambda b, i: (blk0 * steps + b * steps + i, 0)),
                  pl.BlockSpec((seq, 256), lambda b, i: (blk0 + b, 0)),
                  pl.BlockSpec((seq, 256), lambda b, i: (blk0 + b, 0)),
                  pl.BlockSpec((1, past, 256), lambda b, i: (b, 0, 0)),
                  pl.BlockSpec((1, past, 256), lambda b, i: (b, 0, 0)),
                  pl.BlockSpec((1, NA_HEADS, tq, nk), bias_map)],
        out_specs=pl.BlockSpec((tq, 256), lambda b, i: (b * steps + i, 0)),
        out_shape=jax.ShapeDtypeStruct((nb * seq, 256), BF16),
        compiler_params=_cparams(("parallel", "arbitrary")),
        name="na_latent",
    )(z["q"], z["k"], z["v"], cache_k, cache_v, bias)


def _rope_tables(seq):
    t = np.arange(seq)
    pos = np.stack([t // GRID_W, t % GRID_W], axis=-1).astype(np.float32)
    n_freq = MLA_ROPE // 4
    inv = jnp.asarray(ROPE_BASE, F32) ** (-jnp.arange(n_freq, dtype=F32) / n_freq)
    ang = jnp.asarray(pos)[:, :, None] * inv
    cos, sin = jnp.cos(ang), jnp.sin(ang)
    c = jnp.stack([cos, cos], axis=2).reshape(seq, MLA_ROPE)
    s = jnp.stack([-sin, sin], axis=2).reshape(seq, MLA_ROPE)
    pad = ((0, 0), (0, 128 - MLA_ROPE))
    return jnp.pad(c, pad), jnp.pad(s, pad)


def _swap_rope_halves(w):
    w4 = w.reshape(w.shape[:-1] + (2, 2, MLA_ROPE // 4))
    return w4[..., ::-1, :].reshape(w.shape)


def _mla_prep_kernel(cq_ref, ckv_ref, krp_ref, krs_ref, cos_ref, sin_ref, qn_ref, wqn_ref, wqr_ref, wqs_ref,
                     kvn_ref, wkt_ref, qm_ref, kl_ref, klt_ref):
    tq = cq_ref.shape[0]
    cos = cos_ref[...]
    sin = sin_ref[...]
    ckv_n = _rms(ckv_ref[...], kvn_ref[...])
    kl_ref[0, :, 0:128] = ckv_n.astype(kl_ref.dtype)
    kl_ref[0, :, 128:256] = (krp_ref[...] * cos + krs_ref[...] * sin).astype(kl_ref.dtype)
    klt_ref[0] = ckv_n.T.astype(klt_ref.dtype)
    cqn = _rms(cq_ref[...], qn_ref[...]).astype(BF16)
    qn = _dot(cqn, wqn_ref[...]).astype(BF16)
    scale = (MLA_NOPE + MLA_ROPE) ** -0.5 * math.log2(math.e)
    for h in range(MLA_HEADS):
        qa = _dot(qn[:, h * MLA_NOPE:(h + 1) * MLA_NOPE], wkt_ref[h])
        qr = _dot(cqn, wqr_ref[h]) * cos + _dot(cqn, wqs_ref[h]) * sin
        qm_ref[0, 0, 0:128, h * tq:(h + 1) * tq] = (qa * scale).T.astype(qm_ref.dtype)
        qm_ref[0, 0, 128:256, h * tq:(h + 1) * tq] = (qr * scale).T.astype(qm_ref.dtype)


def _mla_prep(z, cos, sin, p, seq, blk0, nb):
    tq = TQ_MLA
    steps = seq // tq
    tok = lambda b, i: (blk0 * steps + b * steps + i, 0)
    pos = lambda b, i: (i, 0)
    const2 = lambda b, i: (0, 0)
    const3 = lambda b, i: (0, 0, 0)
    return pl.pallas_call(
        _mla_prep_kernel,
        grid=(nb, steps),
        in_specs=[pl.BlockSpec((tq, MLA_Q_LORA), tok), pl.BlockSpec((tq, MLA_KV_LORA), tok),
                  pl.BlockSpec((tq, 128), tok), pl.BlockSpec((tq, 128), tok),
                  pl.BlockSpec((tq, 128), pos), pl.BlockSpec((tq, 128), pos),
                  pl.BlockSpec((1, MLA_Q_LORA), const2),
                  pl.BlockSpec((MLA_Q_LORA, MLA_HEADS * MLA_NOPE), const2),
                  pl.BlockSpec((MLA_HEADS, MLA_Q_LORA, 128), const3),
                  pl.BlockSpec((MLA_HEADS, MLA_Q_LORA, 128), const3),
                  pl.BlockSpec((1, MLA_KV_LORA), const2),
                  pl.BlockSpec((MLA_HEADS, MLA_NOPE, MLA_KV_LORA), const3)],
        out_specs=[pl.BlockSpec((1, 1, MLA_KDIM, MLA_HEADS * tq), lambda b, i: (b, i, 0, 0)),
                   pl.BlockSpec((1, tq, MLA_KDIM), lambda b, i: (b, i, 0)),
                   pl.BlockSpec((1, MLA_KV_LORA, tq), lambda b, i: (b, 0, i))],
        out_shape=[jax.ShapeDtypeStruct((nb, steps, MLA_KDIM, MLA_HEADS * tq), BF16),
                   jax.ShapeDtypeStruct((nb, seq, MLA_KDIM), BF16),
                   jax.ShapeDtypeStruct((nb, MLA_KV_LORA, seq), BF16)],
        compiler_params=_cparams(("parallel", "parallel")),
        name="mla_prep",
    )(z["cq"], z["ckv"], z["krp"], z["krs"], cos, sin, p["mla_q_norm"], p["mla_wqn"], p["mla_wqr"],
      p["mla_wqs"], p["mla_kv_norm"], p["mla_wkt"])


def _mla_lat_kernel(qt_ref, kl_ref, klt_ref, kc_ref, kct_ref, wvt_ref, pd_ref, m_ref, l_ref, acc_ref,
                    s_buf, p_buf, a_buf):
    qt = qt_ref[0, 0]
    cols = qt.shape[1]
    tq = cols // MLA_HEADS
    n_chunks = kl_ref.shape[1] // TK_MLA
    m_ref[...] = jnp.full((1, cols), NEG, F32)
    l_ref[...] = jnp.zeros((1, cols), F32)
    acc_ref[...] = jnp.zeros((MLA_KV_LORA, cols), F32)

    def softmax_stats(s):
        m_old = m_ref[...]
        m_new = jnp.maximum(m_old, jnp.max(s, axis=0, keepdims=True))
        alpha = jnp.exp2(m_old - m_new)
        p = jnp.exp2(s - m_new)
        l_ref[...] = alpha * l_ref[...] + jnp.sum(p, axis=0, keepdims=True)
        m_ref[...] = m_new
        return alpha, p.astype(BF16)

    alpha, p = softmax_stats(_dot(kc_ref[0], qt))
    acc_ref[...] = alpha * acc_ref[...] + _dot(kct_ref[0], p)

    def chunk(c):
        return pl.ds(pl.multiple_of(c * TK_MLA, TK_MLA), TK_MLA)

    def scores(c, slot):
        s_buf[slot] = _dot(kl_ref[0, chunk(c), :], qt)

    def softmax(slot):
        alpha, p = softmax_stats(s_buf[slot])
        a_buf[slot] = alpha
        p_buf[slot] = p

    def values(c, slot):
        acc_ref[...] = a_buf[slot] * acc_ref[...] + _dot(klt_ref[0, :, chunk(c)], p_buf[slot])

    scores(0, 0)
    scores(1, 1)
    softmax(0)

    def body(j, carry):
        c = 2 * j
        scores(c + 2, 0)
        softmax(1)
        values(c, 0)
        scores(c + 3, 1)
        softmax(0)
        values(c + 1, 1)
        return carry

    lax.fori_loop(0, (n_chunks - 2) // 2, body, 0)
    softmax(1)
    values(n_chunks - 2, 0)
    values(n_chunks - 1, 1)
    o = (acc_ref[...] / l_ref[...]).astype(BF16)
    out_t = jnp.concatenate([_dot(wvt_ref[h], o[:, h * tq:(h + 1) * tq]) for h in range(MLA_HEADS)], axis=0)
    pd_ref[...] = out_t.T.astype(pd_ref.dtype)


def _mla_latent(qm, kl, klt, kc, kct, wvt, seq, nb):
    tq = TQ_MLA
    steps = seq // tq
    cols = MLA_HEADS * tq
    past = kc.shape[1]
    return pl.pallas_call(
        _mla_lat_kernel,
        grid=(nb, steps),
        in_specs=[pl.BlockSpec((1, 1, MLA_KDIM, cols), lambda b, i: (b, i, 0, 0)),
                  pl.BlockSpec((1, seq, MLA_KDIM), lambda b, i: (b, 0, 0)),
                  pl.BlockSpec((1, MLA_KV_LORA, seq), lambda b, i: (b, 0, 0)),
                  pl.BlockSpec((1, past, MLA_KDIM), lambda b, i: (b, 0, 0)),
                  pl.BlockSpec((1, MLA_KV_LORA, past), lambda b, i: (b, 0, 0)),
                  pl.BlockSpec((MLA_HEADS, MLA_VD, MLA_KV_LORA), lambda b, i: (0, 0, 0))],
        out_specs=pl.BlockSpec((tq, MLA_HEADS * MLA_VD), lambda b, i: (b * steps + i, 0)),
        out_shape=jax.ShapeDtypeStruct((nb * seq, MLA_HEADS * MLA_VD), BF16),
        scratch_shapes=[pltpu.VMEM((1, cols), F32), pltpu.VMEM((1, cols), F32),
                        pltpu.VMEM((MLA_KV_LORA, cols), F32),
                        pltpu.VMEM((2, TK_MLA, cols), F32), pltpu.VMEM((2, TK_MLA, cols), BF16),
                        pltpu.VMEM((2, 1, cols), F32)],
        compiler_params=_cparams(("parallel", "arbitrary")),
        name="mla_latent",
    )(qm, kl, klt, kc, kct, wvt)


def _merge_kernel(n_a, ha_ref, hb_ref, mod_ref, g_ref, pa_c, pa_l, pb_c, pb_l, pc_c, pc_l, pd_c, pd_l,
                  wg_ref, wb_ref, wo_ref, o_ref):
    mod = mod_ref[0]
    h = _pick(n_a, ha_ref, hb_ref)
    u = _norm_mod(h, g_ref[...], mod[:, D_MODEL:2 * D_MODEL], mod[:, 0:D_MODEL]).astype(BF16)
    y = None
    for j, (c_ref, l_ref) in enumerate(((pa_c, pa_l), (pb_c, pb_l), (pc_c, pc_l), (pd_c, pd_l))):
        gate = _sigmoid(_dot(u, wg_ref[:, j * D_MODEL:(j + 1) * D_MODEL]))
        term = gate * _dot(_pick(n_a, c_ref, l_ref), wb_ref[j])
        y = term if y is None else y + term
    out = _dot(y.astype(BF16), wo_ref[...])
    o_ref[...] = h + mod[:, 2 * D_MODEL:3 * D_MODEL] * out


def _merge(h, t, mods, g, branches, p, n_ctx_tiles, tiles_per_lat):
    d = D_MODEL
    tm = TM_TOKEN
    const2 = lambda i: (0, 0)
    branch_specs, branch_args = [], []
    for br in branches:
        branch_specs += br.specs(tm)
        branch_args += [br.a, br.b]
    return pl.pallas_call(
        functools.partial(_merge_kernel, h.n_a),
        grid=(t // tm,),
        in_specs=h.specs(tm) + [
            pl.BlockSpec((1, 1, mods.shape[-1]), _mod_row_map(n_ctx_tiles, tiles_per_lat)),
            pl.BlockSpec((1, d), const2)] + branch_specs + [
            pl.BlockSpec((d, N_BRANCH * d), const2),
            pl.BlockSpec((N_BRANCH, 256, d), lambda i: (0, 0, 0)),
            pl.BlockSpec((d, d), const2)],
        out_specs=pl.BlockSpec((tm, d), lambda i: (i, 0)),
        out_shape=jax.ShapeDtypeStruct((t, d), F32),
        compiler_params=_cparams(("parallel",)),
        name="merge",
    )(h.a, h.b, mods, g, *branch_args, p["w_gates"], p["w_branch_out"], p["w_out"])


def _router_gates(logits):
    lane_i = lax.broadcasted_iota(jnp.int32, logits.shape, 1)
    lane = lane_i.astype(F32)
    ninf = jnp.float32(-jnp.inf)

    def first_argmax(x):
        m = jnp.max(x, axis=-1, keepdims=True)
        idx = jnp.min(jnp.where(x == m, lane, jnp.float32(1e9)), axis=-1, keepdims=True)
        return m, idx

    gl = jnp.where(lane_i < MOE_GROUPS, logits, ninf)
    gmax, gsel = first_argmax(gl)
    gp = 1.0 / jnp.sum(jnp.exp(gl - gmax), axis=-1, keepdims=True)
    e_idx = lane_i - MOE_GROUPS
    e_group = lax.shift_right_arithmetic(e_idx, jnp.full_like(e_idx, 2)).astype(F32)
    in_group = (e_idx >= 0) & (e_idx < MOE_EXPERTS) & (e_group == gsel)
    el = jnp.where(in_group, logits, ninf)
    m1, i1 = first_argmax(el)
    m2, i2 = first_argmax(jnp.where(lane == i1, ninf, el))
    e2 = jnp.exp(m2 - m1)
    w1 = gp / (1.0 + e2)
    w2 = gp * e2 / (1.0 + e2)
    return jnp.where(lane == i1, w1, 0.0) + jnp.where(lane == i2, w2, 0.0)


def _moe_kernel(h_ref, mod_ref, g_ref, wr_ref, w13_ref, w2_ref, o_ref, u_ref, gate_ref, acc_ref):
    e = pl.program_id(1)
    mod = mod_ref[0]

    @pl.when(e == 0)
    def _():
        u = _norm_mod(h_ref[...], g_ref[...], mod[:, 4 * D_MODEL:5 * D_MODEL], mod[:, 3 * D_MODEL:4 * D_MODEL])
        ub = u.astype(BF16)
        u_ref[...] = ub
        gate_ref[...] = _router_gates(_dot(ub, wr_ref[...]))
        acc_ref[...] = jnp.zeros_like(acc_ref)

    ub = u_ref[...]
    lane = lax.broadcasted_iota(jnp.int32, gate_ref.shape, 1)
    gate = jnp.sum(jnp.where(lane == e + MOE_GROUPS, gate_ref[...], 0.0), axis=-1, keepdims=True)
    h13 = _dot(ub, w13_ref[0])
    hid = _silu(h13[:, 0:MOE_FF]) * h13[:, MOE_FF:2 * MOE_FF] * gate
    acc_ref[...] += _dot(hid.astype(BF16), w2_ref[0])

    @pl.when(e == MOE_EXPERTS - 1)
    def _():
        o_ref[...] = h_ref[...] + mod[:, 5 * D_MODEL:6 * D_MODEL] * acc_ref[...]


def _moe(h, mods, g, p, n_ctx_tiles, tiles_per_lat):
    t, d = h.shape
    tm = TM_MOE
    nct = n_ctx_tiles * TM_TOKEN // tm
    tpl = tiles_per_lat * TM_TOKEN // tm
    return pl.pallas_call(
        _moe_kernel,
        grid=(t // tm, MOE_EXPERTS),
        in_specs=[pl.BlockSpec((tm, d), lambda i, e: (i, 0)),
                  pl.BlockSpec((1, 1, mods.shape[-1]), _mod_row_map(nct, tpl)),
                  pl.BlockSpec((1, d), lambda i, e: (0, 0)),
                  pl.BlockSpec((d, ROUTER_LANES), lambda i, e: (0, 0)),
                  pl.BlockSpec((1, d, 2 * MOE_FF), lambda i, e: (e, 0, 0)),
                  pl.BlockSpec((1, MOE_FF, d), lambda i, e: (e, 0, 0))],
        out_specs=pl.BlockSpec((tm, d), lambda i, e: (i, 0)),
        out_shape=jax.ShapeDtypeStruct((t, d), F32),
        scratch_shapes=[pltpu.VMEM((tm, d), BF16), pltpu.VMEM((tm, ROUTER_LANES), F32),
                        pltpu.VMEM((tm, d), F32)],
        compiler_params=_cparams(("parallel", "arbitrary")),
        name="moe",
    )(h, mods, g, p["w_router"], p["w13"], p["w2"])


def _final_kernel(h_ref, g_ref, o_ref):
    o_ref[...] = _rms(h_ref[...], g_ref[...])


def _final_norm(h, g, blk0, rows):
    d = h.shape[1]
    tm = TM_TOKEN
    return pl.pallas_call(
        _final_kernel,
        grid=(rows // tm,),
        in_specs=[pl.BlockSpec((tm, d), lambda i: (blk0 + i, 0)), pl.BlockSpec((1, d), lambda i: (0, 0))],
        out_specs=pl.BlockSpec((tm, d), lambda i: (i, 0)),
        out_shape=jax.ShapeDtypeStruct((rows, d), F32),
        compiler_params=_cparams(("parallel",)),
        name="final_norm",
    )(h, g)


def _block_diag(w):
    nd, nb, bw, _ = w.shape
    eye = jnp.eye(nb, dtype=w.dtype)
    return jnp.einsum("dnij,nm->dnimj", w, eye).reshape(nd, nb * bw, nb * bw)


def _layer_params(l, w_in, lru_conv_w, lru_conv_b, lru_wa, lru_ba, lru_wx, lru_bx, lru_lam, w_lru_out, w_na_out,
                  cm_dw_w, cm_dw_b, cm_ln_g, cm_ln_b, w_cm_out, mla_q_norm, mla_wq_b, mla_kv_norm, mla_wkv_b,
                  w_mla_out, w_out, moe_w_group, moe_w_expert, moe_w1, moe_w3, moe_w2):
    wi = w_in[l]
    n_small = 2 * LRU_WIDTH + 3 * NA_HEADS * NA_HD + 2 * CM_WIDTH + MLA_Q_LORA + MLA_KV_LORA
    kr_cols = wi[:, n_small:n_small + MLA_ROPE]
    zpad = jnp.zeros((D_MODEL, 128 - MLA_ROPE), wi.dtype)
    w_small = jnp.concatenate([wi[:, :n_small], kr_cols, zpad, _swap_rope_halves(kr_cols), zpad], axis=1)
    w_gates = wi[:, n_small + MLA_ROPE:]

    qd = MLA_NOPE + MLA_ROPE
    wq = mla_wq_b[l].reshape(MLA_Q_LORA, MLA_HEADS, qd)
    wqn = wq[:, :, :MLA_NOPE].reshape(MLA_Q_LORA, MLA_HEADS * MLA_NOPE)
    wqr = jnp.moveaxis(wq[:, :, MLA_NOPE:], 1, 0)
    rpad = ((0, 0), (0, 0), (0, 128 - MLA_ROPE))
    wkv = mla_wkv_b[l].reshape(MLA_KV_LORA, MLA_HEADS, MLA_NOPE + MLA_VD)
    wkt = jnp.transpose(wkv[:, :, :MLA_NOPE], (1, 2, 0))
    wvt = jnp.transpose(wkv[:, :, MLA_NOPE:], (1, 2, 0))

    router = jnp.concatenate([moe_w_group[l], moe_w_expert[l]], axis=1)
    router = jnp.pad(router, ((0, 0), (0, ROUTER_LANES - router.shape[1])))
    row = lambda a: a.reshape(1, -1)
    return dict(
        w_small=w_small.astype(BF16), w_gates=w_gates.astype(BF16),
        lru_conv_w=lru_conv_w[l], lru_conv_b=row(lru_conv_b[l]),
        lru_wa=_block_diag(lru_wa[l]).astype(BF16), lru_ba=lru_ba[l][:, None, :],
        lru_wx=_block_diag(lru_wx[l]).astype(BF16), lru_bx=lru_bx[l][:, None, :],
        lru_lam=lru_lam[l][:, None, :],
        cm_dw_w=cm_dw_w[l], cm_dw_b=row(cm_dw_b[l]), cm_ln_g=row(cm_ln_g[l]), cm_ln_b=row(cm_ln_b[l]),
        mla_q_norm=row(mla_q_norm[l]), mla_kv_norm=row(mla_kv_norm[l]),
        mla_wq_b=mla_wq_b[l].astype(BF16), mla_wkv_b=mla_wkv_b[l].astype(BF16),
        mla_wqn=wqn.astype(BF16), mla_wqr=jnp.pad(wqr, rpad).astype(BF16),
        mla_wqs=jnp.pad(_swap_rope_halves(wqr), rpad).astype(BF16),
        mla_wkt=wkt.astype(BF16), mla_wvt=wvt.astype(BF16),
        w_branch_out=jnp.stack([w_lru_out[l], w_na_out[l], w_cm_out[l], w_mla_out[l]], axis=0).astype(BF16),
        w_out=w_out[l].astype(BF16),
        w_router=router.astype(BF16),
        w13=jnp.concatenate([moe_w1[l], moe_w3[l]], axis=-1).astype(BF16),
        w2=moe_w2[l].astype(BF16),
    )


def kernel(x_prompt, x_sample, cache_na_k, cache_na_v, cache_mla_ckv, cache_mla_krope, state_lru, c, c_ctx,
           w_ada, b_ada, norm1_g, w_in, lru_conv_w, lru_conv_b, lru_wa, lru_ba, lru_wx, lru_bx, lru_lam,
           w_lru_out, na_rpb, w_na_out, cm_dw_w, cm_dw_b, cm_ln_g, cm_ln_b, w_cm_out, mla_q_norm, mla_wq_b,
           mla_kv_norm, mla_wkv_b, w_mla_out, w_out, norm2_g, moe_w_group, moe_w_expert, moe_w1, moe_w3,
           moe_w2, final_g):
    nb_c, seq_c, d = x_prompt.shape
    nb_l, seq_l, _ = x_sample.shape
    depth = w_in.shape[0]
    past = cache_na_k.shape[2]
    t_ctx = nb_c * seq_c
    t_lat = nb_l * seq_l
    assert d == D_MODEL and seq_l % (GRID_W * NA_QROWS) == 0 and seq_l % seq_c == 0
    assert t_ctx % TM_MOE == 0 and seq_l % TM_MOE == 0 and seq_c % SEQ_CHUNK == 0 and seq_l % TK_MLA == 0
    n_ctx_tiles = t_ctx // TM_TOKEN
    tiles_per_lat = seq_l // TM_TOKEN
    lat_blk0 = t_ctx // seq_l
    assert lat_blk0 * seq_l == t_ctx and (seq_l // TK_MLA) % 2 == 0

    t_all = t_ctx + t_lat
    h = _TwoSource(x_prompt.reshape(t_ctx, d), x_sample.reshape(t_lat, d), n_ctx_tiles, 0)
    n_cond = 1 + nb_l
    cvec = jnp.concatenate([c_ctx[None, :], c, jnp.zeros((-n_cond % 8, d), F32)], axis=0)
    mods = _modulation(cvec, w_ada, b_ada)
    cos, sin = _rope_tables(seq_l)
    zero_state = jnp.zeros((nb_c, 2, LRU_WIDTH), F32)

    st_k, st_v, st_ckv, st_kr, st_lru = [], [], [], [], []
    for l in range(depth):
        p = _layer_params(l, w_in, lru_conv_w, lru_conv_b, lru_wa, lru_ba, lru_wx, lru_bx, lru_lam, w_lru_out,
                          w_na_out, cm_dw_w, cm_dw_b, cm_ln_g, cm_ln_b, w_cm_out, mla_q_norm, mla_wq_b,
                          mla_kv_norm, mla_wkv_b, w_mla_out, w_out, moe_w_group, moe_w_expert, moe_w1, moe_w3,
                          moe_w2)
        mod_l = mods[l].reshape(mods.shape[1], 1, mods.shape[2])
        g1 = norm1_g[l].reshape(1, d)
        g2 = norm2_g[l].reshape(1, d)
        z = _in_proj(h, t_all, mod_l, g1, p["w_small"], n_ctx_tiles, tiles_per_lat)

        pa_c, lru_c = _lru_branch(z["lx"], z["ly"], zero_state, p, seq_c, 0, nb_c)
        pc_c = _conformer_branch(z["ga"], z["gg"], p, seq_c, 0, nb_c)
        pb_c, pd_c, ckvn_c = _ctx_attention(z, p, seq_c, nb_c)

        pa_l, _ = _lru_branch(z["lx"], z["ly"], state_lru[:, l], p, seq_l, lat_blk0, nb_l)
        pc_l = _conformer_branch(z["ga"], z["gg"], p, seq_l, lat_blk0, nb_l)
        bias = _na_bias_tables(na_rpb[l], seq_l // GRID_W)
        pb_l = _na_latent(z, cache_na_k[:, l].reshape(nb_l, past, NA_HEADS * NA_HD),
                          cache_na_v[:, l].reshape(nb_l, past, NA_HEADS * NA_HD), bias, seq_l, lat_blk0, nb_l)
        qm, kl, klt = _mla_prep(z, cos, sin, p, seq_l, lat_blk0, nb_l)
        kc = jnp.concatenate([cache_mla_ckv[:, l], cache_mla_krope[:, l],
                              jnp.zeros((nb_l, past, MLA_KDIM - MLA_KV_LORA - MLA_ROPE), F32)], axis=-1).astype(BF16)
        kct = jnp.swapaxes(cache_mla_ckv[:, l], 1, 2).astype(BF16)
        pd_l = _mla_latent(qm, kl, klt, kc, kct, p["mla_wvt"], seq_l, nb_l)

        branches = [_TwoSource(br_c, br_l, n_ctx_tiles, 0)
                    for br_c, br_l in ((pa_c, pa_l), (pb_c, pb_l), (pc_c, pc_l), (pd_c, pd_l))]
        h_mid = _merge(h, t_all, mod_l, g1, branches, p, n_ctx_tiles, tiles_per_lat)
        h_new = _moe(h_mid, mod_l, g2, p, n_ctx_tiles, tiles_per_lat)
        h = _TwoSource(h_new, h_new, n_ctx_tiles, n_ctx_tiles)

        st_k.append(z["k"][:t_ctx].reshape(nb_c, seq_c, NA_HEADS, NA_HD))
        st_v.append(z["v"][:t_ctx].reshape(nb_c, seq_c, NA_HEADS, NA_HD))
        st_ckv.append(ckvn_c.reshape(nb_c, seq_c, MLA_KV_LORA))
        st_kr.append(z["krp"][:t_ctx, :MLA_ROPE].reshape(nb_c, seq_c, MLA_ROPE))
        st_lru.append(lru_c)

    fg = final_g.reshape(1, d)
    y_prompt = _final_norm(h_new, fg, 0, t_ctx).reshape(nb_c, seq_c, d)
    y_sample = _final_norm(h_new, fg, n_ctx_tiles, t_lat).reshape(nb_l, seq_l, d)
    return (y_prompt, y_sample, jnp.stack(st_k, axis=1), jnp.stack(st_v, axis=1), jnp.stack(st_ckv, axis=1),
            jnp.stack(st_kr, axis=1), jnp.stack(st_lru, axis=1))
```

```python
import functools
import math

import numpy as np
import jax
import jax.numpy as jnp
from jax import lax
from jax.experimental import pallas as pl
from jax.experimental.pallas import tpu as pltpu

F32 = jnp.float32
BF16 = jnp.bfloat16

D_MODEL = 1024
GRID_W = 64
EPS = 1e-6
NEG = -1e30
N_BRANCH = 4
LRU_WIDTH = 256
LRU_BLOCKS = 4
LRU_CONV = 4
LRU_C = 8.0
NA_HEADS = 4
NA_HD = 64
NA_WR = 8
NA_WC = 16
NA_QROWS = 4
NA_KROWS = NA_QROWS + NA_WR
CM_WIDTH = 256
CM_KERNEL = 31
MLA_HEADS = 4
MLA_Q_LORA = 256
MLA_KV_LORA = 128
MLA_NOPE = 64
MLA_ROPE = 32
MLA_VD = 64
MLA_KDIM = 256
ROPE_BASE = 10000.0
MOE_GROUPS = 4
MOE_PER_GROUP = 4
MOE_EXPERTS = 16
MOE_FF = 256
ROUTER_LANES = 128

TM_TOKEN = 512
TM_MOE = 1024
TQ_MLA = 256
TK_MLA = 512
SEQ_CHUNK = 256
VMEM_LIMIT = 56 * 1024 * 1024


def _cparams(sem, vmem=VMEM_LIMIT):
    return pltpu.CompilerParams(dimension_semantics=sem, vmem_limit_bytes=vmem)


LOG2E = math.log2(math.e)


def _sigmoid(x):
    return 0.5 * jnp.tanh(0.5 * x) + 0.5


def _silu(x):
    return x * _sigmoid(x)


def _gelu_tanh(x):
    c = math.sqrt(2.0 / math.pi)
    return x * (0.5 * (1.0 + jnp.tanh(c * (x + 0.044715 * (x * x * x)))))


def _softplus(x):
    return jnp.maximum(x, 0.0) + jnp.log1p(jnp.exp(-jnp.abs(x)))


def _rms(x, g):
    return x * lax.rsqrt(jnp.mean(x * x, axis=-1, keepdims=True) + EPS) * g


def _norm_mod(h, g, scale, shift):
    return _rms(h, g) * (1.0 + scale) + shift


def _dot(a, b):
    return jnp.dot(a, b, preferred_element_type=F32)


def _dot_nt(a, b):
    return lax.dot_general(a, b, (((1,), (1,)), ((), ())), preferred_element_type=F32)


def _mod_kernel(c_ref, w_ref, b_ref, o_ref):
    s = _silu(c_ref[...])
    o_ref[0] = jnp.dot(s, w_ref[0], preferred_element_type=F32, precision=lax.Precision.HIGHEST) + b_ref[0]


def _modulation(cvec, w_ada, b_ada):
    depth, d, n = w_ada.shape
    rows = cvec.shape[0]
    tn = 1024
    return pl.pallas_call(
        _mod_kernel,
        grid=(depth, n // tn),
        in_specs=[pl.BlockSpec((rows, d), lambda l, j: (0, 0)),
                  pl.BlockSpec((1, d, tn), lambda l, j: (l, 0, j)),
                  pl.BlockSpec((1, 1, tn), lambda l, j: (l, 0, j))],
        out_specs=pl.BlockSpec((1, rows, tn), lambda l, j: (l, 0, j)),
        out_shape=jax.ShapeDtypeStruct((depth, rows, n), F32),
        compiler_params=_cparams(("parallel", "parallel")),
        name="modulation",
    )(cvec, w_ada, b_ada.reshape(depth, 1, n))


def _mod_row_map(n_ctx_tiles, tiles_per_lat):
    def index_map(i, *_):
        row = jnp.maximum(i - n_ctx_tiles, 0) // tiles_per_lat + (i >= n_ctx_tiles).astype(jnp.int32)
        return (row, 0, 0)
    return index_map


_IN_SEGS = (("lx", 256), ("ly", 256), ("q", 256), ("k", 256), ("v", 256), ("ga", 256), ("gg", 256),
            ("cq", 256), ("ckv", 128), ("krp", 128), ("krs", 128))


class _TwoSource:
    def __init__(self, a, b, n_a, b_tile0):
        self.a, self.b, self.n_a, self.b_tile0 = a, b, n_a, b_tile0

    def specs(self, tm):
        n_a, b0 = self.n_a, self.b_tile0
        width = self.a.shape[1]
        return [pl.BlockSpec((tm, width), lambda i: (jnp.minimum(i, n_a - 1), 0)),
                pl.BlockSpec((tm, width), lambda i: (jnp.maximum(i - n_a, 0) + b0, 0))]


def _pick(n_a, a_ref, b_ref):
    return jnp.where(pl.program_id(0) < n_a, a_ref[...], b_ref[...])


def _in_kernel(n_a, ha_ref, hb_ref, mod_ref, g_ref, w_ref, *out_refs):
    mod = mod_ref[0]
    h = _pick(n_a, ha_ref, hb_ref)
    u = _norm_mod(h, g_ref[...], mod[:, D_MODEL:2 * D_MODEL], mod[:, 0:D_MODEL]).astype(BF16)
    off = 0
    for (_, width), o_ref in zip(_IN_SEGS, out_refs):
        o_ref[...] = _dot(u, w_ref[:, off:off + width]).astype(o_ref.dtype)
        off += width


def _in_proj(h, t, mods, g, w_small, n_ctx_tiles, tiles_per_lat):
    d = D_MODEL
    tm = TM_TOKEN
    n = w_small.shape[1]
    out_shape = [jax.ShapeDtypeStruct((t, width), F32) for _, width in _IN_SEGS]
    out_specs = [pl.BlockSpec((tm, width), lambda i: (i, 0)) for _, width in _IN_SEGS]
    outs = pl.pallas_call(
        functools.partial(_in_kernel, h.n_a),
        grid=(t // tm,),
        in_specs=h.specs(tm) + [
            pl.BlockSpec((1, 1, mods.shape[-1]), _mod_row_map(n_ctx_tiles, tiles_per_lat)),
            pl.BlockSpec((1, d), lambda i: (0, 0)),
            pl.BlockSpec((d, n), lambda i: (0, 0))],
        out_specs=out_specs,
        out_shape=out_shape,
        compiler_params=_cparams(("parallel",)),
        name="in_proj",
    )(h.a, h.b, mods, g, w_small)
    return dict(zip([s for s, _ in _IN_SEGS], outs))


def _shifted_windows(xw, offsets, length):
    n = xw.shape[0]
    rolled = {0: xw}
    out = {}
    for o in offsets:
        r = o % 8
        if r not in rolled:
            rolled[r] = pltpu.roll(xw, n - r, 0)
        base = o - r
        out[o] = rolled[r][base:base + length]
    return out


def _lru_kernel(lx_ref, ly_ref, h0_ref, cw_ref, cb_ref, wa_ref, ba_ref, wx_ref, bx_ref, lam_ref,
                pa_ref, st_ref, pad_ref, af_ref, bf_ref, ab_ref, bb_ref):
    t = lx_ref.shape[0]
    ch = min(SEQ_CHUNK, t)
    halo = 8
    zeros = jnp.zeros((halo, LRU_WIDTH), F32)
    pad_ref[0:halo, :] = zeros
    pad_ref[halo + t:2 * halo + t, :] = zeros
    pad_ref[halo:halo + t, :] = lx_ref[...]
    pad_l = LRU_CONV // 2
    taps = [halo - pad_l + k for k in range(LRU_CONV)]
    a_refs = (af_ref, ab_ref)
    b_refs = (bf_ref, bb_ref)

    def gates(c, carry):
        r0 = pl.multiple_of(c * ch, ch)
        xw = pad_ref[pl.ds(r0, ch + 2 * halo), :]
        win = _shifted_windows(xw, taps, ch)
        xc = cb_ref[...] + sum(cw_ref[k:k + 1, :] * win[taps[k]] for k in range(LRU_CONV))
        xcb = xc.astype(BF16)
        for d in range(2):
            r = _sigmoid(_dot(xcb, wa_ref[d]) + ba_ref[d])
            i = _sigmoid(_dot(xcb, wx_ref[d]) + bx_ref[d])
            log_a = (-LRU_C) * r * _softplus(-lam_ref[d])
            a = jnp.exp(log_a)
            one_minus_a2 = -jnp.tanh(log_a) * (a * a + 1.0)
            a_refs[d][pl.ds(r0, ch), :] = a
            b_refs[d][pl.ds(r0, ch), :] = jnp.sqrt(one_minus_a2) * (i * xc)
        return carry

    lax.fori_loop(0, t // ch, gates, 0)

    sub = 8
    row = lax.broadcasted_iota(jnp.int32, (sub, LRU_WIDTH), 0)

    def bcast(x, j):
        return jnp.broadcast_to(x[j:j + 1, :], (sub, LRU_WIDTH))

    def scan(c, carry):
        hf, hb = carry
        rf = pl.multiple_of(c * sub, sub)
        rb = pl.multiple_of(t - sub - c * sub, sub)
        a_f, b_f = af_ref[pl.ds(rf, sub), :], bf_ref[pl.ds(rf, sub), :]
        a_b, b_b = ab_ref[pl.ds(rb, sub), :], bb_ref[pl.ds(rb, sub), :]
        out_f = jnp.zeros((sub, LRU_WIDTH), F32)
        out_b = jnp.zeros((sub, LRU_WIDTH), F32)
        for j in range(sub):
            hf = bcast(a_f, j) * hf + bcast(b_f, j)
            out_f = jnp.where(row == j, hf, out_f)
            jb = sub - 1 - j
            hb = bcast(a_b, jb) * hb + bcast(b_b, jb)
            out_b = jnp.where(row == jb, hb, out_b)
        bf_ref[pl.ds(rf, sub), :] = out_f
        bb_ref[pl.ds(rb, sub), :] = out_b
        return hf, hb

    h0 = (jnp.broadcast_to(h0_ref[0, 0:1, :], (sub, LRU_WIDTH)), jnp.broadcast_to(h0_ref[0, 1:2, :], (sub, LRU_WIDTH)))
    hf, hb = lax.fori_loop(0, t // sub, scan, h0)
    st_ref[0, 0:1, :] = hf[0:1, :]
    st_ref[0, 1:2, :] = hb[0:1, :]

    def emit(c, carry):
        r0 = pl.multiple_of(c * ch, ch)
        y = (bf_ref[pl.ds(r0, ch), :] + bb_ref[pl.ds(r0, ch), :]) * _gelu_tanh(ly_ref[pl.ds(r0, ch), :])
        pa_ref[pl.ds(r0, ch), :] = y.astype(pa_ref.dtype)
        return carry

    lax.fori_loop(0, t // ch, emit, 0)


def _lru_branch(lx, ly, h0, p, seq, blk0, nb):
    w = LRU_WIDTH
    const2 = lambda b: (0, 0)
    const3 = lambda b: (0, 0, 0)
    return pl.pallas_call(
        _lru_kernel,
        grid=(nb,),
        in_specs=[pl.BlockSpec((seq, w), lambda b: (blk0 + b, 0)),
                  pl.BlockSpec((seq, w), lambda b: (blk0 + b, 0)),
                  pl.BlockSpec((1, 2, w), lambda b: (b, 0, 0)),
                  pl.BlockSpec((LRU_CONV, w), const2),
                  pl.BlockSpec((1, w), const2),
                  pl.BlockSpec((2, w, w), const3),
                  pl.BlockSpec((2, 1, w), const3),
                  pl.BlockSpec((2, w, w), const3),
                  pl.BlockSpec((2, 1, w), const3),
                  pl.BlockSpec((2, 1, w), const3)],
        out_specs=[pl.BlockSpec((seq, w), lambda b: (b, 0)),
                   pl.BlockSpec((1, 2, w), lambda b: (b, 0, 0))],
        out_shape=[jax.ShapeDtypeStruct((nb * seq, w), BF16),
                   jax.ShapeDtypeStruct((nb, 2, w), F32)],
        scratch_shapes=[pltpu.VMEM((seq + 16, w), F32)] + [pltpu.VMEM((seq, w), F32)] * 4,
        compiler_params=_cparams(("parallel",)),
        name="rglru",
    )(lx, ly, h0, p["lru_conv_w"], p["lru_conv_b"], p["lru_wa"], p["lru_ba"], p["lru_wx"], p["lru_bx"],
      p["lru_lam"])


def _cm_kernel(ga_ref, gg_ref, w_ref, b_ref, lg_ref, lb_ref, pc_ref, pad_ref):
    t = ga_ref.shape[0]
    ch = min(SEQ_CHUNK, t)
    halo = 16
    zeros = jnp.zeros((halo, CM_WIDTH), F32)
    pad_ref[0:halo, :] = zeros
    pad_ref[halo + t:2 * halo + t, :] = zeros

    def glu(c, carry):
        r0 = pl.multiple_of(c * ch, ch)
        pad_ref[pl.ds(halo + r0, ch), :] = ga_ref[pl.ds(r0, ch), :] * _sigmoid(gg_ref[pl.ds(r0, ch), :])
        return carry

    lax.fori_loop(0, t // ch, glu, 0)
    taps = [halo - CM_KERNEL // 2 + k for k in range(CM_KERNEL)]

    def conv(c, carry):
        r0 = pl.multiple_of(c * ch, ch)
        xw = pad_ref[pl.ds(r0, ch + 2 * halo), :]
        win = _shifted_windows(xw, taps, ch)
        z = b_ref[...] + sum(w_ref[k:k + 1, :] * win[taps[k]] for k in range(CM_KERNEL))
        mu = jnp.mean(z, axis=-1, keepdims=True)
        zc = z - mu
        var = jnp.mean(zc * zc, axis=-1, keepdims=True)
        y = zc * lax.rsqrt(var + EPS) * lg_ref[...] + lb_ref[...]
        pc_ref[pl.ds(r0, ch), :] = _silu(y).astype(pc_ref.dtype)
        return carry

    lax.fori_loop(0, t // ch, conv, 0)


def _conformer_branch(ga, gg, p, seq, blk0, nb):
    w = CM_WIDTH
    const2 = lambda b: (0, 0)
    return pl.pallas_call(
        _cm_kernel,
        grid=(nb,),
        in_specs=[pl.BlockSpec((seq, w), lambda b: (blk0 + b, 0)),
                  pl.BlockSpec((seq, w), lambda b: (blk0 + b, 0)),
                  pl.BlockSpec((CM_KERNEL, w), const2),
                  pl.BlockSpec((1, w), const2),
                  pl.BlockSpec((1, w), const2),
                  pl.BlockSpec((1, w), const2)],
        out_specs=pl.BlockSpec((seq, w), lambda b: (b, 0)),
        out_shape=jax.ShapeDtypeStruct((nb * seq, w), BF16),
        scratch_shapes=[pltpu.VMEM((seq + 32, w), F32)],
        compiler_params=_cparams(("parallel",)),
        name="conformer",
    )(ga, gg, p["cm_dw_w"], p["cm_dw_b"], p["cm_ln_g"], p["cm_ln_b"])


def _softmax_pv(scores, values):
    m = functools.reduce(jnp.maximum, [jnp.max(s, axis=-1, keepdims=True) for s in scores])
    ps = [jnp.exp2(s - m) for s in scores]
    l = sum(jnp.sum(p, axis=-1, keepdims=True) for p in ps)
    o = sum(_dot(p.astype(BF16), v) for p, v in zip(ps, values))
    return o / l


def _ctx_attn_kernel(q_ref, k_ref, v_ref, cq_ref, ckv_ref, krp_ref, qn_ref, wq_ref, kvn_ref, wkv_ref,
                     pb_ref, pd_ref, ckvn_ref):
    q = (q_ref[...] * (NA_HD ** -0.5 * LOG2E)).astype(BF16)
    k = k_ref[...].astype(BF16)
    v = v_ref[...].astype(BF16)
    for h in range(NA_HEADS):
        sl = slice(h * NA_HD, (h + 1) * NA_HD)
        s = _dot_nt(q[:, sl], k[:, sl])
        pb_ref[:, sl] = _softmax_pv([s], [v[:, sl]]).astype(pb_ref.dtype)

    ckv_n = _rms(ckv_ref[...], kvn_ref[...])
    ckvn_ref[...] = ckv_n
    mla_scale = (MLA_NOPE + MLA_ROPE) ** -0.5 * LOG2E
    qf = (_dot(_rms(cq_ref[...], qn_ref[...]).astype(BF16), wq_ref[...]) * mla_scale).astype(BF16)
    kv = _dot(ckv_n.astype(BF16), wkv_ref[...]).astype(BF16)
    kr = krp_ref[:, 0:MLA_ROPE].astype(BF16)
    qd = MLA_NOPE + MLA_ROPE
    kd = MLA_NOPE + MLA_VD
    for h in range(MLA_HEADS):
        qn = qf[:, h * qd:h * qd + MLA_NOPE]
        qr = qf[:, h * qd + MLA_NOPE:(h + 1) * qd]
        kn = kv[:, h * kd:h * kd + MLA_NOPE]
        vm = kv[:, h * kd + MLA_NOPE:(h + 1) * kd]
        s = _dot_nt(qn, kn) + _dot_nt(qr, kr)
        pd_ref[:, h * MLA_VD:(h + 1) * MLA_VD] = _softmax_pv([s], [vm]).astype(pd_ref.dtype)


def _ctx_attention(z, p, seq, nb):
    const2 = lambda b: (0, 0)
    row = lambda b: (b, 0)
    return pl.pallas_call(
        _ctx_attn_kernel,
        grid=(nb,),
        in_specs=[pl.BlockSpec((seq, 256), row), pl.BlockSpec((seq, 256), row), pl.BlockSpec((seq, 256), row),
                  pl.BlockSpec((seq, MLA_Q_LORA), row), pl.BlockSpec((seq, MLA_KV_LORA), row),
                  pl.BlockSpec((seq, 128), row),
                  pl.BlockSpec((1, MLA_Q_LORA), const2),
                  pl.BlockSpec(p["mla_wq_b"].shape, const2),
                  pl.BlockSpec((1, MLA_KV_LORA), const2),
                  pl.BlockSpec(p["mla_wkv_b"].shape, const2)],
        out_specs=[pl.BlockSpec((seq, 256), row), pl.BlockSpec((seq, 256), row),
                   pl.BlockSpec((seq, MLA_KV_LORA), row)],
        out_shape=[jax.ShapeDtypeStruct((nb * seq, 256), BF16),
                   jax.ShapeDtypeStruct((nb * seq, 256), BF16),
                   jax.ShapeDtypeStruct((nb * seq, MLA_KV_LORA), F32)],
        compiler_params=_cparams(("parallel",)),
        name="ctx_attention",
    )(z["q"], z["k"], z["v"], z["cq"], z["ckv"], z["krp"], p["mla_q_norm"], p["mla_wq_b"],
      p["mla_kv_norm"], p["mla_wkv_b"])


def _na_bias_tables(rpb, rows):
    heads, n_rel_r, n_rel_c = rpb.shape
    w = GRID_W
    rpb = rpb * LOG2E
    u = jnp.concatenate([rpb[:, :, NA_WC - 1:], jnp.zeros((heads, n_rel_r, 2 * w - n_rel_c), rpb.dtype),
                         rpb[:, :, :NA_WC - 1]], axis=-1)
    toep = jnp.tile(u, (1, 1, w))[:, :, :w * (2 * w - 1)].reshape(heads, n_rel_r, w, 2 * w - 1)[..., :w]
    qc = np.arange(w)
    cs = np.clip(qc - NA_WC // 2, 0, w - NA_WC)
    col_ok = (qc[None, :] >= cs[:, None]) & (qc[None, :] < cs[:, None] + NA_WC)
    toep = jnp.where(col_ok, toep, NEG)
    masked = jnp.full((heads, w, w), NEG, rpb.dtype)
    n_steps = rows // NA_QROWS
    tables = []
    for step in (0, 1, n_steps - 1):
        start = int(np.clip(NA_QROWS * step - NA_WR // 2, 0, rows - NA_KROWS))
        block_rows = []
        for a in range(NA_QROWS):
            qr = NA_QROWS * step + a
            rs = int(np.clip(qr - NA_WR // 2, 0, rows - NA_WR))
            blocks = []
            for b in range(NA_KROWS):
                kr = start + b
                blocks.append(toep[:, kr - qr + NA_WR - 1] if rs <= kr < rs + NA_WR else masked)
            block_rows.append(jnp.concatenate(blocks, axis=-1))
        tables.append(jnp.concatenate(block_rows, axis=1))
    return jnp.stack(tables, axis=0)


def _na_lat_kernel(q_ref, k_ref, v_ref, ck_ref, cv_ref, bias_ref, pb_ref):
    i = pl.program_id(1)
    rows = k_ref.shape[0] // GRID_W
    start = jnp.clip(NA_QROWS * i - NA_WR // 2, 0, rows - NA_KROWS) * GRID_W
    start = pl.multiple_of(start, GRID_W)
    nk = NA_KROWS * GRID_W
    q = (q_ref[...] * (NA_HD ** -0.5 * LOG2E)).astype(BF16)
    kw = k_ref[pl.ds(start, nk), :].astype(BF16)
    vw = v_ref[pl.ds(start, nk), :].astype(BF16)
    ck = ck_ref[0].astype(BF16)
    cv = cv_ref[0].astype(BF16)
    for h in range(NA_HEADS):
        sl = slice(h * NA_HD, (h + 1) * NA_HD)
        s_loc = _dot_nt(q[:, sl], kw[:, sl]) + bias_ref[0, h]
        s_ctx = _dot_nt(q[:, sl], ck[:, sl])
        pb_ref[:, sl] = _softmax_pv([s_loc, s_ctx], [vw[:, sl], cv[:, sl]]).astype(pb_ref.dtype)


def _na_latent(z, cache_k, cache_v, bias, seq, blk0, nb):
    tq = NA_QROWS * GRID_W
    steps = seq // tq
    nk = NA_KROWS * GRID_W
    past = cache_k.shape[1]

    def bias_map(b, i):
        return (jnp.where(i == 0, 0, jnp.where(i == steps - 1, 2, 1)), 0, 0, 0)

    return pl.pallas_call(
        _na_lat_kernel,
        grid=(nb, steps),
        in_specs=[pl.BlockSpec((tq, 256), lambda b, i: (blk0 * steps + b * steps + i, 0)),
                  pl.BlockSpec((seq, 256), lambda b, i: (blk0 + b, 0)),
                  pl.BlockSpec((seq, 256), lambda b, i: (blk0 + b, 0)),
                  pl.BlockSpec((1, past, 256), lambda b, i: (b, 0, 0)),
                  pl.BlockSpec((1, past, 256), lambda b, i: (b, 0, 0)),
                  pl.BlockSpec((1, NA_HEADS, tq, nk), bias_map)],
        out_specs=pl.BlockSpec((tq, 256), lambda b, i: (b * steps + i, 0)),
        out_shape=jax.ShapeDtypeStruct((nb * seq, 256), BF16),
        compiler_params=_cparams(("parallel", "arbitrary")),
        name="na_latent",
    )(z["q"], z["k"], z["v"], cache_k, cache_v, bias)


def _rope_tables(seq):
    t = np.arange(seq)
    pos = np.stack([t // GRID_W, t % GRID_W], axis=-1).astype(np.float32)
    n_freq = MLA_ROPE // 4
    inv = jnp.asarray(ROPE_BASE, F32) ** (-jnp.arange(n_freq, dtype=F32) / n_freq)
    ang = jnp.asarray(pos)[:, :, None] * inv
    cos, sin = jnp.cos(ang), jnp.sin(ang)
    c = jnp.stack([cos, cos], axis=2).reshape(seq, MLA_ROPE)
    s = jnp.stack([-sin, sin], axis=2).reshape(seq, MLA_ROPE)
    pad = ((0, 0), (0, 128 - MLA_ROPE))
    return jnp.pad(c, pad), jnp.pad(s, pad)


def _swap_rope_halves(w):
    w4 = w.reshape(w.shape[:-1] + (2, 2, MLA_ROPE // 4))
    return w4[..., ::-1, :].reshape(w.shape)


def _mla_prep_kernel(cq_ref, ckv_ref, krp_ref, krs_ref, cos_ref, sin_ref, qn_ref, wqn_ref, wqr_ref, wqs_ref,
                     kvn_ref, wkt_ref, qm_ref, kl_ref, klt_ref):
    tq = cq_ref.shape[0]
    cos = cos_ref[...]
    sin = sin_ref[...]
    ckv_n = _rms(ckv_ref[...], kvn_ref[...])
    kl_ref[0, :, 0:128] = ckv_n.astype(kl_ref.dtype)
    kl_ref[0, :, 128:256] = (krp_ref[...] * cos + krs_ref[...] * sin).astype(kl_ref.dtype)
    klt_ref[0] = ckv_n.T.astype(klt_ref.dtype)
    cqn = _rms(cq_ref[...], qn_ref[...]).astype(BF16)
    qn = _dot(cqn, wqn_ref[...]).astype(BF16)
    scale = (MLA_NOPE + MLA_ROPE) ** -0.5 * math.log2(math.e)
    for h in range(MLA_HEADS):
        qa = _dot(qn[:, h * MLA_NOPE:(h + 1) * MLA_NOPE], wkt_ref[h])
        qr = _dot(cqn, wqr_ref[h]) * cos + _dot(cqn, wqs_ref[h]) * sin
        qm_ref[0, 0, 0:128, h * tq:(h + 1) * tq] = (qa * scale).T.astype(qm_ref.dtype)
        qm_ref[0, 0, 128:256, h * tq:(h + 1) * tq] = (qr * scale).T.astype(qm_ref.dtype)


def _mla_prep(z, cos, sin, p, seq, blk0, nb):
    tq = TQ_MLA
    steps = seq // tq
    tok = lambda b, i: (blk0 * steps + b * steps + i, 0)
    pos = lambda b, i: (i, 0)
    const2 = lambda b, i: (0, 0)
    const3 = lambda b, i: (0, 0, 0)
    return pl.pallas_call(
        _mla_prep_kernel,
        grid=(nb, steps),
        in_specs=[pl.BlockSpec((tq, MLA_Q_LORA), tok), pl.BlockSpec((tq, MLA_KV_LORA), tok),
                  pl.BlockSpec((tq, 128), tok), pl.BlockSpec((tq, 128), tok),
                  pl.BlockSpec((tq, 128), pos), pl.BlockSpec((tq, 128), pos),
                  pl.BlockSpec((1, MLA_Q_LORA), const2),
                  pl.BlockSpec((MLA_Q_LORA, MLA_HEADS * MLA_NOPE), const2),
                  pl.BlockSpec((MLA_HEADS, MLA_Q_LORA, 128), const3),
                  pl.BlockSpec((MLA_HEADS, MLA_Q_LORA, 128), const3),
                  pl.BlockSpec((1, MLA_KV_LORA), const2),
                  pl.BlockSpec((MLA_HEADS, MLA_NOPE, MLA_KV_LORA), const3)],
        out_specs=[pl.BlockSpec((1, 1, MLA_KDIM, MLA_HEADS * tq), lambda b, i: (b, i, 0, 0)),
                   pl.BlockSpec((1, tq, MLA_KDIM), lambda b, i: (b, i, 0)),
                   pl.BlockSpec((1, MLA_KV_LORA, tq), lambda b, i: (b, 0, i))],
        out_shape=[jax.ShapeDtypeStruct((nb, steps, MLA_KDIM, MLA_HEADS * tq), BF16),
                   jax.ShapeDtypeStruct((nb, seq, MLA_KDIM), BF16),
                   jax.ShapeDtypeStruct((nb, MLA_KV_LORA, seq), BF16)],
        compiler_params=_cparams(("parallel", "parallel")),
        name="mla_prep",
    )(z["cq"], z["ckv"], z["krp"], z["krs"], cos, sin, p["mla_q_norm"], p["mla_wqn"], p["mla_wqr"],
      p["mla_wqs"], p["mla_kv_norm"], p["mla_wkt"])


def _mla_lat_kernel(qt_ref, kl_ref, klt_ref, kc_ref, kct_ref, wvt_ref, pd_ref, m_ref, l_ref, acc_ref,
                    s_buf, p_buf, a_buf):
    qt = qt_ref[0, 0]
    cols = qt.shape[1]
    tq = cols // MLA_HEADS
    n_chunks = kl_ref.shape[1] // TK_MLA
    m_ref[...] = jnp.full((1, cols), NEG, F32)
    l_ref[...] = jnp.zeros((1, cols), F32)
    acc_ref[...] = jnp.zeros((MLA_KV_LORA, cols), F32)

    def softmax_stats(s):
        m_old = m_ref[...]
        m_new = jnp.maximum(m_old, jnp.max(s, axis=0, keepdims=True))
        alpha = jnp.exp2(m_old - m_new)
        p = jnp.exp2(s - m_new)
        l_ref[...] = alpha * l_ref[...] + jnp.sum(p, axis=0, keepdims=True)
        m_ref[...] = m_new
        return alpha, p.astype(BF16)

    alpha, p = softmax_stats(_dot(kc_ref[0], qt))
    acc_ref[...] = alpha * acc_ref[...] + _dot(kct_ref[0], p)

    def chunk(c):
        return pl.ds(pl.multiple_of(c * TK_MLA, TK_MLA), TK_MLA)

    def scores(c, slot):
        s_buf[slot] = _dot(kl_ref[0, chunk(c), :], qt)

    def softmax(slot):
        alpha, p = softmax_stats(s_buf[slot])
        a_buf[slot] = alpha
        p_buf[slot] = p

    def values(c, slot):
        acc_ref[...] = a_buf[slot] * acc_ref[...] + _dot(klt_ref[0, :, chunk(c)], p_buf[slot])

    scores(0, 0)
    scores(1, 1)
    softmax(0)

    def body(j, carry):
        c = 2 * j
        scores(c + 2, 0)
        softmax(1)
        values(c, 0)
        scores(c + 3, 1)
        softmax(0)
        values(c + 1, 1)
        return carry

    lax.fori_loop(0, (n_chunks - 2) // 2, body, 0)
    softmax(1)
    values(n_chunks - 2, 0)
    values(n_chunks - 1, 1)
    o = (acc_ref[...] / l_ref[...]).astype(BF16)
    out_t = jnp.concatenate([_dot(wvt_ref[h], o[:, h * tq:(h + 1) * tq]) for h in range(MLA_HEADS)], axis=0)
    pd_ref[...] = out_t.T.astype(pd_ref.dtype)


def _mla_latent(qm, kl, klt, kc, kct, wvt, seq, nb):
    tq = TQ_MLA
    steps = seq // tq
    cols = MLA_HEADS * tq
    past = kc.shape[1]
    return pl.pallas_call(
        _mla_lat_kernel,
        grid=(nb, steps),
        in_specs=[pl.BlockSpec((1, 1, MLA_KDIM, cols), lambda b, i: (b, i, 0, 0)),
                  pl.BlockSpec((1, seq, MLA_KDIM), lambda b, i: (b, 0, 0)),
                  pl.BlockSpec((1, MLA_KV_LORA, seq), lambda b, i: (b, 0, 0)),
                  pl.BlockSpec((1, past, MLA_KDIM), lambda b, i: (b, 0, 0)),
                  pl.BlockSpec((1, MLA_KV_LORA, past), lambda b, i: (b, 0, 0)),
                  pl.BlockSpec((MLA_HEADS, MLA_VD, MLA_KV_LORA), lambda b, i: (0, 0, 0))],
        out_specs=pl.BlockSpec((tq, MLA_HEADS * MLA_VD), lambda b, i: (b * steps + i, 0)),
        out_shape=jax.ShapeDtypeStruct((nb * seq, MLA_HEADS * MLA_VD), BF16),
        scratch_shapes=[pltpu.VMEM((1, cols), F32), pltpu.VMEM((1, cols), F32),
                        pltpu.VMEM((MLA_KV_LORA, cols), F32),
                        pltpu.VMEM((2, TK_MLA, cols), F32), pltpu.VMEM((2, TK_MLA, cols), BF16),
                        pltpu.VMEM((2, 1, cols), F32)],
        compiler_params=_cparams(("parallel", "arbitrary")),
        name="mla_latent",
    )(qm, kl, klt, kc, kct, wvt)


def _merge_kernel(n_a, ha_ref, hb_ref, mod_ref, g_ref, pa_c, pa_l, pb_c, pb_l, pc_c, pc_l, pd_c, pd_l,
                  wg_ref, wb_ref, wo_ref, o_ref):
    mod = mod_ref[0]
    h = _pick(n_a, ha_ref, hb_ref)
    u = _norm_mod(h, g_ref[...], mod[:, D_MODEL:2 * D_MODEL], mod[:, 0:D_MODEL]).astype(BF16)
    y = None
    for j, (c_ref, l_ref) in enumerate(((pa_c, pa_l), (pb_c, pb_l), (pc_c, pc_l), (pd_c, pd_l))):
        gate = _sigmoid(_dot(u, wg_ref[:, j * D_MODEL:(j + 1) * D_MODEL]))
        term = gate * _dot(_pick(n_a, c_ref, l_ref), wb_ref[j])
        y = term if y is None else y + term
    out = _dot(y.astype(BF16), wo_ref[...])
    o_ref[...] = h + mod[:, 2 * D_MODEL:3 * D_MODEL] * out


def _merge(h, t, mods, g, branches, p, n_ctx_tiles, tiles_per_lat):
    d = D_MODEL
    tm = TM_TOKEN
    const2 = lambda i: (0, 0)
    branch_specs, branch_args = [], []
    for br in branches:
        branch_specs += br.specs(tm)
        branch_args += [br.a, br.b]
    return pl.pallas_call(
        functools.partial(_merge_kernel, h.n_a),
        grid=(t // tm,),
        in_specs=h.specs(tm) + [
            pl.BlockSpec((1, 1, mods.shape[-1]), _mod_row_map(n_ctx_tiles, tiles_per_lat)),
            pl.BlockSpec((1, d), const2)] + branch_specs + [
            pl.BlockSpec((d, N_BRANCH * d), const2),
            pl.BlockSpec((N_BRANCH, 256, d), lambda i: (0, 0, 0)),
            pl.BlockSpec((d, d), const2)],
        out_specs=pl.BlockSpec((tm, d), lambda i: (i, 0)),
        out_shape=jax.ShapeDtypeStruct((t, d), F32),
        compiler_params=_cparams(("parallel",)),
        name="merge",
    )(h.a, h.b, mods, g, *branch_args, p["w_gates"], p["w_branch_out"], p["w_out"])


def _router_gates(logits):
    lane_i = lax.broadcasted_iota(jnp.int32, logits.shape, 1)
    lane = lane_i.astype(F32)
    ninf = jnp.float32(-jnp.inf)

    def first_argmax(x):
        m = jnp.max(x, axis=-1, keepdims=True)
        idx = jnp.min(jnp.where(x == m, lane, jnp.float32(1e9)), axis=-1, keepdims=True)
        return m, idx

    gl = jnp.where(lane_i < MOE_GROUPS, logits, ninf)
    gmax, gsel = first_argmax(gl)
    gp = 1.0 / jnp.sum(jnp.exp(gl - gmax), axis=-1, keepdims=True)
    e_idx = lane_i - MOE_GROUPS
    e_group = lax.shift_right_arithmetic(e_idx, jnp.full_like(e_idx, 2)).astype(F32)
    in_group = (e_idx >= 0) & (e_idx < MOE_EXPERTS) & (e_group == gsel)
    el = jnp.where(in_group, logits, ninf)
    m1, i1 = first_argmax(el)
    m2, i2 = first_argmax(jnp.where(lane == i1, ninf, el))
    e2 = jnp.exp(m2 - m1)
    w1 = gp / (1.0 + e2)
    w2 = gp * e2 / (1.0 + e2)
    return jnp.where(lane == i1, w1, 0.0) + jnp.where(lane == i2, w2, 0.0)


def _moe_kernel(h_ref, mod_ref, g_ref, wr_ref, w13_ref, w2_ref, o_ref, u_ref, gate_ref, acc_ref):
    e = pl.program_id(1)
    mod = mod_ref[0]

    @pl.when(e == 0)
    def _():
        u = _norm_mod(h_ref[...], g_ref[...], mod[:, 4 * D_MODEL:5 * D_MODEL], mod[:, 3 * D_MODEL:4 * D_MODEL])
        ub = u.astype(BF16)
        u_ref[...] = ub
        gate_ref[...] = _router_gates(_dot(ub, wr_ref[...]))
        acc_ref[...] = jnp.zeros_like(acc_ref)

    ub = u_ref[...]
    lane = lax.broadcasted_iota(jnp.int32, gate_ref.shape, 1)
    gate = jnp.sum(jnp.where(lane == e + MOE_GROUPS, gate_ref[...], 0.0), axis=-1, keepdims=True)
    h13 = _dot(ub, w13_ref[0])
    hid = _silu(h13[:, 0:MOE_FF]) * h13[:, MOE_FF:2 * MOE_FF] * gate
    acc_ref[...] += _dot(hid.astype(BF16), w2_ref[0])

    @pl.when(e == MOE_EXPERTS - 1)
    def _():
        o_ref[...] = h_ref[...] + mod[:, 5 * D_MODEL:6 * D_MODEL] * acc_ref[...]


def _moe(h, mods, g, p, n_ctx_tiles, tiles_per_lat):
    t, d = h.shape
    tm = TM_MOE
    nct = n_ctx_tiles * TM_TOKEN // tm
    tpl = tiles_per_lat * TM_TOKEN // tm
    return pl.pallas_call(
        _moe_kernel,
        grid=(t // tm, MOE_EXPERTS),
        in_specs=[pl.BlockSpec((tm, d), lambda i, e: (i, 0)),
                  pl.BlockSpec((1, 1, mods.shape[-1]), _mod_row_map(nct, tpl)),
                  pl.BlockSpec((1, d), lambda i, e: (0, 0)),
                  pl.BlockSpec((d, ROUTER_LANES), lambda i, e: (0, 0)),
                  pl.BlockSpec((1, d, 2 * MOE_FF), lambda i, e: (e, 0, 0)),
                  pl.BlockSpec((1, MOE_FF, d), lambda i, e: (e, 0, 0))],
        out_specs=pl.BlockSpec((tm, d), lambda i, e: (i, 0)),
        out_shape=jax.ShapeDtypeStruct((t, d), F32),
        scratch_shapes=[pltpu.VMEM((tm, d), BF16), pltpu.VMEM((tm, ROUTER_LANES), F32),
                        pltpu.VMEM((tm, d), F32)],
        compiler_params=_cparams(("parallel", "arbitrary")),
        name="moe",
    )(h, mods, g, p["w_router"], p["w13"], p["w2"])


def _final_kernel(h_ref, g_ref, o_ref):
    o_ref[...] = _rms(h_ref[...], g_ref[...])


def _final_norm(h, g, blk0, rows):
    d = h.shape[1]
    tm = TM_TOKEN
    return pl.pallas_call(
        _final_kernel,
        grid=(rows // tm,),
        in_specs=[pl.BlockSpec((tm, d), lambda i: (blk0 + i, 0)), pl.BlockSpec((1, d), lambda i: (0, 0))],
        out_specs=pl.BlockSpec((tm, d), lambda i: (i, 0)),
        out_shape=jax.ShapeDtypeStruct((rows, d), F32),
        compiler_params=_cparams(("parallel",)),
        name="final_norm",
    )(h, g)


def _block_diag(w):
    nd, nb, bw, _ = w.shape
    eye = jnp.eye(nb, dtype=w.dtype)
    return jnp.einsum("dnij,nm->dnimj", w, eye).reshape(nd, nb * bw, nb * bw)


def _layer_params(l, w_in, lru_conv_w, lru_conv_b, lru_wa, lru_ba, lru_wx, lru_bx, lru_lam, w_lru_out, w_na_out,
                  cm_dw_w, cm_dw_b, cm_ln_g, cm_ln_b, w_cm_out, mla_q_norm, mla_wq_b, mla_kv_norm, mla_wkv_b,
                  w_mla_out, w_out, moe_w_group, moe_w_expert, moe_w1, moe_w3, moe_w2):
    wi = w_in[l]
    n_small = 2 * LRU_WIDTH + 3 * NA_HEADS * NA_HD + 2 * CM_WIDTH + MLA_Q_LORA + MLA_KV_LORA
    kr_cols = wi[:, n_small:n_small + MLA_ROPE]
    zpad = jnp.zeros((D_MODEL, 128 - MLA_ROPE), wi.dtype)
    w_small = jnp.concatenate([wi[:, :n_small], kr_cols, zpad, _swap_rope_halves(kr_cols), zpad], axis=1)
    w_gates = wi[:, n_small + MLA_ROPE:]

    qd = MLA_NOPE + MLA_ROPE
    wq = mla_wq_b[l].reshape(MLA_Q_LORA, MLA_HEADS, qd)
    wqn = wq[:, :, :MLA_NOPE].reshape(MLA_Q_LORA, MLA_HEADS * MLA_NOPE)
    wqr = jnp.moveaxis(wq[:, :, MLA_NOPE:], 1, 0)
    rpad = ((0, 0), (0, 0), (0, 128 - MLA_ROPE))
    wkv = mla_wkv_b[l].reshape(MLA_KV_LORA, MLA_HEADS, MLA_NOPE + MLA_VD)
    wkt = jnp.transpose(wkv[:, :, :MLA_NOPE], (1, 2, 0))
    wvt = jnp.transpose(wkv[:, :, MLA_NOPE:], (1, 2, 0))

    router = jnp.concatenate([moe_w_group[l], moe_w_expert[l]], axis=1)
    router = jnp.pad(router, ((0, 0), (0, ROUTER_LANES - router.shape[1])))
    row = lambda a: a.reshape(1, -1)
    return dict(
        w_small=w_small.astype(BF16), w_gates=w_gates.astype(BF16),
        lru_conv_w=lru_conv_w[l], lru_conv_b=row(lru_conv_b[l]),
        lru_wa=_block_diag(lru_wa[l]).astype(BF16), lru_ba=lru_ba[l][:, None, :],
        lru_wx=_block_diag(lru_wx[l]).astype(BF16), lru_bx=lru_bx[l][:, None, :],
        lru_lam=lru_lam[l][:, None, :],
        cm_dw_w=cm_dw_w[l], cm_dw_b=row(cm_dw_b[l]), cm_ln_g=row(cm_ln_g[l]), cm_ln_b=row(cm_ln_b[l]),
        mla_q_norm=row(mla_q_norm[l]), mla_kv_norm=row(mla_kv_norm[l]),
        mla_wq_b=mla_wq_b[l].astype(BF16), mla_wkv_b=mla_wkv_b[l].astype(BF16),
        mla_wqn=wqn.astype(BF16), mla_wqr=jnp.pad(wqr, rpad).astype(BF16),
        mla_wqs=jnp.pad(_swap_rope_halves(wqr), rpad).astype(BF16),
        mla_wkt=wkt.astype(BF16), mla_wvt=wvt.astype(BF16),
        w_branch_out=jnp.stack([w_lru_out[l], w_na_out[l], w_cm_out[l], w_mla_out[l]], axis=0).astype(BF16),
        w_out=w_out[l].astype(BF16),
        w_router=router.astype(BF16),
        w13=jnp.concatenate([moe_w1[l], moe_w3[l]], axis=-1).astype(BF16),
        w2=moe_w2[l].astype(BF16),
    )


def kernel(x_prompt, x_sample, cache_na_k, cache_na_v, cache_mla_ckv, cache_mla_krope, state_lru, c, c_ctx,
           w_ada, b_ada, norm1_g, w_in, lru_conv_w, lru_conv_b, lru_wa, lru_ba, lru_wx, lru_bx, lru_lam,
           w_lru_out, na_rpb, w_na_out, cm_dw_w, cm_dw_b, cm_ln_g, cm_ln_b, w_cm_out, mla_q_norm, mla_wq_b,
           mla_kv_norm, mla_wkv_b, w_mla_out, w_out, norm2_g, moe_w_group, moe_w_expert, moe_w1, moe_w3,
           moe_w2, final_g):
    nb_c, seq_c, d = x_prompt.shape
    nb_l, seq_l, _ = x_sample.shape
    depth = w_in.shape[0]
    past = cache_na_k.shape[2]
    t_ctx = nb_c * seq_c
    t_lat = nb_l * seq_l
    assert d == D_MODEL and seq_l % (GRID_W * NA_QROWS) == 0 and seq_l % seq_c == 0
    assert t_ctx % TM_MOE == 0 and seq_l % TM_MOE == 0 and seq_c % SEQ_CHUNK == 0 and seq_l % TK_MLA == 0
    n_ctx_tiles = t_ctx // TM_TOKEN
    tiles_per_lat = seq_l // TM_TOKEN
    lat_blk0 = t_ctx // seq_l
    assert lat_blk0 * seq_l == t_ctx and (seq_l // TK_MLA) % 2 == 0

    t_all = t_ctx + t_lat
    h = _TwoSource(x_prompt.reshape(t_ctx, d), x_sample.reshape(t_lat, d), n_ctx_tiles, 0)
    n_cond = 1 + nb_l
    cvec = jnp.concatenate([c_ctx[None, :], c, jnp.zeros((-n_cond % 8, d), F32)], axis=0)
    mods = _modulation(cvec, w_ada, b_ada)
    cos, sin = _rope_tables(seq_l)
    zero_state = jnp.zeros((nb_c, 2, LRU_WIDTH), F32)

    st_k, st_v, st_ckv, st_kr, st_lru = [], [], [], [], []
    for l in range(depth):
        p = _layer_params(l, w_in, lru_conv_w, lru_conv_b, lru_wa, lru_ba, lru_wx, lru_bx, lru_lam, w_lru_out,
                          w_na_out, cm_dw_w, cm_dw_b, cm_ln_g, cm_ln_b, w_cm_out, mla_q_norm, mla_wq_b,
                          mla_kv_norm, mla_wkv_b, w_mla_out, w_out, moe_w_group, moe_w_expert, moe_w1, moe_w3,
                          moe_w2)
        mod_l = mods[l].reshape(mods.shape[1], 1, mods.shape[2])
        g1 = norm1_g[l].reshape(1, d)
        g2 = norm2_g[l].reshape(1, d)
        z = _in_proj(h, t_all, mod_l, g1, p["w_small"], n_ctx_tiles, tiles_per_lat)

        pa_c, lru_c = _lru_branch(z["lx"], z["ly"], zero_state, p, seq_c, 0, nb_c)
        pc_c = _conformer_branch(z["ga"], z["gg"], p, seq_c, 0, nb_c)
        pb_c, pd_c, ckvn_c = _ctx_attention(z, p, seq_c, nb_c)

        pa_l, _ = _lru_branch(z["lx"], z["ly"], state_lru[:, l], p, seq_l, lat_blk0, nb_l)
        pc_l = _conformer_branch(z["ga"], z["gg"], p, seq_l, lat_blk0, nb_l)
        bias = _na_bias_tables(na_rpb[l], seq_l // GRID_W)
        pb_l = _na_latent(z, cache_na_k[:, l].reshape(nb_l, past, NA_HEADS * NA_HD),
                          cache_na_v[:, l].reshape(nb_l, past, NA_HEADS * NA_HD), bias, seq_l, lat_blk0, nb_l)
        qm, kl, klt = _mla_prep(z, cos, sin, p, seq_l, lat_blk0, nb_l)
        kc = jnp.concatenate([cache_mla_ckv[:, l], cache_mla_krope[:, l],
                              jnp.zeros((nb_l, past, MLA_KDIM - MLA_KV_LORA - MLA_ROPE), F32)], axis=-1).astype(BF16)
        kct = jnp.swapaxes(cache_mla_ckv[:, l], 1, 2).astype(BF16)
        pd_l = _mla_latent(qm, kl, klt, kc, kct, p["mla_wvt"], seq_l, nb_l)

        branches = [_TwoSource(br_c, br_l, n_ctx_tiles, 0)
                    for br_c, br_l in ((pa_c, pa_l), (pb_c, pb_l), (pc_c, pc_l), (pd_c, pd_l))]
        h_mid = _merge(h, t_all, mod_l, g1, branches, p, n_ctx_tiles, tiles_per_lat)
        h_new = _moe(h_mid, mod_l, g2, p, n_ctx_tiles, tiles_per_lat)
        h = _TwoSource(h_new, h_new, n_ctx_tiles, n_ctx_tiles)

        st_k.append(z["k"][:t_ctx].reshape(nb_c, seq_c, NA_HEADS, NA_HD))
        st_v.append(z["v"][:t_ctx].reshape(nb_c, seq_c, NA_HEADS, NA_HD))
        st_ckv.append(ckvn_c.reshape(nb_c, seq_c, MLA_KV_LORA))
        st_kr.append(z["krp"][:t_ctx, :MLA_ROPE].reshape(nb_c, seq_c, MLA_ROPE))
        st_lru.append(lru_c)

    fg = final_g.reshape(1, d)
    y_prompt = _final_norm(h_new, fg, 0, t_ctx).reshape(nb_c, seq_c, d)
    y_sample = _final_norm(h_new, fg, n_ctx_tiles, t_lat).reshape(nb_l, seq_l, d)
    return (y_prompt, y_sample, jnp.stack(st_k, axis=1), jnp.stack(st_v, axis=1), jnp.stack(st_ckv, axis=1),
            jnp.stack(st_kr, axis=1), jnp.stack(st_lru, axis=1))
```

```python
import functools
import math

import numpy as np
import jax
import jax.numpy as jnp
from jax import lax
from jax.experimental import pallas as pl
from jax.experimental.pallas import tpu as pltpu

F32 = jnp.float32
BF16 = jnp.bfloat16

D_MODEL = 1024
GRID_W = 64
EPS = 1e-6
NEG = -1e30
N_BRANCH = 4
LRU_WIDTH = 256
LRU_BLOCKS = 4
LRU_CONV = 4
LRU_C = 8.0
NA_HEADS = 4
NA_HD = 64
NA_WR = 8
NA_WC = 16
NA_QROWS = 4
NA_KROWS = NA_QROWS + NA_WR
CM_WIDTH = 256
CM_KERNEL = 31
MLA_HEADS = 4
MLA_Q_LORA = 256
MLA_KV_LORA = 128
MLA_NOPE = 64
MLA_ROPE = 32
MLA_VD = 64
MLA_KDIM = 256
ROPE_BASE = 10000.0
MOE_GROUPS = 4
MOE_PER_GROUP = 4
MOE_EXPERTS = 16
MOE_FF = 256
ROUTER_LANES = 128

TM_TOKEN = 512
TM_MOE = 1024
TQ_MLA = 256
TK_MLA = 512
SEQ_CHUNK = 256
VMEM_LIMIT = 56 * 1024 * 1024


def _cparams(sem, vmem=VMEM_LIMIT):
    return pltpu.CompilerParams(dimension_semantics=sem, vmem_limit_bytes=vmem)


def _sigmoid(x):
    return 0.5 * jnp.tanh(0.5 * x) + 0.5


def _silu(x):
    return x * _sigmoid(x)


def _gelu_tanh(x):
    c = math.sqrt(2.0 / math.pi)
    return x * (0.5 * (1.0 + jnp.tanh(c * (x + 0.044715 * (x * x * x)))))


def _softplus(x):
    return jnp.maximum(x, 0.0) + jnp.log1p(jnp.exp(-jnp.abs(x)))


def _rms(x, g):
    return x * lax.rsqrt(jnp.mean(x * x, axis=-1, keepdims=True) + EPS) * g


def _norm_mod(h, g, scale, shift):
    return _rms(h, g) * (1.0 + scale) + shift


def _dot(a, b):
    return jnp.dot(a, b, preferred_element_type=F32)


def _dot_nt(a, b):
    return lax.dot_general(a, b, (((1,), (1,)), ((), ())), preferred_element_type=F32)


def _mod_kernel(c_ref, w_ref, b_ref, o_ref):
    s = _silu(c_ref[...])
    o_ref[0] = jnp.dot(s, w_ref[0], preferred_element_type=F32, precision=lax.Precision.HIGHEST) + b_ref[0]


def _modulation(cvec, w_ada, b_ada):
    depth, d, n = w_ada.shape
    rows = cvec.shape[0]
    tn = 1024
    return pl.pallas_call(
        _mod_kernel,
        grid=(depth, n // tn),
        in_specs=[pl.BlockSpec((rows, d), lambda l, j: (0, 0)),
                  pl.BlockSpec((1, d, tn), lambda l, j: (l, 0, j)),
                  pl.BlockSpec((1, 1, tn), lambda l, j: (l, 0, j))],
        out_specs=pl.BlockSpec((1, rows, tn), lambda l, j: (l, 0, j)),
        out_shape=jax.ShapeDtypeStruct((depth, rows, n), F32),
        compiler_params=_cparams(("parallel", "parallel")),
        name="modulation",
    )(cvec, w_ada, b_ada.reshape(depth, 1, n))


def _mod_row_map(n_ctx_tiles, tiles_per_lat):
    def index_map(i, *_):
        row = jnp.maximum(i - n_ctx_tiles, 0) // tiles_per_lat + (i >= n_ctx_tiles).astype(jnp.int32)
        return (row, 0, 0)
    return index_map


_IN_SEGS = (("lx", 256), ("ly", 256), ("q", 256), ("k", 256), ("v", 256), ("ga", 256), ("gg", 256),
            ("cq", 256), ("ckv", 128), ("krp", 128), ("krs", 128))
_IN_DTYPES = {"q": BF16}


class _TwoSource:
    def __init__(self, a, b, n_a, b_tile0):
        self.a, self.b, self.n_a, self.b_tile0 = a, b, n_a, b_tile0

    def specs(self, tm):
        n_a, b0 = self.n_a, self.b_tile0
        width = self.a.shape[1]
        return [pl.BlockSpec((tm, width), lambda i: (jnp.minimum(i, n_a - 1), 0)),
                pl.BlockSpec((tm, width), lambda i: (jnp.maximum(i - n_a, 0) + b0, 0))]


def _pick(n_a, a_ref, b_ref):
    return jnp.where(pl.program_id(0) < n_a, a_ref[...], b_ref[...])


def _in_kernel(n_a, ha_ref, hb_ref, mod_ref, g_ref, w_ref, *out_refs):
    mod = mod_ref[0]
    h = _pick(n_a, ha_ref, hb_ref)
    u = _norm_mod(h, g_ref[...], mod[:, D_MODEL:2 * D_MODEL], mod[:, 0:D_MODEL]).astype(BF16)
    off = 0
    for (_, width), o_ref in zip(_IN_SEGS, out_refs):
        o_ref[...] = _dot(u, w_ref[:, off:off + width]).astype(o_ref.dtype)
        off += width


def _in_proj(h, t, mods, g, w_small, n_ctx_tiles, tiles_per_lat):
    d = D_MODEL
    tm = TM_TOKEN
    n = w_small.shape[1]
    out_shape = [jax.ShapeDtypeStruct((t, width), _IN_DTYPES.get(name, F32)) for name, width in _IN_SEGS]
    out_specs = [pl.BlockSpec((tm, width), lambda i: (i, 0)) for _, width in _IN_SEGS]
    outs = pl.pallas_call(
        functools.partial(_in_kernel, h.n_a),
        grid=(t // tm,),
        in_specs=h.specs(tm) + [
            pl.BlockSpec((1, 1, mods.shape[-1]), _mod_row_map(n_ctx_tiles, tiles_per_lat)),
            pl.BlockSpec((1, d), lambda i: (0, 0)),
            pl.BlockSpec((d, n), lambda i: (0, 0))],
        out_specs=out_specs,
        out_shape=out_shape,
        compiler_params=_cparams(("parallel",)),
        name="in_proj",
    )(h.a, h.b, mods, g, w_small)
    return dict(zip([s for s, _ in _IN_SEGS], outs))


def _shifted_windows(xw, offsets, length):
    n = xw.shape[0]
    rolled = {0: xw}
    out = {}
    for o in offsets:
        r = o % 8
        if r not in rolled:
            rolled[r] = pltpu.roll(xw, n - r, 0)
        base = o - r
        out[o] = rolled[r][base:base + length]
    return out


def _lru_kernel(lx_ref, ly_ref, h0_ref, cw_ref, cb_ref, wa_ref, ba_ref, wx_ref, bx_ref, lam_ref,
                pa_ref, st_ref, pad_ref, af_ref, bf_ref, ab_ref, bb_ref):
    t = lx_ref.shape[0]
    ch = min(SEQ_CHUNK, t)
    halo = 8
    zeros = jnp.zeros((halo, LRU_WIDTH), F32)
    pad_ref[0:halo, :] = zeros
    pad_ref[halo + t:2 * halo + t, :] = zeros
    pad_ref[halo:halo + t, :] = lx_ref[...]
    pad_l = LRU_CONV // 2
    taps = [halo - pad_l + k for k in range(LRU_CONV)]
    a_refs = (af_ref, ab_ref)
    b_refs = (bf_ref, bb_ref)

    def gates(c, carry):
        r0 = pl.multiple_of(c * ch, ch)
        xw = pad_ref[pl.ds(r0, ch + 2 * halo), :]
        win = _shifted_windows(xw, taps, ch)
        xc = cb_ref[...] + sum(cw_ref[k:k + 1, :] * win[taps[k]] for k in range(LRU_CONV))
        xcb = xc.astype(BF16)
        for d in range(2):
            r = _sigmoid(_dot(xcb, wa_ref[d]) + ba_ref[d])
            i = _sigmoid(_dot(xcb, wx_ref[d]) + bx_ref[d])
            log_a = (-LRU_C) * r * _softplus(-lam_ref[d])
            a = jnp.exp(log_a)
            one_minus_a2 = -jnp.tanh(log_a) * (a * a + 1.0)
            a_refs[d][pl.ds(r0, ch), :] = a
            b_refs[d][pl.ds(r0, ch), :] = jnp.sqrt(one_minus_a2) * (i * xc)
        return carry

    lax.fori_loop(0, t // ch, gates, 0)

    sub = 8
    row = lax.broadcasted_iota(jnp.int32, (sub, LRU_WIDTH), 0)

    def bcast(x, j):
        return jnp.broadcast_to(x[j:j + 1, :], (sub, LRU_WIDTH))

    def scan(c, carry):
        hf, hb = carry
        rf = pl.multiple_of(c * sub, sub)
        rb = pl.multiple_of(t - sub - c * sub, sub)
        a_f, b_f = af_ref[pl.ds(rf, sub), :], bf_ref[pl.ds(rf, sub), :]
        a_b, b_b = ab_ref[pl.ds(rb, sub), :], bb_ref[pl.ds(rb, sub), :]
        out_f = jnp.zeros((sub, LRU_WIDTH), F32)
        out_b = jnp.zeros((sub, LRU_WIDTH), F32)
        for j in range(sub):
            hf = bcast(a_f, j) * hf + bcast(b_f, j)
            out_f = jnp.where(row == j, hf, out_f)
            jb = sub - 1 - j
            hb = bcast(a_b, jb) * hb + bcast(b_b, jb)
            out_b = jnp.where(row == jb, hb, out_b)
        bf_ref[pl.ds(rf, sub), :] = out_f
        bb_ref[pl.ds(rb, sub), :] = out_b
        return hf, hb

    h0 = (jnp.broadcast_to(h0_ref[0, 0:1, :], (sub, LRU_WIDTH)), jnp.broadcast_to(h0_ref[0, 1:2, :], (sub, LRU_WIDTH)))
    hf, hb = lax.fori_loop(0, t // sub, scan, h0)
    st_ref[0, 0:1, :] = hf[0:1, :]
    st_ref[0, 1:2, :] = hb[0:1, :]

    def emit(c, carry):
        r0 = pl.multiple_of(c * ch, ch)
        y = (bf_ref[pl.ds(r0, ch), :] + bb_ref[pl.ds(r0, ch), :]) * _gelu_tanh(ly_ref[pl.ds(r0, ch), :])
        pa_ref[pl.ds(r0, ch), :] = y.astype(pa_ref.dtype)
        return carry

    lax.fori_loop(0, t // ch, emit, 0)


def _lru_branch(lx, ly, h0, p, seq, blk0, nb):
    w = LRU_WIDTH
    const2 = lambda b: (0, 0)
    const3 = lambda b: (0, 0, 0)
    return pl.pallas_call(
        _lru_kernel,
        grid=(nb,),
        in_specs=[pl.BlockSpec((seq, w), lambda b: (blk0 + b, 0)),
                  pl.BlockSpec((seq, w), lambda b: (blk0 + b, 0)),
                  pl.BlockSpec((1, 2, w), lambda b: (b, 0, 0)),
                  pl.BlockSpec((LRU_CONV, w), const2),
                  pl.BlockSpec((1, w), const2),
                  pl.BlockSpec((2, w, w), const3),
                  pl.BlockSpec((2, 1, w), const3),
                  pl.BlockSpec((2, w, w), const3),
                  pl.BlockSpec((2, 1, w), const3),
                  pl.BlockSpec((2, 1, w), const3)],
        out_specs=[pl.BlockSpec((seq, w), lambda b: (b, 0)),
                   pl.BlockSpec((1, 2, w), lambda b: (b, 0, 0))],
        out_shape=[jax.ShapeDtypeStruct((nb * seq, w), BF16),
                   jax.ShapeDtypeStruct((nb, 2, w), F32)],
        scratch_shapes=[pltpu.VMEM((seq + 16, w), F32)] + [pltpu.VMEM((seq, w), F32)] * 4,
        compiler_params=_cparams(("parallel",)),
        name="rglru",
    )(lx, ly, h0, p["lru_conv_w"], p["lru_conv_b"], p["lru_wa"], p["lru_ba"], p["lru_wx"], p["lru_bx"],
      p["lru_lam"])


def _cm_kernel(ga_ref, gg_ref, w_ref, b_ref, lg_ref, lb_ref, pc_ref, pad_ref):
    t = ga_ref.shape[0]
    ch = min(SEQ_CHUNK, t)
    halo = 16
    zeros = jnp.zeros((halo, CM_WIDTH), F32)
    pad_ref[0:halo, :] = zeros
    pad_ref[halo + t:2 * halo + t, :] = zeros

    def glu(c, carry):
        r0 = pl.multiple_of(c * ch, ch)
        pad_ref[pl.ds(halo + r0, ch), :] = ga_ref[pl.ds(r0, ch), :] * _sigmoid(gg_ref[pl.ds(r0, ch), :])
        return carry

    lax.fori_loop(0, t // ch, glu, 0)
    taps = [halo - CM_KERNEL // 2 + k for k in range(CM_KERNEL)]

    def conv(c, carry):
        r0 = pl.multiple_of(c * ch, ch)
        xw = pad_ref[pl.ds(r0, ch + 2 * halo), :]
        win = _shifted_windows(xw, taps, ch)
        z = b_ref[...] + sum(w_ref[k:k + 1, :] * win[taps[k]] for k in range(CM_KERNEL))
        mu = jnp.mean(z, axis=-1, keepdims=True)
        zc = z - mu
        var = jnp.mean(zc * zc, axis=-1, keepdims=True)
        y = zc * lax.rsqrt(var + EPS) * lg_ref[...] + lb_ref[...]
        pc_ref[pl.ds(r0, ch), :] = _silu(y).astype(pc_ref.dtype)
        return carry

    lax.fori_loop(0, t // ch, conv, 0)


def _conformer_branch(ga, gg, p, seq, blk0, nb):
    w = CM_WIDTH
    const2 = lambda b: (0, 0)
    return pl.pallas_call(
        _cm_kernel,
        grid=(nb,),
        in_specs=[pl.BlockSpec((seq, w), lambda b: (blk0 + b, 0)),
                  pl.BlockSpec((seq, w), lambda b: (blk0 + b, 0)),
                  pl.BlockSpec((CM_KERNEL, w), const2),
                  pl.BlockSpec((1, w), const2),
                  pl.BlockSpec((1, w), const2),
                  pl.BlockSpec((1, w), const2)],
        out_specs=pl.BlockSpec((seq, w), lambda b: (b, 0)),
        out_shape=jax.ShapeDtypeStruct((nb * seq, w), BF16),
        scratch_shapes=[pltpu.VMEM((seq + 32, w), F32)],
        compiler_params=_cparams(("parallel",)),
        name="conformer",
    )(ga, gg, p["cm_dw_w"], p["cm_dw_b"], p["cm_ln_g"], p["cm_ln_b"])


def _softmax_pv(scores, values):
    m = functools.reduce(jnp.maximum, [jnp.max(s, axis=-1, keepdims=True) for s in scores])
    ps = [jnp.exp(s - m) for s in scores]
    l = sum(jnp.sum(p, axis=-1, keepdims=True) for p in ps)
    o = sum(_dot(p.astype(BF16), v) for p, v in zip(ps, values))
    return o / l


def _ctx_attn_kernel(q_ref, k_ref, v_ref, cq_ref, ckv_ref, krp_ref, qn_ref, wq_ref, kvn_ref, wkv_ref,
                     pb_ref, pd_ref, ckvn_ref):
    q = q_ref[...]
    k = k_ref[...].astype(BF16)
    v = v_ref[...].astype(BF16)
    na_scale = NA_HD ** -0.5
    for h in range(NA_HEADS):
        sl = slice(h * NA_HD, (h + 1) * NA_HD)
        s = _dot_nt(q[:, sl], k[:, sl]) * na_scale
        pb_ref[:, sl] = _softmax_pv([s], [v[:, sl]]).astype(pb_ref.dtype)

    ckv_n = _rms(ckv_ref[...], kvn_ref[...])
    ckvn_ref[...] = ckv_n
    mla_scale = (MLA_NOPE + MLA_ROPE) ** -0.5
    qf = _dot(_rms(cq_ref[...], qn_ref[...]).astype(BF16), wq_ref[...]).astype(BF16)
    kv = _dot(ckv_n.astype(BF16), wkv_ref[...]).astype(BF16)
    kr = krp_ref[:, 0:MLA_ROPE].astype(BF16)
    qd = MLA_NOPE + MLA_ROPE
    kd = MLA_NOPE + MLA_VD
    for h in range(MLA_HEADS):
        qn = qf[:, h * qd:h * qd + MLA_NOPE]
        qr = qf[:, h * qd + MLA_NOPE:(h + 1) * qd]
        kn = kv[:, h * kd:h * kd + MLA_NOPE]
        vm = kv[:, h * kd + MLA_NOPE:(h + 1) * kd]
        s = (_dot_nt(qn, kn) + _dot_nt(qr, kr)) * mla_scale
        pd_ref[:, h * MLA_VD:(h + 1) * MLA_VD] = _softmax_pv([s], [vm]).astype(pd_ref.dtype)


def _ctx_attention(z, p, seq, nb):
    const2 = lambda b: (0, 0)
    row = lambda b: (b, 0)
    return pl.pallas_call(
        _ctx_attn_kernel,
        grid=(nb,),
        in_specs=[pl.BlockSpec((seq, 256), row), pl.BlockSpec((seq, 256), row), pl.BlockSpec((seq, 256), row),
                  pl.BlockSpec((seq, MLA_Q_LORA), row), pl.BlockSpec((seq, MLA_KV_LORA), row),
                  pl.BlockSpec((seq, 128), row),
                  pl.BlockSpec((1, MLA_Q_LORA), const2),
                  pl.BlockSpec(p["mla_wq_b"].shape, const2),
                  pl.BlockSpec((1, MLA_KV_LORA), const2),
                  pl.BlockSpec(p["mla_wkv_b"].shape, const2)],
        out_specs=[pl.BlockSpec((seq, 256), row), pl.BlockSpec((seq, 256), row),
                   pl.BlockSpec((seq, MLA_KV_LORA), row)],
        out_shape=[jax.ShapeDtypeStruct((nb * seq, 256), BF16),
                   jax.ShapeDtypeStruct((nb * seq, 256), BF16),
                   jax.ShapeDtypeStruct((nb * seq, MLA_KV_LORA), F32)],
        compiler_params=_cparams(("parallel",)),
        name="ctx_attention",
    )(z["q"], z["k"], z["v"], z["cq"], z["ckv"], z["krp"], p["mla_q_norm"], p["mla_wq_b"],
      p["mla_kv_norm"], p["mla_wkv_b"])


def _na_bias_tables(rpb, rows):
    heads, n_rel_r, n_rel_c = rpb.shape
    w = GRID_W
    u = jnp.concatenate([rpb[:, :, NA_WC - 1:], jnp.zeros((heads, n_rel_r, 2 * w - n_rel_c), rpb.dtype),
                         rpb[:, :, :NA_WC - 1]], axis=-1)
    toep = jnp.tile(u, (1, 1, w))[:, :, :w * (2 * w - 1)].reshape(heads, n_rel_r, w, 2 * w - 1)[..., :w]
    qc = np.arange(w)
    cs = np.clip(qc - NA_WC // 2, 0, w - NA_WC)
    col_ok = (qc[None, :] >= cs[:, None]) & (qc[None, :] < cs[:, None] + NA_WC)
    toep = jnp.where(col_ok, toep, NEG)
    masked = jnp.full((heads, w, w), NEG, rpb.dtype)
    n_steps = rows // NA_QROWS
    tables = []
    for step in (0, 1, n_steps - 1):
        start = int(np.clip(NA_QROWS * step - NA_WR // 2, 0, rows - NA_KROWS))
        block_rows = []
        for a in range(NA_QROWS):
            qr = NA_QROWS * step + a
            rs = int(np.clip(qr - NA_WR // 2, 0, rows - NA_WR))
            blocks = []
            for b in range(NA_KROWS):
                kr = start + b
                blocks.append(toep[:, kr - qr + NA_WR - 1] if rs <= kr < rs + NA_WR else masked)
            block_rows.append(jnp.concatenate(blocks, axis=-1))
        tables.append(jnp.concatenate(block_rows, axis=1))
    return jnp.stack(tables, axis=0)


def _na_lat_kernel(q_ref, k_ref, v_ref, ck_ref, cv_ref, bias_ref, pb_ref):
    i = pl.program_id(1)
    rows = k_ref.shape[0] // GRID_W
    start = jnp.clip(NA_QROWS * i - NA_WR // 2, 0, rows - NA_KROWS) * GRID_W
    start = pl.multiple_of(start, GRID_W)
    nk = NA_KROWS * GRID_W
    q = q_ref[...]
    kw = k_ref[pl.ds(start, nk), :].astype(BF16)
    vw = v_ref[pl.ds(start, nk), :].astype(BF16)
    ck = ck_ref[0].astype(BF16)
    cv = cv_ref[0].astype(BF16)
    scale = NA_HD ** -0.5
    for h in range(NA_HEADS):
        sl = slice(h * NA_HD, (h + 1) * NA_HD)
        s_loc = _dot_nt(q[:, sl], kw[:, sl]) * scale + bias_ref[0, h]
        s_ctx = _dot_nt(q[:, sl], ck[:, sl]) * scale
        pb_ref[:, sl] = _softmax_pv([s_loc, s_ctx], [vw[:, sl], cv[:, sl]]).astype(pb_ref.dtype)


def _na_latent(z, cache_k, cache_v, bias, seq, blk0, nb):
    tq = NA_QROWS * GRID_W
    steps = seq // tq
    nk = NA_KROWS * GRID_W
    past = cache_k.shape[1]

    def bias_map(b, i):
        return (jnp.where(i == 0, 0, jnp.where(i == steps - 1, 2, 1)), 0, 0, 0)

    return pl.pallas_call(
        _na_lat_kernel,
        grid=(nb, steps),
        in_specs=[pl.BlockSpec((tq, 256), lambda b, i: (blk0 * steps + b * steps + i, 0)),
                  pl.BlockSpec((seq, 256), lambda b, i: (blk0 + b, 0)),
                  pl.BlockSpec((seq, 256), lambda b, i: (blk0 + b, 0)),
                  pl.BlockSpec((1, past, 256), lambda b, i: (b, 0, 0)),
                  pl.BlockSpec((1, past, 256), lambda b, i: (b, 0, 0)),
                  pl.BlockSpec((1, NA_HEADS, tq, nk), bias_map)],
        out_specs=pl.BlockSpec((tq, 256), lambda b, i: (b * steps + i, 0)),
        out_shape=jax.ShapeDtypeStruct((nb * seq, 256), BF16),
        compiler_params=_cparams(("parallel", "arbitrary")),
        name="na_latent",
    )(z["q"], z["k"], z["v"], cache_k, cache_v, bias)


def _rope_tables(seq):
    t = np.arange(seq)
    pos = np.stack([t // GRID_W, t % GRID_W], axis=-1).astype(np.float32)
    n_freq = MLA_ROPE // 4
    inv = jnp.asarray(ROPE_BASE, F32) ** (-jnp.arange(n_freq, dtype=F32) / n_freq)
    ang = jnp.asarray(pos)[:, :, None] * inv
    cos, sin = jnp.cos(ang), jnp.sin(ang)
    c = jnp.stack([cos, cos], axis=2).reshape(seq, MLA_ROPE)
    s = jnp.stack([-sin, sin], axis=2).reshape(seq, MLA_ROPE)
    pad = ((0, 0), (0, 128 - MLA_ROPE))
    return jnp.pad(c, pad), jnp.pad(s, pad)


def _swap_rope_halves(w):
    w4 = w.reshape(w.shape[:-1] + (2, 2, MLA_ROPE // 4))
    return w4[..., ::-1, :].reshape(w.shape)


def _mla_prep_kernel(cq_ref, ckv_ref, krp_ref, krs_ref, cos_ref, sin_ref, qn_ref, wqn_ref, wqr_ref, wqs_ref,
                     kvn_ref, wkt_ref, qm_ref, kl_ref, klt_ref):
    tq = cq_ref.shape[0]
    cos = cos_ref[...]
    sin = sin_ref[...]
    ckv_n = _rms(ckv_ref[...], kvn_ref[...])
    kl_ref[0, :, 0:128] = ckv_n.astype(kl_ref.dtype)
    kl_ref[0, :, 128:256] = (krp_ref[...] * cos + krs_ref[...] * sin).astype(kl_ref.dtype)
    klt_ref[0] = ckv_n.T.astype(klt_ref.dtype)
    cqn = _rms(cq_ref[...], qn_ref[...]).astype(BF16)
    qn = _dot(cqn, wqn_ref[...]).astype(BF16)
    scale = (MLA_NOPE + MLA_ROPE) ** -0.5 * math.log2(math.e)
    for h in range(MLA_HEADS):
        qa = _dot(qn[:, h * MLA_NOPE:(h + 1) * MLA_NOPE], wkt_ref[h])
        qr = _dot(cqn, wqr_ref[h]) * cos + _dot(cqn, wqs_ref[h]) * sin
        qm_ref[0, 0, 0:128, h * tq:(h + 1) * tq] = (qa * scale).T.astype(qm_ref.dtype)
        qm_ref[0, 0, 128:256, h * tq:(h + 1) * tq] = (qr * scale).T.astype(qm_ref.dtype)


def _mla_prep(z, cos, sin, p, seq, blk0, nb):
    tq = TQ_MLA
    steps = seq // tq
    tok = lambda b, i: (blk0 * steps + b * steps + i, 0)
    pos = lambda b, i: (i, 0)
    const2 = lambda b, i: (0, 0)
    const3 = lambda b, i: (0, 0, 0)
    return pl.pallas_call(
        _mla_prep_kernel,
        grid=(nb, steps),
        in_specs=[pl.BlockSpec((tq, MLA_Q_LORA), tok), pl.BlockSpec((tq, MLA_KV_LORA), tok),
                  pl.BlockSpec((tq, 128), tok), pl.BlockSpec((tq, 128), tok),
                  pl.BlockSpec((tq, 128), pos), pl.BlockSpec((tq, 128), pos),
                  pl.BlockSpec((1, MLA_Q_LORA), const2),
                  pl.BlockSpec((MLA_Q_LORA, MLA_HEADS * MLA_NOPE), const2),
                  pl.BlockSpec((MLA_HEADS, MLA_Q_LORA, 128), const3),
                  pl.BlockSpec((MLA_HEADS, MLA_Q_LORA, 128), const3),
                  pl.BlockSpec((1, MLA_KV_LORA), const2),
                  pl.BlockSpec((MLA_HEADS, MLA_NOPE, MLA_KV_LORA), const3)],
        out_specs=[pl.BlockSpec((1, 1, MLA_KDIM, MLA_HEADS * tq), lambda b, i: (b, i, 0, 0)),
                   pl.BlockSpec((1, tq, MLA_KDIM), lambda b, i: (b, i, 0)),
                   pl.BlockSpec((1, MLA_KV_LORA, tq), lambda b, i: (b, 0, i))],
        out_shape=[jax.ShapeDtypeStruct((nb, steps, MLA_KDIM, MLA_HEADS * tq), BF16),
                   jax.ShapeDtypeStruct((nb, seq, MLA_KDIM), BF16),
                   jax.ShapeDtypeStruct((nb, MLA_KV_LORA, seq), BF16)],
        compiler_params=_cparams(("parallel", "parallel")),
        name="mla_prep",
    )(z["cq"], z["ckv"], z["krp"], z["krs"], cos, sin, p["mla_q_norm"], p["mla_wqn"], p["mla_wqr"],
      p["mla_wqs"], p["mla_kv_norm"], p["mla_wkt"])


def _mla_lat_kernel(qt_ref, kl_ref, klt_ref, kc_ref, kct_ref, wvt_ref, pd_ref, m_ref, l_ref, acc_ref,
                    s_buf, p_buf, a_buf):
    qt = qt_ref[0, 0]
    cols = qt.shape[1]
    tq = cols // MLA_HEADS
    n_chunks = kl_ref.shape[1] // TK_MLA
    m_ref[...] = jnp.full((1, cols), NEG, F32)
    l_ref[...] = jnp.zeros((1, cols), F32)
    acc_ref[...] = jnp.zeros((MLA_KV_LORA, cols), F32)

    def softmax_stats(s):
        m_old = m_ref[...]
        m_new = jnp.maximum(m_old, jnp.max(s, axis=0, keepdims=True))
        alpha = jnp.exp2(m_old - m_new)
        p = jnp.exp2(s - m_new)
        l_ref[...] = alpha * l_ref[...] + jnp.sum(p, axis=0, keepdims=True)
        m_ref[...] = m_new
        return alpha, p.astype(BF16)

    alpha, p = softmax_stats(_dot(kc_ref[0], qt))
    acc_ref[...] = alpha * acc_ref[...] + _dot(kct_ref[0], p)

    def chunk(c):
        return pl.ds(pl.multiple_of(c * TK_MLA, TK_MLA), TK_MLA)

    def scores(c, slot):
        s_buf[slot] = _dot(kl_ref[0, chunk(c), :], qt)

    def softmax(slot):
        alpha, p = softmax_stats(s_buf[slot])
        a_buf[slot] = alpha
        p_buf[slot] = p

    def values(c, slot):
        acc_ref[...] = a_buf[slot] * acc_ref[...] + _dot(klt_ref[0, :, chunk(c)], p_buf[slot])

    scores(0, 0)
    scores(1, 1)
    softmax(0)

    def body(j, carry):
        c = 2 * j
        scores(c + 2, 0)
        softmax(1)
        values(c, 0)
        scores(c + 3, 1)
        softmax(0)
        values(c + 1, 1)
        return carry

    lax.fori_loop(0, (n_chunks - 2) // 2, body, 0)
    softmax(1)
    values(n_chunks - 2, 0)
    values(n_chunks - 1, 1)
    o = (acc_ref[...] / l_ref[...]).astype(BF16)
    out_t = jnp.concatenate([_dot(wvt_ref[h], o[:, h * tq:(h + 1) * tq]) for h in range(MLA_HEADS)], axis=0)
    pd_ref[...] = out_t.T.astype(pd_ref.dtype)


def _mla_latent(qm, kl, klt, kc, kct, wvt, seq, nb):
    tq = TQ_MLA
    steps = seq // tq
    cols = MLA_HEADS * tq
    past = kc.shape[1]
    return pl.pallas_call(
        _mla_lat_kernel,
        grid=(nb, steps),
        in_specs=[pl.BlockSpec((1, 1, MLA_KDIM, cols), lambda b, i: (b, i, 0, 0)),
                  pl.BlockSpec((1, seq, MLA_KDIM), lambda b, i: (b, 0, 0)),
                  pl.BlockSpec((1, MLA_KV_LORA, seq), lambda b, i: (b, 0, 0)),
                  pl.BlockSpec((1, past, MLA_KDIM), lambda b, i: (b, 0, 0)),
                  pl.BlockSpec((1, MLA_KV_LORA, past), lambda b, i: (b, 0, 0)),
                  pl.BlockSpec((MLA_HEADS, MLA_VD, MLA_KV_LORA), lambda b, i: (0, 0, 0))],
        out_specs=pl.BlockSpec((tq, MLA_HEADS * MLA_VD), lambda b, i: (b * steps + i, 0)),
        out_shape=jax.ShapeDtypeStruct((nb * seq, MLA_HEADS * MLA_VD), BF16),
        scratch_shapes=[pltpu.VMEM((1, cols), F32), pltpu.VMEM((1, cols), F32),
                        pltpu.VMEM((MLA_KV_LORA, cols), F32),
                        pltpu.VMEM((2, TK_MLA, cols), F32), pltpu.VMEM((2, TK_MLA, cols), BF16),
                        pltpu.VMEM((2, 1, cols), F32)],
        compiler_params=_cparams(("parallel", "arbitrary")),
        name="mla_latent",
    )(qm, kl, klt, kc, kct, wvt)


def _merge_kernel(n_a, ha_ref, hb_ref, mod_ref, g_ref, pa_c, pa_l, pb_c, pb_l, pc_c, pc_l, pd_c, pd_l,
                  wg_ref, wb_ref, wo_ref, o_ref):
    mod = mod_ref[0]
    h = _pick(n_a, ha_ref, hb_ref)
    u = _norm_mod(h, g_ref[...], mod[:, D_MODEL:2 * D_MODEL], mod[:, 0:D_MODEL]).astype(BF16)
    y = None
    for j, (c_ref, l_ref) in enumerate(((pa_c, pa_l), (pb_c, pb_l), (pc_c, pc_l), (pd_c, pd_l))):
        gate = _sigmoid(_dot(u, wg_ref[:, j * D_MODEL:(j + 1) * D_MODEL]))
        term = gate * _dot(_pick(n_a, c_ref, l_ref), wb_ref[j])
        y = term if y is None else y + term
    out = _dot(y.astype(BF16), wo_ref[...])
    o_ref[...] = h + mod[:, 2 * D_MODEL:3 * D_MODEL] * out


def _merge(h, t, mods, g, branches, p, n_ctx_tiles, tiles_per_lat):
    d = D_MODEL
    tm = TM_TOKEN
    const2 = lambda i: (0, 0)
    branch_specs, branch_args = [], []
    for br in branches:
        branch_specs += br.specs(tm)
        branch_args += [br.a, br.b]
    return pl.pallas_call(
        functools.partial(_merge_kernel, h.n_a),
        grid=(t // tm,),
        in_specs=h.specs(tm) + [
            pl.BlockSpec((1, 1, mods.shape[-1]), _mod_row_map(n_ctx_tiles, tiles_per_lat)),
            pl.BlockSpec((1, d), const2)] + branch_specs + [
            pl.BlockSpec((d, N_BRANCH * d), const2),
            pl.BlockSpec((N_BRANCH, 256, d), lambda i: (0, 0, 0)),
            pl.BlockSpec((d, d), const2)],
        out_specs=pl.BlockSpec((tm, d), lambda i: (i, 0)),
        out_shape=jax.ShapeDtypeStruct((t, d), F32),
        compiler_params=_cparams(("parallel",)),
        name="merge",
    )(h.a, h.b, mods, g, *branch_args, p["w_gates"], p["w_branch_out"], p["w_out"])


def _router_gates(logits):
    lane_i = lax.broadcasted_iota(jnp.int32, logits.shape, 1)
    lane = lane_i.astype(F32)
    ninf = jnp.float32(-jnp.inf)

    def first_argmax(x):
        m = jnp.max(x, axis=-1, keepdims=True)
        idx = jnp.min(jnp.where(x == m, lane, jnp.float32(1e9)), axis=-1, keepdims=True)
        return m, idx

    gl = jnp.where(lane_i < MOE_GROUPS, logits, ninf)
    gmax, gsel = first_argmax(gl)
    gp = 1.0 / jnp.sum(jnp.exp(gl - gmax), axis=-1, keepdims=True)
    e_idx = lane_i - MOE_GROUPS
    e_group = lax.shift_right_arithmetic(e_idx, jnp.full_like(e_idx, 2)).astype(F32)
    in_group = (e_idx >= 0) & (e_idx < MOE_EXPERTS) & (e_group == gsel)
    el = jnp.where(in_group, logits, ninf)
    m1, i1 = first_argmax(el)
    m2, i2 = first_argmax(jnp.where(lane == i1, ninf, el))
    e2 = jnp.exp(m2 - m1)
    w1 = gp / (1.0 + e2)
    w2 = gp * e2 / (1.0 + e2)
    return jnp.where(lane == i1, w1, 0.0) + jnp.where(lane == i2, w2, 0.0)


def _moe_kernel(n_ctx, is_final, h_ref, mod_ref, g_ref, *refs):
    if is_final:
        fg_ref, refs = refs[0], refs[1:]
        n_out = 2
    else:
        fg_ref, n_out = None, 1
    wr_ref, w13_ref, w2_ref = refs[:3]
    out_refs = refs[3:3 + n_out]
    u_ref, gate_ref, acc_ref = refs[3 + n_out:]
    e = pl.program_id(1)
    mod = mod_ref[0]

    @pl.when(e == 0)
    def _():
        u = _norm_mod(h_ref[...], g_ref[...], mod[:, 4 * D_MODEL:5 * D_MODEL], mod[:, 3 * D_MODEL:4 * D_MODEL])
        ub = u.astype(BF16)
        u_ref[...] = ub
        gate_ref[...] = _router_gates(_dot(ub, wr_ref[...]))
        acc_ref[...] = jnp.zeros_like(acc_ref)

    ub = u_ref[...]
    lane = lax.broadcasted_iota(jnp.int32, gate_ref.shape, 1)
    gate = jnp.sum(jnp.where(lane == e + MOE_GROUPS, gate_ref[...], 0.0), axis=-1, keepdims=True)
    h13 = _dot(ub, w13_ref[0])
    hid = _silu(h13[:, 0:MOE_FF]) * h13[:, MOE_FF:2 * MOE_FF] * gate
    acc_ref[...] += _dot(hid.astype(BF16), w2_ref[0])

    @pl.when(e == MOE_EXPERTS - 1)
    def _():
        h_out = h_ref[...] + mod[:, 5 * D_MODEL:6 * D_MODEL] * acc_ref[...]
        if fg_ref is None:
            out_refs[0][...] = h_out
        else:
            y = _rms(h_out, fg_ref[...])
            is_ctx = pl.program_id(0) < n_ctx

            @pl.when(is_ctx)
            def _():
                out_refs[0][...] = y

            @pl.when(jnp.logical_not(is_ctx))
            def _():
                out_refs[1][...] = y


def _moe(h, mods, g, p, n_ctx_tiles, tiles_per_lat, final_g=None):
    t, d = h.shape
    tm = TM_MOE
    nct = n_ctx_tiles * TM_TOKEN // tm
    tpl = tiles_per_lat * TM_TOKEN // tm
    in_specs = [pl.BlockSpec((tm, d), lambda i, e: (i, 0)),
                pl.BlockSpec((1, 1, mods.shape[-1]), _mod_row_map(nct, tpl)),
                pl.BlockSpec((1, d), lambda i, e: (0, 0))]
    args = [h, mods, g]
    if final_g is None:
        out_specs = [pl.BlockSpec((tm, d), lambda i, e: (i, 0))]
        out_shape = [jax.ShapeDtypeStruct((t, d), F32)]
    else:
        in_specs.append(pl.BlockSpec((1, d), lambda i, e: (0, 0)))
        args.append(final_g)
        out_specs = [pl.BlockSpec((tm, d), lambda i, e: (jnp.minimum(i, nct - 1), 0)),
                     pl.BlockSpec((tm, d), lambda i, e: (jnp.maximum(i - nct, 0), 0))]
        out_shape = [jax.ShapeDtypeStruct((nct * tm, d), F32), jax.ShapeDtypeStruct((t - nct * tm, d), F32)]
    in_specs += [pl.BlockSpec((d, ROUTER_LANES), lambda i, e: (0, 0)),
                 pl.BlockSpec((1, d, 2 * MOE_FF), lambda i, e: (e, 0, 0)),
                 pl.BlockSpec((1, MOE_FF, d), lambda i, e: (e, 0, 0))]
    args += [p["w_router"], p["w13"], p["w2"]]
    return pl.pallas_call(
        functools.partial(_moe_kernel, nct, final_g is not None),
        grid=(t // tm, MOE_EXPERTS),
        in_specs=in_specs,
        out_specs=out_specs,
        out_shape=out_shape,
        scratch_shapes=[pltpu.VMEM((tm, d), BF16), pltpu.VMEM((tm, ROUTER_LANES), F32),
                        pltpu.VMEM((tm, d), F32)],
        compiler_params=_cparams(("arbitrary", "arbitrary")),
        name="moe",
    )(*args)


def _block_diag(w):
    nd, nb, bw, _ = w.shape
    eye = jnp.eye(nb, dtype=w.dtype)
    return jnp.einsum("dnij,nm->dnimj", w, eye).reshape(nd, nb * bw, nb * bw)


def _layer_params(l, w_in, lru_conv_w, lru_conv_b, lru_wa, lru_ba, lru_wx, lru_bx, lru_lam, w_lru_out, w_na_out,
                  cm_dw_w, cm_dw_b, cm_ln_g, cm_ln_b, w_cm_out, mla_q_norm, mla_wq_b, mla_kv_norm, mla_wkv_b,
                  w_mla_out, w_out, moe_w_group, moe_w_expert, moe_w1, moe_w3, moe_w2):
    wi = w_in[l]
    n_small = 2 * LRU_WIDTH + 3 * NA_HEADS * NA_HD + 2 * CM_WIDTH + MLA_Q_LORA + MLA_KV_LORA
    kr_cols = wi[:, n_small:n_small + MLA_ROPE]
    zpad = jnp.zeros((D_MODEL, 128 - MLA_ROPE), wi.dtype)
    w_small = jnp.concatenate([wi[:, :n_small], kr_cols, zpad, _swap_rope_halves(kr_cols), zpad], axis=1)
    w_gates = wi[:, n_small + MLA_ROPE:]

    qd = MLA_NOPE + MLA_ROPE
    wq = mla_wq_b[l].reshape(MLA_Q_LORA, MLA_HEADS, qd)
    wqn = wq[:, :, :MLA_NOPE].reshape(MLA_Q_LORA, MLA_HEADS * MLA_NOPE)
    wqr = jnp.moveaxis(wq[:, :, MLA_NOPE:], 1, 0)
    rpad = ((0, 0), (0, 0), (0, 128 - MLA_ROPE))
    wkv = mla_wkv_b[l].reshape(MLA_KV_LORA, MLA_HEADS, MLA_NOPE + MLA_VD)
    wkt = jnp.transpose(wkv[:, :, :MLA_NOPE], (1, 2, 0))
    wvt = jnp.transpose(wkv[:, :, MLA_NOPE:], (1, 2, 0))

    router = jnp.concatenate([moe_w_group[l], moe_w_expert[l]], axis=1)
    router = jnp.pad(router, ((0, 0), (0, ROUTER_LANES - router.shape[1])))
    row = lambda a: a.reshape(1, -1)
    return dict(
        w_small=w_small.astype(BF16), w_gates=w_gates.astype(BF16),
        lru_conv_w=lru_conv_w[l], lru_conv_b=row(lru_conv_b[l]),
        lru_wa=_block_diag(lru_wa[l]).astype(BF16), lru_ba=lru_ba[l][:, None, :],
        lru_wx=_block_diag(lru_wx[l]).astype(BF16), lru_bx=lru_bx[l][:, None, :],
        lru_lam=lru_lam[l][:, None, :],
        cm_dw_w=cm_dw_w[l], cm_dw_b=row(cm_dw_b[l]), cm_ln_g=row(cm_ln_g[l]), cm_ln_b=row(cm_ln_b[l]),
        mla_q_norm=row(mla_q_norm[l]), mla_kv_norm=row(mla_kv_norm[l]),
        mla_wq_b=mla_wq_b[l].astype(BF16), mla_wkv_b=mla_wkv_b[l].astype(BF16),
        mla_wqn=wqn.astype(BF16), mla_wqr=jnp.pad(wqr, rpad).astype(BF16),
        mla_wqs=jnp.pad(_swap_rope_halves(wqr), rpad).astype(BF16),
        mla_wkt=wkt.astype(BF16), mla_wvt=wvt.astype(BF16),
        w_branch_out=jnp.stack([w_lru_out[l], w_na_out[l], w_cm_out[l], w_mla_out[l]], axis=0).astype(BF16),
        w_out=w_out[l].astype(BF16),
        w_router=router.astype(BF16),
        w13=jnp.concatenate([moe_w1[l], moe_w3[l]], axis=-1).astype(BF16),
        w2=moe_w2[l].astype(BF16),
    )


def kernel(x_prompt, x_sample, cache_na_k, cache_na_v, cache_mla_ckv, cache_mla_krope, state_lru, c, c_ctx,
           w_ada, b_ada, norm1_g, w_in, lru_conv_w, lru_conv_b, lru_wa, lru_ba, lru_wx, lru_bx, lru_lam,
           w_lru_out, na_rpb, w_na_out, cm_dw_w, cm_dw_b, cm_ln_g, cm_ln_b, w_cm_out, mla_q_norm, mla_wq_b,
           mla_kv_norm, mla_wkv_b, w_mla_out, w_out, norm2_g, moe_w_group, moe_w_expert, moe_w1, moe_w3,
           moe_w2, final_g):
    nb_c, seq_c, d = x_prompt.shape
    nb_l, seq_l, _ = x_sample.shape
    depth = w_in.shape[0]
    past = cache_na_k.shape[2]
    t_ctx = nb_c * seq_c
    t_lat = nb_l * seq_l
    assert d == D_MODEL and seq_l % (GRID_W * NA_QROWS) == 0 and seq_l % seq_c == 0
    assert t_ctx % TM_MOE == 0 and seq_l % TM_MOE == 0 and seq_c % SEQ_CHUNK == 0 and seq_l % TK_MLA == 0
    n_ctx_tiles = t_ctx // TM_TOKEN
    tiles_per_lat = seq_l // TM_TOKEN
    lat_blk0 = t_ctx // seq_l
    assert lat_blk0 * seq_l == t_ctx and (seq_l // TK_MLA) % 2 == 0

    t_all = t_ctx + t_lat
    h = _TwoSource(x_prompt.reshape(t_ctx, d), x_sample.reshape(t_lat, d), n_ctx_tiles, 0)
    n_cond = 1 + nb_l
    cvec = jnp.concatenate([c_ctx[None, :], c, jnp.zeros((-n_cond % 8, d), F32)], axis=0)
    mods = _modulation(cvec, w_ada, b_ada)
    cos, sin = _rope_tables(seq_l)
    zero_state = jnp.zeros((nb_c, 2, LRU_WIDTH), F32)

    st_k, st_v, st_ckv, st_kr, st_lru = [], [], [], [], []
    for l in range(depth):
        p = _layer_params(l, w_in, lru_conv_w, lru_conv_b, lru_wa, lru_ba, lru_wx, lru_bx, lru_lam, w_lru_out,
                          w_na_out, cm_dw_w, cm_dw_b, cm_ln_g, cm_ln_b, w_cm_out, mla_q_norm, mla_wq_b,
                          mla_kv_norm, mla_wkv_b, w_mla_out, w_out, moe_w_group, moe_w_expert, moe_w1, moe_w3,
                          moe_w2)
        mod_l = mods[l].reshape(mods.shape[1], 1, mods.shape[2])
        g1 = norm1_g[l].reshape(1, d)
        g2 = norm2_g[l].reshape(1, d)
        z = _in_proj(h, t_all, mod_l, g1, p["w_small"], n_ctx_tiles, tiles_per_lat)

        pa_c, lru_c = _lru_branch(z["lx"], z["ly"], zero_state, p, seq_c, 0, nb_c)
        pc_c = _conformer_branch(z["ga"], z["gg"], p, seq_c, 0, nb_c)
        pb_c, pd_c, ckvn_c = _ctx_attention(z, p, seq_c, nb_c)

        pa_l, _ = _lru_branch(z["lx"], z["ly"], state_lru[:, l], p, seq_l, lat_blk0, nb_l)
        pc_l = _conformer_branch(z["ga"], z["gg"], p, seq_l, lat_blk0, nb_l)
        bias = _na_bias_tables(na_rpb[l], seq_l // GRID_W)
        pb_l = _na_latent(z, cache_na_k[:, l].reshape(nb_l, past, NA_HEADS * NA_HD),
                          cache_na_v[:, l].reshape(nb_l, past, NA_HEADS * NA_HD), bias, seq_l, lat_blk0, nb_l)
        qm, kl, klt = _mla_prep(z, cos, sin, p, seq_l, lat_blk0, nb_l)
        kc = jnp.concatenate([cache_mla_ckv[:, l], cache_mla_krope[:, l],
                              jnp.zeros((nb_l, past, MLA_KDIM - MLA_KV_LORA - MLA_ROPE), F32)], axis=-1).astype(BF16)
        kct = jnp.swapaxes(cache_mla_ckv[:, l], 1, 2).astype(BF16)
        pd_l = _mla_latent(qm, kl, klt, kc, kct, p["mla_wvt"], seq_l, nb_l)

        branches = [_TwoSource(br_c, br_l, n_ctx_tiles, 0)
                    for br_c, br_l in ((pa_c, pa_l), (pb_c, pb_l), (pc_c, pc_l), (pd_c, pd_l))]
        h_mid = _merge(h, t_all, mod_l, g1, branches, p, n_ctx_tiles, tiles_per_lat)
        if l + 1 < depth:
            h_new, = _moe(h_mid, mod_l, g2, p, n_ctx_tiles, tiles_per_lat)
            h = _TwoSource(h_new, h_new, n_ctx_tiles, n_ctx_tiles)
        else:
            y_prompt, y_sample = _moe(h_mid, mod_l, g2, p, n_ctx_tiles, tiles_per_lat, final_g.reshape(1, d))

        st_k.append(z["k"][:t_ctx].reshape(nb_c, seq_c, NA_HEADS, NA_HD))
        st_v.append(z["v"][:t_ctx].reshape(nb_c, seq_c, NA_HEADS, NA_HD))
        st_ckv.append(ckvn_c.reshape(nb_c, seq_c, MLA_KV_LORA))
        st_kr.append(z["krp"][:t_ctx, :MLA_ROPE].reshape(nb_c, seq_c, MLA_ROPE))
        st_lru.append(lru_c)

    y_prompt = y_prompt.reshape(nb_c, seq_c, d)
    y_sample = y_sample.reshape(nb_l, seq_l, d)
    return (y_prompt, y_sample, jnp.stack(st_k, axis=1), jnp.stack(st_v, axis=1), jnp.stack(st_ckv, axis=1),
            jnp.stack(st_kr, axis=1), jnp.stack(st_lru, axis=1))
```

```python
import functools
import math

import numpy as np
import jax
import jax.numpy as jnp
from jax import lax
from jax.experimental import pallas as pl
from jax.experimental.pallas import tpu as pltpu

F32 = jnp.float32
BF16 = jnp.bfloat16

D_MODEL = 1024
GRID_W = 64
EPS = 1e-6
NEG = -1e30
N_BRANCH = 4
LRU_WIDTH = 256
LRU_BLOCKS = 4
LRU_CONV = 4
LRU_C = 8.0
NA_HEADS = 4
NA_HD = 64
NA_WR = 8
NA_WC = 16
NA_QROWS = 4
NA_KROWS = NA_QROWS + NA_WR
CM_WIDTH = 256
CM_KERNEL = 31
MLA_HEADS = 4
MLA_Q_LORA = 256
MLA_KV_LORA = 128
MLA_NOPE = 64
MLA_ROPE = 32
MLA_VD = 64
MLA_KDIM = 256
ROPE_BASE = 10000.0
MOE_GROUPS = 4
MOE_PER_GROUP = 4
MOE_EXPERTS = 16
MOE_FF = 256
ROUTER_LANES = 128

TM_TOKEN = 512
TM_MOE = 1024
TQ_MLA = 512
TK_MLA = 512
SEQ_CHUNK = 256
VMEM_LIMIT = 56 * 1024 * 1024


def _cparams(sem, vmem=VMEM_LIMIT):
    return pltpu.CompilerParams(dimension_semantics=sem, vmem_limit_bytes=vmem)


def _sigmoid(x):
    return 0.5 * jnp.tanh(0.5 * x) + 0.5


def _silu(x):
    return x * _sigmoid(x)


def _gelu_tanh(x):
    c = math.sqrt(2.0 / math.pi)
    return x * (0.5 * (1.0 + jnp.tanh(c * (x + 0.044715 * (x * x * x)))))


def _softplus(x):
    return jnp.maximum(x, 0.0) + jnp.log1p(jnp.exp(-jnp.abs(x)))


def _rms(x, g):
    return x * lax.rsqrt(jnp.mean(x * x, axis=-1, keepdims=True) + EPS) * g


def _norm_mod(h, g, scale, shift):
    return _rms(h, g) * (1.0 + scale) + shift


def _dot(a, b):
    return jnp.dot(a, b, preferred_element_type=F32)


def _dot_nt(a, b):
    return lax.dot_general(a, b, (((1,), (1,)), ((), ())), preferred_element_type=F32)


def _mod_kernel(c_ref, w_ref, b_ref, o_ref):
    s = _silu(c_ref[...])
    o_ref[0] = jnp.dot(s, w_ref[0], preferred_element_type=F32, precision=lax.Precision.HIGHEST) + b_ref[0]


def _modulation(cvec, w_ada, b_ada):
    depth, d, n = w_ada.shape
    rows = cvec.shape[0]
    tn = 1024
    return pl.pallas_call(
        _mod_kernel,
        grid=(depth, n // tn),
        in_specs=[pl.BlockSpec((rows, d), lambda l, j: (0, 0)),
                  pl.BlockSpec((1, d, tn), lambda l, j: (l, 0, j)),
                  pl.BlockSpec((1, 1, tn), lambda l, j: (l, 0, j))],
        out_specs=pl.BlockSpec((1, rows, tn), lambda l, j: (l, 0, j)),
        out_shape=jax.ShapeDtypeStruct((depth, rows, n), F32),
        compiler_params=_cparams(("parallel", "parallel")),
        name="modulation",
    )(cvec, w_ada, b_ada.reshape(depth, 1, n))


def _mod_row_map(n_ctx_tiles, tiles_per_lat):
    def index_map(i, *_):
        row = jnp.maximum(i - n_ctx_tiles, 0) // tiles_per_lat + (i >= n_ctx_tiles).astype(jnp.int32)
        return (row, 0, 0)
    return index_map


_IN_SEGS = (("lx", 256), ("ly", 256), ("q", 256), ("k", 256), ("v", 256), ("ga", 256), ("gg", 256),
            ("cq", 256), ("ckv", 128), ("krp", 128), ("krs", 128))
_IN_DTYPES = {"q": BF16}


class _TwoSource:
    def __init__(self, a, b, n_a, b_tile0):
        self.a, self.b, self.n_a, self.b_tile0 = a, b, n_a, b_tile0

    def specs(self, tm):
        n_a, b0 = self.n_a, self.b_tile0
        width = self.a.shape[1]
        return [pl.BlockSpec((tm, width), lambda i: (jnp.minimum(i, n_a - 1), 0)),
                pl.BlockSpec((tm, width), lambda i: (jnp.maximum(i - n_a, 0) + b0, 0))]


def _pick(n_a, a_ref, b_ref):
    return jnp.where(pl.program_id(0) < n_a, a_ref[...], b_ref[...])


def _in_kernel(n_a, ha_ref, hb_ref, mod_ref, g_ref, w_ref, *out_refs):
    mod = mod_ref[0]
    h = _pick(n_a, ha_ref, hb_ref)
    u = _norm_mod(h, g_ref[...], mod[:, D_MODEL:2 * D_MODEL], mod[:, 0:D_MODEL]).astype(BF16)
    off = 0
    for (_, width), o_ref in zip(_IN_SEGS, out_refs):
        o_ref[...] = _dot(u, w_ref[:, off:off + width]).astype(o_ref.dtype)
        off += width


def _in_proj(h, t, mods, g, w_small, n_ctx_tiles, tiles_per_lat):
    d = D_MODEL
    tm = TM_TOKEN
    n = w_small.shape[1]
    out_shape = [jax.ShapeDtypeStruct((t, width), _IN_DTYPES.get(name, F32)) for name, width in _IN_SEGS]
    out_specs = [pl.BlockSpec((tm, width), lambda i: (i, 0)) for _, width in _IN_SEGS]
    outs = pl.pallas_call(
        functools.partial(_in_kernel, h.n_a),
        grid=(t // tm,),
        in_specs=h.specs(tm) + [
            pl.BlockSpec((1, 1, mods.shape[-1]), _mod_row_map(n_ctx_tiles, tiles_per_lat)),
            pl.BlockSpec((1, d), lambda i: (0, 0)),
            pl.BlockSpec((d, n), lambda i: (0, 0))],
        out_specs=out_specs,
        out_shape=out_shape,
        compiler_params=_cparams(("parallel",)),
        name="in_proj",
    )(h.a, h.b, mods, g, w_small)
    return dict(zip([s for s, _ in _IN_SEGS], outs))


def _shifted_windows(xw, offsets, length):
    n = xw.shape[0]
    rolled = {0: xw}
    out = {}
    for o in offsets:
        r = o % 8
        if r not in rolled:
            rolled[r] = pltpu.roll(xw, n - r, 0)
        base = o - r
        out[o] = rolled[r][base:base + length]
    return out


def _lru_kernel(lx_ref, ly_ref, h0_ref, cw_ref, cb_ref, wa_ref, ba_ref, wx_ref, bx_ref, lam_ref,
                pa_ref, st_ref, pad_ref, af_ref, bf_ref, ab_ref, bb_ref):
    t = lx_ref.shape[0]
    ch = min(SEQ_CHUNK, t)
    halo = 8
    zeros = jnp.zeros((halo, LRU_WIDTH), F32)
    pad_ref[0:halo, :] = zeros
    pad_ref[halo + t:2 * halo + t, :] = zeros
    pad_ref[halo:halo + t, :] = lx_ref[...]
    pad_l = LRU_CONV // 2
    taps = [halo - pad_l + k for k in range(LRU_CONV)]
    a_refs = (af_ref, ab_ref)
    b_refs = (bf_ref, bb_ref)

    def gates(c, carry):
        r0 = pl.multiple_of(c * ch, ch)
        xw = pad_ref[pl.ds(r0, ch + 2 * halo), :]
        win = _shifted_windows(xw, taps, ch)
        xc = cb_ref[...] + sum(cw_ref[k:k + 1, :] * win[taps[k]] for k in range(LRU_CONV))
        xcb = xc.astype(BF16)
        for d in range(2):
            r = _sigmoid(_dot(xcb, wa_ref[d]) + ba_ref[d])
            i = _sigmoid(_dot(xcb, wx_ref[d]) + bx_ref[d])
            log_a = (-LRU_C) * r * _softplus(-lam_ref[d])
            a = jnp.exp(log_a)
            one_minus_a2 = -jnp.tanh(log_a) * (a * a + 1.0)
            a_refs[d][pl.ds(r0, ch), :] = a
            b_refs[d][pl.ds(r0, ch), :] = jnp.sqrt(one_minus_a2) * (i * xc)
        return carry

    lax.fori_loop(0, t // ch, gates, 0)

    sub = 8
    row = lax.broadcasted_iota(jnp.int32, (sub, LRU_WIDTH), 0)

    def bcast(x, j):
        return jnp.broadcast_to(x[j:j + 1, :], (sub, LRU_WIDTH))

    def scan(c, carry):
        hf, hb = carry
        rf = pl.multiple_of(c * sub, sub)
        rb = pl.multiple_of(t - sub - c * sub, sub)
        a_f, b_f = af_ref[pl.ds(rf, sub), :], bf_ref[pl.ds(rf, sub), :]
        a_b, b_b = ab_ref[pl.ds(rb, sub), :], bb_ref[pl.ds(rb, sub), :]
        out_f = jnp.zeros((sub, LRU_WIDTH), F32)
        out_b = jnp.zeros((sub, LRU_WIDTH), F32)
        for j in range(sub):
            hf = bcast(a_f, j) * hf + bcast(b_f, j)
            out_f = jnp.where(row == j, hf, out_f)
            jb = sub - 1 - j
            hb = bcast(a_b, jb) * hb + bcast(b_b, jb)
            out_b = jnp.where(row == jb, hb, out_b)
        bf_ref[pl.ds(rf, sub), :] = out_f
        bb_ref[pl.ds(rb, sub), :] = out_b
        return hf, hb

    h0 = (jnp.broadcast_to(h0_ref[0, 0:1, :], (sub, LRU_WIDTH)), jnp.broadcast_to(h0_ref[0, 1:2, :], (sub, LRU_WIDTH)))
    hf, hb = lax.fori_loop(0, t // sub, scan, h0)
    st_ref[0, 0:1, :] = hf[0:1, :]
    st_ref[0, 1:2, :] = hb[0:1, :]

    def emit(c, carry):
        r0 = pl.multiple_of(c * ch, ch)
        y = (bf_ref[pl.ds(r0, ch), :] + bb_ref[pl.ds(r0, ch), :]) * _gelu_tanh(ly_ref[pl.ds(r0, ch), :])
        pa_ref[pl.ds(r0, ch), :] = y.astype(pa_ref.dtype)
        return carry

    lax.fori_loop(0, t // ch, emit, 0)


def _lru_branch(lx, ly, h0, p, seq, blk0, nb):
    w = LRU_WIDTH
    const2 = lambda b: (0, 0)
    const3 = lambda b: (0, 0, 0)
    return pl.pallas_call(
        _lru_kernel,
        grid=(nb,),
        in_specs=[pl.BlockSpec((seq, w), lambda b: (blk0 + b, 0)),
                  pl.BlockSpec((seq, w), lambda b: (blk0 + b, 0)),
                  pl.BlockSpec((1, 2, w), lambda b: (b, 0, 0)),
                  pl.BlockSpec((LRU_CONV, w), const2),
                  pl.BlockSpec((1, w), const2),
                  pl.BlockSpec((2, w, w), const3),
                  pl.BlockSpec((2, 1, w), const3),
                  pl.BlockSpec((2, w, w), const3),
                  pl.BlockSpec((2, 1, w), const3),
                  pl.BlockSpec((2, 1, w), const3)],
        out_specs=[pl.BlockSpec((seq, w), lambda b: (b, 0)),
                   pl.BlockSpec((1, 2, w), lambda b: (b, 0, 0))],
        out_shape=[jax.ShapeDtypeStruct((nb * seq, w), BF16),
                   jax.ShapeDtypeStruct((nb, 2, w), F32)],
        scratch_shapes=[pltpu.VMEM((seq + 16, w), F32)] + [pltpu.VMEM((seq, w), F32)] * 4,
        compiler_params=_cparams(("parallel",)),
        name="rglru",
    )(lx, ly, h0, p["lru_conv_w"], p["lru_conv_b"], p["lru_wa"], p["lru_ba"], p["lru_wx"], p["lru_bx"],
      p["lru_lam"])


def _cm_kernel(ga_ref, gg_ref, w_ref, b_ref, lg_ref, lb_ref, pc_ref, pad_ref):
    t = ga_ref.shape[0]
    ch = min(SEQ_CHUNK, t)
    halo = 16
    zeros = jnp.zeros((halo, CM_WIDTH), F32)
    pad_ref[0:halo, :] = zeros
    pad_ref[halo + t:2 * halo + t, :] = zeros

    def glu(c, carry):
        r0 = pl.multiple_of(c * ch, ch)
        pad_ref[pl.ds(halo + r0, ch), :] = ga_ref[pl.ds(r0, ch), :] * _sigmoid(gg_ref[pl.ds(r0, ch), :])
        return carry

    lax.fori_loop(0, t // ch, glu, 0)
    taps = [halo - CM_KERNEL // 2 + k for k in range(CM_KERNEL)]

    def conv(c, carry):
        r0 = pl.multiple_of(c * ch, ch)
        xw = pad_ref[pl.ds(r0, ch + 2 * halo), :]
        win = _shifted_windows(xw, taps, ch)
        z = b_ref[...] + sum(w_ref[k:k + 1, :] * win[taps[k]] for k in range(CM_KERNEL))
        mu = jnp.mean(z, axis=-1, keepdims=True)
        zc = z - mu
        var = jnp.mean(zc * zc, axis=-1, keepdims=True)
        y = zc * lax.rsqrt(var + EPS) * lg_ref[...] + lb_ref[...]
        pc_ref[pl.ds(r0, ch), :] = _silu(y).astype(pc_ref.dtype)
        return carry

    lax.fori_loop(0, t // ch, conv, 0)


def _conformer_branch(ga, gg, p, seq, blk0, nb):
    w = CM_WIDTH
    const2 = lambda b: (0, 0)
    return pl.pallas_call(
        _cm_kernel,
        grid=(nb,),
        in_specs=[pl.BlockSpec((seq, w), lambda b: (blk0 + b, 0)),
                  pl.BlockSpec((seq, w), lambda b: (blk0 + b, 0)),
                  pl.BlockSpec((CM_KERNEL, w), const2),
                  pl.BlockSpec((1, w), const2),
                  pl.BlockSpec((1, w), const2),
                  pl.BlockSpec((1, w), const2)],
        out_specs=pl.BlockSpec((seq, w), lambda b: (b, 0)),
        out_shape=jax.ShapeDtypeStruct((nb * seq, w), BF16),
        scratch_shapes=[pltpu.VMEM((seq + 32, w), F32)],
        compiler_params=_cparams(("parallel",)),
        name="conformer",
    )(ga, gg, p["cm_dw_w"], p["cm_dw_b"], p["cm_ln_g"], p["cm_ln_b"])


def _softmax_pv(scores, values):
    m = functools.reduce(jnp.maximum, [jnp.max(s, axis=-1, keepdims=True) for s in scores])
    ps = [jnp.exp(s - m) for s in scores]
    l = sum(jnp.sum(p, axis=-1, keepdims=True) for p in ps)
    o = sum(_dot(p.astype(BF16), v) for p, v in zip(ps, values))
    return o / l


def _ctx_attn_kernel(q_ref, k_ref, v_ref, cq_ref, ckv_ref, krp_ref, qn_ref, wq_ref, kvn_ref, wkv_ref,
                     pb_ref, pd_ref, ckvn_ref):
    q = q_ref[...]
    k = k_ref[...].astype(BF16)
    v = v_ref[...].astype(BF16)
    na_scale = NA_HD ** -0.5
    for h in range(NA_HEADS):
        sl = slice(h * NA_HD, (h + 1) * NA_HD)
        s = _dot_nt(q[:, sl], k[:, sl]) * na_scale
        pb_ref[:, sl] = _softmax_pv([s], [v[:, sl]]).astype(pb_ref.dtype)

    ckv_n = _rms(ckv_ref[...], kvn_ref[...])
    ckvn_ref[...] = ckv_n
    mla_scale = (MLA_NOPE + MLA_ROPE) ** -0.5
    qf = _dot(_rms(cq_ref[...], qn_ref[...]).astype(BF16), wq_ref[...]).astype(BF16)
    kv = _dot(ckv_n.astype(BF16), wkv_ref[...]).astype(BF16)
    kr = krp_ref[:, 0:MLA_ROPE].astype(BF16)
    qd = MLA_NOPE + MLA_ROPE
    kd = MLA_NOPE + MLA_VD
    for h in range(MLA_HEADS):
        qn = qf[:, h * qd:h * qd + MLA_NOPE]
        qr = qf[:, h * qd + MLA_NOPE:(h + 1) * qd]
        kn = kv[:, h * kd:h * kd + MLA_NOPE]
        vm = kv[:, h * kd + MLA_NOPE:(h + 1) * kd]
        s = (_dot_nt(qn, kn) + _dot_nt(qr, kr)) * mla_scale
        pd_ref[:, h * MLA_VD:(h + 1) * MLA_VD] = _softmax_pv([s], [vm]).astype(pd_ref.dtype)


def _ctx_attention(z, p, seq, nb):
    const2 = lambda b: (0, 0)
    row = lambda b: (b, 0)
    return pl.pallas_call(
        _ctx_attn_kernel,
        grid=(nb,),
        in_specs=[pl.BlockSpec((seq, 256), row), pl.BlockSpec((seq, 256), row), pl.BlockSpec((seq, 256), row),
                  pl.BlockSpec((seq, MLA_Q_LORA), row), pl.BlockSpec((seq, MLA_KV_LORA), row),
                  pl.BlockSpec((seq, 128), row),
                  pl.BlockSpec((1, MLA_Q_LORA), const2),
                  pl.BlockSpec(p["mla_wq_b"].shape, const2),
                  pl.BlockSpec((1, MLA_KV_LORA), const2),
                  pl.BlockSpec(p["mla_wkv_b"].shape, const2)],
        out_specs=[pl.BlockSpec((seq, 256), row), pl.BlockSpec((seq, 256), row),
                   pl.BlockSpec((seq, MLA_KV_LORA), row)],
        out_shape=[jax.ShapeDtypeStruct((nb * seq, 256), BF16),
                   jax.ShapeDtypeStruct((nb * seq, 256), BF16),
                   jax.ShapeDtypeStruct((nb * seq, MLA_KV_LORA), F32)],
        compiler_params=_cparams(("parallel",)),
        name="ctx_attention",
    )(z["q"], z["k"], z["v"], z["cq"], z["ckv"], z["krp"], p["mla_q_norm"], p["mla_wq_b"],
      p["mla_kv_norm"], p["mla_wkv_b"])


def _na_bias_tables(rpb, rows):
    heads, n_rel_r, n_rel_c = rpb.shape
    w = GRID_W
    u = jnp.concatenate([rpb[:, :, NA_WC - 1:], jnp.zeros((heads, n_rel_r, 2 * w - n_rel_c), rpb.dtype),
                         rpb[:, :, :NA_WC - 1]], axis=-1)
    toep = jnp.tile(u, (1, 1, w))[:, :, :w * (2 * w - 1)].reshape(heads, n_rel_r, w, 2 * w - 1)[..., :w]
    qc = np.arange(w)
    cs = np.clip(qc - NA_WC // 2, 0, w - NA_WC)
    col_ok = (qc[None, :] >= cs[:, None]) & (qc[None, :] < cs[:, None] + NA_WC)
    toep = jnp.where(col_ok, toep, NEG)
    masked = jnp.full((heads, w, w), NEG, rpb.dtype)
    n_steps = rows // NA_QROWS
    tables = []
    for step in (0, 1, n_steps - 1):
        start = int(np.clip(NA_QROWS * step - NA_WR // 2, 0, rows - NA_KROWS))
        block_rows = []
        for a in range(NA_QROWS):
            qr = NA_QROWS * step + a
            rs = int(np.clip(qr - NA_WR // 2, 0, rows - NA_WR))
            blocks = []
            for b in range(NA_KROWS):
                kr = start + b
                blocks.append(toep[:, kr - qr + NA_WR - 1] if rs <= kr < rs + NA_WR else masked)
            block_rows.append(jnp.concatenate(blocks, axis=-1))
        tables.append(jnp.concatenate(block_rows, axis=1))
    return jnp.stack(tables, axis=0)


def _na_lat_kernel(q_ref, k_ref, v_ref, ck_ref, cv_ref, bias_ref, pb_ref):
    i = pl.program_id(1)
    rows = k_ref.shape[0] // GRID_W
    start = jnp.clip(NA_QROWS * i - NA_WR // 2, 0, rows - NA_KROWS) * GRID_W
    start = pl.multiple_of(start, GRID_W)
    nk = NA_KROWS * GRID_W
    q = q_ref[...]
    kw = k_ref[pl.ds(start, nk), :].astype(BF16)
    vw = v_ref[pl.ds(start, nk), :].astype(BF16)
    ck = ck_ref[0].astype(BF16)
    cv = cv_ref[0].astype(BF16)
    scale = NA_HD ** -0.5
    for h in range(NA_HEADS):
        sl = slice(h * NA_HD, (h + 1) * NA_HD)
        s_loc = _dot_nt(q[:, sl], kw[:, sl]) * scale + bias_ref[0, h]
        s_ctx = _dot_nt(q[:, sl], ck[:, sl]) * scale
        pb_ref[:, sl] = _softmax_pv([s_loc, s_ctx], [vw[:, sl], cv[:, sl]]).astype(pb_ref.dtype)


def _na_latent(z, cache_k, cache_v, bias, seq, blk0, nb):
    tq = NA_QROWS * GRID_W
    steps = seq // tq
    nk = NA_KROWS * GRID_W
    past = cache_k.shape[1]

    def bias_map(b, i):
        return (jnp.where(i == 0, 0, jnp.where(i == steps - 1, 2, 1)), 0, 0, 0)

    return pl.pallas_call(
        _na_lat_kernel,
        grid=(nb, steps),
        in_specs=[pl.BlockSpec((tq, 256), lambda b, i: (blk0 * steps + b * steps + i, 0)),
                  pl.BlockSpec((seq, 256), lambda b, i: (blk0 + b, 0)),
                  pl.BlockSpec((seq, 256), lambda b, i: (blk0 + b, 0)),
                  pl.BlockSpec((1, past, 256), lambda b, i: (b, 0, 0)),
                  pl.BlockSpec((1, past, 256), lambda b, i: (b, 0, 0)),
                  pl.BlockSpec((1, NA_HEADS, tq, nk), bias_map)],
        out_specs=pl.BlockSpec((tq, 256), lambda b, i: (b * steps + i, 0)),
        out_shape=jax.ShapeDtypeStruct((nb * seq, 256), BF16),
        compiler_params=_cparams(("parallel", "arbitrary")),
        name="na_latent",
    )(z["q"], z["k"], z["v"], cache_k, cache_v, bias)


def _rope_tables(seq):
    t = np.arange(seq)
    pos = np.stack([t // GRID_W, t % GRID_W], axis=-1).astype(np.float32)
    n_freq = MLA_ROPE // 4
    inv = jnp.asarray(ROPE_BASE, F32) ** (-jnp.arange(n_freq, dtype=F32) / n_freq)
    ang = jnp.asarray(pos)[:, :, None] * inv
    cos, sin = jnp.cos(ang), jnp.sin(ang)
    c = jnp.stack([cos, cos], axis=2).reshape(seq, MLA_ROPE)
    s = jnp.stack([-sin, sin], axis=2).reshape(seq, MLA_ROPE)
    pad = ((0, 0), (0, 128 - MLA_ROPE))
    return jnp.pad(c, pad), jnp.pad(s, pad)


def _swap_rope_halves(w):
    w4 = w.reshape(w.shape[:-1] + (2, 2, MLA_ROPE // 4))
    return w4[..., ::-1, :].reshape(w.shape)


def _mla_prep_kernel(cq_ref, ckv_ref, krp_ref, krs_ref, cos_ref, sin_ref, qn_ref, wqn_ref, wqr_ref, wqs_ref,
                     kvn_ref, wkt_ref, qm_ref, kl_ref, klt_ref):
    tq = cq_ref.shape[0]
    cos = cos_ref[...]
    sin = sin_ref[...]
    ckv_n = _rms(ckv_ref[...], kvn_ref[...])
    kl_ref[0, :, 0:128] = ckv_n.astype(kl_ref.dtype)
    kl_ref[0, :, 128:256] = (krp_ref[...] * cos + krs_ref[...] * sin).astype(kl_ref.dtype)
    klt_ref[0] = ckv_n.T.astype(klt_ref.dtype)
    cqn = _rms(cq_ref[...], qn_ref[...]).astype(BF16)
    qn = _dot(cqn, wqn_ref[...]).astype(BF16)
    scale = (MLA_NOPE + MLA_ROPE) ** -0.5 * math.log2(math.e)
    for h in range(MLA_HEADS):
        qa = _dot(qn[:, h * MLA_NOPE:(h + 1) * MLA_NOPE], wkt_ref[h])
        qr = _dot(cqn, wqr_ref[h]) * cos + _dot(cqn, wqs_ref[h]) * sin
        qm_ref[0, 0, 0:128, h * tq:(h + 1) * tq] = (qa * scale).T.astype(qm_ref.dtype)
        qm_ref[0, 0, 128:256, h * tq:(h + 1) * tq] = (qr * scale).T.astype(qm_ref.dtype)


def _mla_prep(z, cos, sin, p, seq, blk0, nb):
    tq = TQ_MLA
    steps = seq // tq
    tok = lambda b, i: (blk0 * steps + b * steps + i, 0)
    pos = lambda b, i: (i, 0)
    const2 = lambda b, i: (0, 0)
    const3 = lambda b, i: (0, 0, 0)
    return pl.pallas_call(
        _mla_prep_kernel,
        grid=(nb, steps),
        in_specs=[pl.BlockSpec((tq, MLA_Q_LORA), tok), pl.BlockSpec((tq, MLA_KV_LORA), tok),
                  pl.BlockSpec((tq, 128), tok), pl.BlockSpec((tq, 128), tok),
                  pl.BlockSpec((tq, 128), pos), pl.BlockSpec((tq, 128), pos),
                  pl.BlockSpec((1, MLA_Q_LORA), const2),
                  pl.BlockSpec((MLA_Q_LORA, MLA_HEADS * MLA_NOPE), const2),
                  pl.BlockSpec((MLA_HEADS, MLA_Q_LORA, 128), const3),
                  pl.BlockSpec((MLA_HEADS, MLA_Q_LORA, 128), const3),
                  pl.BlockSpec((1, MLA_KV_LORA), const2),
                  pl.BlockSpec((MLA_HEADS, MLA_NOPE, MLA_KV_LORA), const3)],
        out_specs=[pl.BlockSpec((1, 1, MLA_KDIM, MLA_HEADS * tq), lambda b, i: (b, i, 0, 0)),
                   pl.BlockSpec((1, tq, MLA_KDIM), lambda b, i: (b, i, 0)),
                   pl.BlockSpec((1, MLA_KV_LORA, tq), lambda b, i: (b, 0, i))],
        out_shape=[jax.ShapeDtypeStruct((nb, steps, MLA_KDIM, MLA_HEADS * tq), BF16),
                   jax.ShapeDtypeStruct((nb, seq, MLA_KDIM), BF16),
                   jax.ShapeDtypeStruct((nb, MLA_KV_LORA, seq), BF16)],
        compiler_params=_cparams(("parallel", "parallel")),
        name="mla_prep",
    )(z["cq"], z["ckv"], z["krp"], z["krs"], cos, sin, p["mla_q_norm"], p["mla_wqn"], p["mla_wqr"],
      p["mla_wqs"], p["mla_kv_norm"], p["mla_wkt"])


def _mla_lat_kernel(qt_ref, kl_ref, klt_ref, kc_ref, kct_ref, wvt_ref, pd_ref, m_ref, l_ref, acc_ref,
                    s_buf, p_buf, a_buf):
    qt = qt_ref[0, 0]
    cols = qt.shape[1]
    tq = cols // MLA_HEADS
    n_chunks = kl_ref.shape[1] // TK_MLA
    m_ref[...] = jnp.full((1, cols), NEG, F32)
    l_ref[...] = jnp.zeros((1, cols), F32)
    acc_ref[...] = jnp.zeros((MLA_KV_LORA, cols), F32)

    def softmax_stats(s):
        m_old = m_ref[...]
        m_new = jnp.maximum(m_old, jnp.max(s, axis=0, keepdims=True))
        alpha = jnp.exp2(m_old - m_new)
        p = jnp.exp2(s - m_new)
        l_ref[...] = alpha * l_ref[...] + jnp.sum(p, axis=0, keepdims=True)
        m_ref[...] = m_new
        return alpha, p.astype(BF16)

    alpha, p = softmax_stats(_dot(kc_ref[0], qt))
    acc_ref[...] = alpha * acc_ref[...] + _dot(kct_ref[0], p)

    def chunk(c):
        return pl.ds(pl.multiple_of(c * TK_MLA, TK_MLA), TK_MLA)

    def scores(c, slot):
        s_buf[slot] = _dot(kl_ref[0, chunk(c), :], qt)

    def softmax(slot):
        alpha, p = softmax_stats(s_buf[slot])
        a_buf[slot] = alpha
        p_buf[slot] = p

    def values(c, slot):
        acc_ref[...] = a_buf[slot] * acc_ref[...] + _dot(klt_ref[0, :, chunk(c)], p_buf[slot])

    scores(0, 0)
    scores(1, 1)
    softmax(0)

    def body(j, carry):
        c = 2 * j
        scores(c + 2, 0)
        softmax(1)
        values(c, 0)
        scores(c + 3, 1)
        softmax(0)
        values(c + 1, 1)
        return carry

    lax.fori_loop(0, (n_chunks - 2) // 2, body, 0)
    softmax(1)
    values(n_chunks - 2, 0)
    values(n_chunks - 1, 1)
    o = (acc_ref[...] / l_ref[...]).astype(BF16)
    out_t = jnp.concatenate([_dot(wvt_ref[h], o[:, h * tq:(h + 1) * tq]) for h in range(MLA_HEADS)], axis=0)
    pd_ref[...] = out_t.T.astype(pd_ref.dtype)


def _mla_latent(qm, kl, klt, kc, kct, wvt, seq, nb):
    tq = TQ_MLA
    steps = seq // tq
    cols = MLA_HEADS * tq
    past = kc.shape[1]
    return pl.pallas_call(
        _mla_lat_kernel,
        grid=(nb, steps),
        in_specs=[pl.BlockSpec((1, 1, MLA_KDIM, cols), lambda b, i: (b, i, 0, 0)),
                  pl.BlockSpec((1, seq, MLA_KDIM), lambda b, i: (b, 0, 0)),
                  pl.BlockSpec((1, MLA_KV_LORA, seq), lambda b, i: (b, 0, 0)),
                  pl.BlockSpec((1, past, MLA_KDIM), lambda b, i: (b, 0, 0)),
                  pl.BlockSpec((1, MLA_KV_LORA, past), lambda b, i: (b, 0, 0)),
                  pl.BlockSpec((MLA_HEADS, MLA_VD, MLA_KV_LORA), lambda b, i: (0, 0, 0))],
        out_specs=pl.BlockSpec((tq, MLA_HEADS * MLA_VD), lambda b, i: (b * steps + i, 0)),
        out_shape=jax.ShapeDtypeStruct((nb * seq, MLA_HEADS * MLA_VD), BF16),
        scratch_shapes=[pltpu.VMEM((1, cols), F32), pltpu.VMEM((1, cols), F32),
                        pltpu.VMEM((MLA_KV_LORA, cols), F32),
                        pltpu.VMEM((2, TK_MLA, cols), F32), pltpu.VMEM((2, TK_MLA, cols), BF16),
                        pltpu.VMEM((2, 1, cols), F32)],
        compiler_params=_cparams(("parallel", "arbitrary")),
        name="mla_latent",
    )(qm, kl, klt, kc, kct, wvt)


def _merge_kernel(n_a, ha_ref, hb_ref, mod_ref, g_ref, pa_c, pa_l, pb_c, pb_l, pc_c, pc_l, pd_c, pd_l,
                  wg_ref, wb_ref, wo_ref, o_ref):
    mod = mod_ref[0]
    h = _pick(n_a, ha_ref, hb_ref)
    u = _norm_mod(h, g_ref[...], mod[:, D_MODEL:2 * D_MODEL], mod[:, 0:D_MODEL]).astype(BF16)
    y = None
    for j, (c_ref, l_ref) in enumerate(((pa_c, pa_l), (pb_c, pb_l), (pc_c, pc_l), (pd_c, pd_l))):
        gate = _sigmoid(_dot(u, wg_ref[:, j * D_MODEL:(j + 1) * D_MODEL]))
        term = gate * _dot(_pick(n_a, c_ref, l_ref), wb_ref[j])
        y = term if y is None else y + term
    out = _dot(y.astype(BF16), wo_ref[...])
    o_ref[...] = h + mod[:, 2 * D_MODEL:3 * D_MODEL] * out


def _merge(h, t, mods, g, branches, p, n_ctx_tiles, tiles_per_lat):
    d = D_MODEL
    tm = TM_TOKEN
    const2 = lambda i: (0, 0)
    branch_specs, branch_args = [], []
    for br in branches:
        branch_specs += br.specs(tm)
        branch_args += [br.a, br.b]
    return pl.pallas_call(
        functools.partial(_merge_kernel, h.n_a),
        grid=(t // tm,),
        in_specs=h.specs(tm) + [
            pl.BlockSpec((1, 1, mods.shape[-1]), _mod_row_map(n_ctx_tiles, tiles_per_lat)),
            pl.BlockSpec((1, d), const2)] + branch_specs + [
            pl.BlockSpec((d, N_BRANCH * d), const2),
            pl.BlockSpec((N_BRANCH, 256, d), lambda i: (0, 0, 0)),
            pl.BlockSpec((d, d), const2)],
        out_specs=pl.BlockSpec((tm, d), lambda i: (i, 0)),
        out_shape=jax.ShapeDtypeStruct((t, d), F32),
        compiler_params=_cparams(("parallel",)),
        name="merge",
    )(h.a, h.b, mods, g, *branch_args, p["w_gates"], p["w_branch_out"], p["w_out"])


def _router_gates(logits):
    lane_i = lax.broadcasted_iota(jnp.int32, logits.shape, 1)
    lane = lane_i.astype(F32)
    ninf = jnp.float32(-jnp.inf)

    def first_argmax(x):
        m = jnp.max(x, axis=-1, keepdims=True)
        idx = jnp.min(jnp.where(x == m, lane, jnp.float32(1e9)), axis=-1, keepdims=True)
        return m, idx

    gl = jnp.where(lane_i < MOE_GROUPS, logits, ninf)
    gmax, gsel = first_argmax(gl)
    gp = 1.0 / jnp.sum(jnp.exp(gl - gmax), axis=-1, keepdims=True)
    e_idx = lane_i - MOE_GROUPS
    e_group = lax.shift_right_arithmetic(e_idx, jnp.full_like(e_idx, 2)).astype(F32)
    in_group = (e_idx >= 0) & (e_idx < MOE_EXPERTS) & (e_group == gsel)
    el = jnp.where(in_group, logits, ninf)
    m1, i1 = first_argmax(el)
    m2, i2 = first_argmax(jnp.where(lane == i1, ninf, el))
    e2 = jnp.exp(m2 - m1)
    w1 = gp / (1.0 + e2)
    w2 = gp * e2 / (1.0 + e2)
    return jnp.where(lane == i1, w1, 0.0) + jnp.where(lane == i2, w2, 0.0)


def _moe_kernel(n_ctx, is_final, h_ref, mod_ref, g_ref, *refs):
    if is_final:
        fg_ref, refs = refs[0], refs[1:]
        n_out = 2
    else:
        fg_ref, n_out = None, 1
    wr_ref, w13_ref, w2_ref = refs[:3]
    out_refs = refs[3:3 + n_out]
    u_ref, gate_ref, acc_ref = refs[3 + n_out:]
    e = pl.program_id(1)
    mod = mod_ref[0]

    @pl.when(e == 0)
    def _():
        u = _norm_mod(h_ref[...], g_ref[...], mod[:, 4 * D_MODEL:5 * D_MODEL], mod[:, 3 * D_MODEL:4 * D_MODEL])
        ub = u.astype(BF16)
        u_ref[...] = ub
        gate_ref[...] = _router_gates(_dot(ub, wr_ref[...]))
        acc_ref[...] = jnp.zeros_like(acc_ref)

    ub = u_ref[...]
    lane = lax.broadcasted_iota(jnp.int32, gate_ref.shape, 1)
    gate = jnp.sum(jnp.where(lane == e + MOE_GROUPS, gate_ref[...], 0.0), axis=-1, keepdims=True)
    h13 = _dot(ub, w13_ref[0])
    hid = _silu(h13[:, 0:MOE_FF]) * h13[:, MOE_FF:2 * MOE_FF] * gate
    acc_ref[...] += _dot(hid.astype(BF16), w2_ref[0])

    @pl.when(e == MOE_EXPERTS - 1)
    def _():
        h_out = h_ref[...] + mod[:, 5 * D_MODEL:6 * D_MODEL] * acc_ref[...]
        if fg_ref is None:
            out_refs[0][...] = h_out
        else:
            y = _rms(h_out, fg_ref[...])
            is_ctx = pl.program_id(0) < n_ctx

            @pl.when(is_ctx)
            def _():
                out_refs[0][...] = y

            @pl.when(jnp.logical_not(is_ctx))
            def _():
                out_refs[1][...] = y


def _moe(h, mods, g, p, n_ctx_tiles, tiles_per_lat, final_g=None):
    t, d = h.shape
    tm = TM_MOE
    nct = n_ctx_tiles * TM_TOKEN // tm
    tpl = tiles_per_lat * TM_TOKEN // tm
    in_specs = [pl.BlockSpec((tm, d), lambda i, e: (i, 0)),
                pl.BlockSpec((1, 1, mods.shape[-1]), _mod_row_map(nct, tpl)),
                pl.BlockSpec((1, d), lambda i, e: (0, 0))]
    args = [h, mods, g]
    if final_g is None:
        out_specs = [pl.BlockSpec((tm, d), lambda i, e: (i, 0))]
        out_shape = [jax.ShapeDtypeStruct((t, d), F32)]
    else:
        in_specs.append(pl.BlockSpec((1, d), lambda i, e: (0, 0)))
        args.append(final_g)
        out_specs = [pl.BlockSpec((tm, d), lambda i, e: (jnp.minimum(i, nct - 1), 0)),
                     pl.BlockSpec((tm, d), lambda i, e: (jnp.maximum(i - nct, 0), 0))]
        out_shape = [jax.ShapeDtypeStruct((nct * tm, d), F32), jax.ShapeDtypeStruct((t - nct * tm, d), F32)]
    in_specs += [pl.BlockSpec((d, ROUTER_LANES), lambda i, e: (0, 0)),
                 pl.BlockSpec((1, d, 2 * MOE_FF), lambda i, e: (e, 0, 0)),
                 pl.BlockSpec((1, MOE_FF, d), lambda i, e: (e, 0, 0))]
    args += [p["w_router"], p["w13"], p["w2"]]
    return pl.pallas_call(
        functools.partial(_moe_kernel, nct, final_g is not None),
        grid=(t // tm, MOE_EXPERTS),
        in_specs=in_specs,
        out_specs=out_specs,
        out_shape=out_shape,
        scratch_shapes=[pltpu.VMEM((tm, d), BF16), pltpu.VMEM((tm, ROUTER_LANES), F32),
                        pltpu.VMEM((tm, d), F32)],
        compiler_params=_cparams(("arbitrary", "arbitrary")),
        name="moe",
    )(*args)


def _block_diag(w):
    nd, nb, bw, _ = w.shape
    eye = jnp.eye(nb, dtype=w.dtype)
    return jnp.einsum("dnij,nm->dnimj", w, eye).reshape(nd, nb * bw, nb * bw)


def _layer_params(l, w_in, lru_conv_w, lru_conv_b, lru_wa, lru_ba, lru_wx, lru_bx, lru_lam, w_lru_out, w_na_out,
                  cm_dw_w, cm_dw_b, cm_ln_g, cm_ln_b, w_cm_out, mla_q_norm, mla_wq_b, mla_kv_norm, mla_wkv_b,
                  w_mla_out, w_out, moe_w_group, moe_w_expert, moe_w1, moe_w3, moe_w2):
    wi = w_in[l]
    n_small = 2 * LRU_WIDTH + 3 * NA_HEADS * NA_HD + 2 * CM_WIDTH + MLA_Q_LORA + MLA_KV_LORA
    kr_cols = wi[:, n_small:n_small + MLA_ROPE]
    zpad = jnp.zeros((D_MODEL, 128 - MLA_ROPE), wi.dtype)
    w_small = jnp.concatenate([wi[:, :n_small], kr_cols, zpad, _swap_rope_halves(kr_cols), zpad], axis=1)
    w_gates = wi[:, n_small + MLA_ROPE:]

    qd = MLA_NOPE + MLA_ROPE
    wq = mla_wq_b[l].reshape(MLA_Q_LORA, MLA_HEADS, qd)
    wqn = wq[:, :, :MLA_NOPE].reshape(MLA_Q_LORA, MLA_HEADS * MLA_NOPE)
    wqr = jnp.moveaxis(wq[:, :, MLA_NOPE:], 1, 0)
    rpad = ((0, 0), (0, 0), (0, 128 - MLA_ROPE))
    wkv = mla_wkv_b[l].reshape(MLA_KV_LORA, MLA_HEADS, MLA_NOPE + MLA_VD)
    wkt = jnp.transpose(wkv[:, :, :MLA_NOPE], (1, 2, 0))
    wvt = jnp.transpose(wkv[:, :, MLA_NOPE:], (1, 2, 0))

    router = jnp.concatenate([moe_w_group[l], moe_w_expert[l]], axis=1)
    router = jnp.pad(router, ((0, 0), (0, ROUTER_LANES - router.shape[1])))
    row = lambda a: a.reshape(1, -1)
    return dict(
        w_small=w_small.astype(BF16), w_gates=w_gates.astype(BF16),
        lru_conv_w=lru_conv_w[l], lru_conv_b=row(lru_conv_b[l]),
        lru_wa=_block_diag(lru_wa[l]).astype(BF16), lru_ba=lru_ba[l][:, None, :],
        lru_wx=_block_diag(lru_wx[l]).astype(BF16), lru_bx=lru_bx[l][:, None, :],
        lru_lam=lru_lam[l][:, None, :],
        cm_dw_w=cm_dw_w[l], cm_dw_b=row(cm_dw_b[l]), cm_ln_g=row(cm_ln_g[l]), cm_ln_b=row(cm_ln_b[l]),
        mla_q_norm=row(mla_q_norm[l]), mla_kv_norm=row(mla_kv_norm[l]),
        mla_wq_b=mla_wq_b[l].astype(BF16), mla_wkv_b=mla_wkv_b[l].astype(BF16),
        mla_wqn=wqn.astype(BF16), mla_wqr=jnp.pad(wqr, rpad).astype(BF16),
        mla_wqs=jnp.pad(_swap_rope_halves(wqr), rpad).astype(BF16),
        mla_wkt=wkt.astype(BF16), mla_wvt=wvt.astype(BF16),
        w_branch_out=jnp.stack([w_lru_out[l], w_na_out[l], w_cm_out[l], w_mla_out[l]], axis=0).astype(BF16),
        w_out=w_out[l].astype(BF16),
        w_router=router.astype(BF16),
        w13=jnp.concatenate([moe_w1[l], moe_w3[l]], axis=-1).astype(BF16),
        w2=moe_w2[l].astype(BF16),
    )


def kernel(x_prompt, x_sample, cache_na_k, cache_na_v, cache_mla_ckv, cache_mla_krope, state_lru, c, c_ctx,
           w_ada, b_ada, norm1_g, w_in, lru_conv_w, lru_conv_b, lru_wa, lru_ba, lru_wx, lru_bx, lru_lam,
           w_lru_out, na_rpb, w_na_out, cm_dw_w, cm_dw_b, cm_ln_g, cm_ln_b, w_cm_out, mla_q_norm, mla_wq_b,
           mla_kv_norm, mla_wkv_b, w_mla_out, w_out, norm2_g, moe_w_group, moe_w_expert, moe_w1, moe_w3,
           moe_w2, final_g):
    nb_c, seq_c, d = x_prompt.shape
    nb_l, seq_l, _ = x_sample.shape
    depth = w_in.shape[0]
    past = cache_na_k.shape[2]
    t_ctx = nb_c * seq_c
    t_lat = nb_l * seq_l
    assert d == D_MODEL and seq_l % (GRID_W * NA_QROWS) == 0 and seq_l % seq_c == 0
    assert t_ctx % TM_MOE == 0 and seq_l % TM_MOE == 0 and seq_c % SEQ_CHUNK == 0 and seq_l % TK_MLA == 0
    n_ctx_tiles = t_ctx // TM_TOKEN
    tiles_per_lat = seq_l // TM_TOKEN
    lat_blk0 = t_ctx // seq_l
    assert lat_blk0 * seq_l == t_ctx and (seq_l // TK_MLA) % 2 == 0

    t_all = t_ctx + t_lat
    h = _TwoSource(x_prompt.reshape(t_ctx, d), x_sample.reshape(t_lat, d), n_ctx_tiles, 0)
    n_cond = 1 + nb_l
    cvec = jnp.concatenate([c_ctx[None, :], c, jnp.zeros((-n_cond % 8, d), F32)], axis=0)
    mods = _modulation(cvec, w_ada, b_ada)
    cos, sin = _rope_tables(seq_l)
    zero_state = jnp.zeros((nb_c, 2, LRU_WIDTH), F32)

    st_k, st_v, st_ckv, st_kr, st_lru = [], [], [], [], []
    for l in range(depth):
        p = _layer_params(l, w_in, lru_conv_w, lru_conv_b, lru_wa, lru_ba, lru_wx, lru_bx, lru_lam, w_lru_out,
                          w_na_out, cm_dw_w, cm_dw_b, cm_ln_g, cm_ln_b, w_cm_out, mla_q_norm, mla_wq_b,
                          mla_kv_norm, mla_wkv_b, w_mla_out, w_out, moe_w_group, moe_w_expert, moe_w1, moe_w3,
                          moe_w2)
        mod_l = mods[l].reshape(mods.shape[1], 1, mods.shape[2])
        g1 = norm1_g[l].reshape(1, d)
        g2 = norm2_g[l].reshape(1, d)
        z = _in_proj(h, t_all, mod_l, g1, p["w_small"], n_ctx_tiles, tiles_per_lat)

        pa_c, lru_c = _lru_branch(z["lx"], z["ly"], zero_state, p, seq_c, 0, nb_c)
        pc_c = _conformer_branch(z["ga"], z["gg"], p, seq_c, 0, nb_c)
        pb_c, pd_c, ckvn_c = _ctx_attention(z, p, seq_c, nb_c)

        pa_l, _ = _lru_branch(z["lx"], z["ly"], state_lru[:, l], p, seq_l, lat_blk0, nb_l)
        pc_l = _conformer_branch(z["ga"], z["gg"], p, seq_l, lat_blk0, nb_l)
        bias = _na_bias_tables(na_rpb[l], seq_l // GRID_W)
        pb_l = _na_latent(z, cache_na_k[:, l].reshape(nb_l, past, NA_HEADS * NA_HD),
                          cache_na_v[:, l].reshape(nb_l, past, NA_HEADS * NA_HD), bias, seq_l, lat_blk0, nb_l)
        qm, kl, klt = _mla_prep(z, cos, sin, p, seq_l, lat_blk0, nb_l)
        kc = jnp.concatenate([cache_mla_ckv[:, l], cache_mla_krope[:, l],
                              jnp.zeros((nb_l, past, MLA_KDIM - MLA_KV_LORA - MLA_ROPE), F32)], axis=-1).astype(BF16)
        kct = jnp.swapaxes(cache_mla_ckv[:, l], 1, 2).astype(BF16)
        pd_l = _mla_latent(qm, kl, klt, kc, kct, p["mla_wvt"], seq_l, nb_l)

        branches = [_TwoSource(br_c, br_l, n_ctx_tiles, 0)
                    for br_c, br_l in ((pa_c, pa_l), (pb_c, pb_l), (pc_c, pc_l), (pd_c, pd_l))]
        h_mid = _merge(h, t_all, mod_l, g1, branches, p, n_ctx_tiles, tiles_per_lat)
        if l + 1 < depth:
            h_new, = _moe(h_mid, mod_l, g2, p, n_ctx_tiles, tiles_per_lat)
            h = _TwoSource(h_new, h_new, n_ctx_tiles, n_ctx_tiles)
        else:
            y_prompt, y_sample = _moe(h_mid, mod_l, g2, p, n_ctx_tiles, tiles_per_lat, final_g.reshape(1, d))

        st_k.append(z["k"][:t_ctx].reshape(nb_c, seq_c, NA_HEADS, NA_HD))
        st_v.append(z["v"][:t_ctx].reshape(nb_c, seq_c, NA_HEADS, NA_HD))
        st_ckv.append(ckvn_c.reshape(nb_c, seq_c, MLA_KV_LORA))
        st_kr.append(z["krp"][:t_ctx, :MLA_ROPE].reshape(nb_c, seq_c, MLA_ROPE))
        st_lru.append(lru_c)

    y_prompt = y_prompt.reshape(nb_c, seq_c, d)
    y_sample = y_sample.reshape(nb_l, seq_l, d)
    return (y_prompt, y_sample, jnp.stack(st_k, axis=1), jnp.stack(st_v, axis=1), jnp.stack(st_ckv, axis=1),
            jnp.stack(st_kr, axis=1), jnp.stack(st_lru, axis=1))
```

```python
import functools
import math

import numpy as np
import jax
import jax.numpy as jnp
from jax import lax
from jax.experimental import pallas as pl
from jax.experimental.pallas import tpu as pltpu

F32 = jnp.float32
BF16 = jnp.bfloat16

D_MODEL = 1024
GRID_W = 64
EPS = 1e-6
NEG = -1e30
N_BRANCH = 4
LRU_WIDTH = 256
LRU_BLOCKS = 4
LRU_CONV = 4
LRU_C = 8.0
NA_HEADS = 4
NA_HD = 64
NA_WR = 8
NA_WC = 16
NA_QROWS = 4
NA_KROWS = NA_QROWS + NA_WR
CM_WIDTH = 256
CM_KERNEL = 31
MLA_HEADS = 4
MLA_Q_LORA = 256
MLA_KV_LORA = 128
MLA_NOPE = 64
MLA_ROPE = 32
MLA_VD = 64
MLA_KDIM = 256
ROPE_BASE = 10000.0
MOE_GROUPS = 4
MOE_PER_GROUP = 4
MOE_EXPERTS = 16
MOE_FF = 256
ROUTER_LANES = 128

TM_TOKEN = 512
TM_MOE = 1024
TQ_MLA = 1024
TK_MLA = 512
SEQ_CHUNK = 256
VMEM_LIMIT = 56 * 1024 * 1024


def _cparams(sem, vmem=VMEM_LIMIT):
    return pltpu.CompilerParams(dimension_semantics=sem, vmem_limit_bytes=vmem)


def _sigmoid(x):
    return 0.5 * jnp.tanh(0.5 * x) + 0.5


def _silu(x):
    return x * _sigmoid(x)


def _gelu_tanh(x):
    c = math.sqrt(2.0 / math.pi)
    return x * (0.5 * (1.0 + jnp.tanh(c * (x + 0.044715 * (x * x * x)))))


def _softplus(x):
    return jnp.maximum(x, 0.0) + jnp.log1p(jnp.exp(-jnp.abs(x)))


def _rms(x, g):
    return x * lax.rsqrt(jnp.mean(x * x, axis=-1, keepdims=True) + EPS) * g


def _norm_mod(h, g, scale, shift):
    return _rms(h, g) * (1.0 + scale) + shift


def _dot(a, b):
    return jnp.dot(a, b, preferred_element_type=F32)


def _dot_nt(a, b):
    return lax.dot_general(a, b, (((1,), (1,)), ((), ())), preferred_element_type=F32)


def _mod_kernel(c_ref, w_ref, b_ref, o_ref):
    s = _silu(c_ref[...])
    o_ref[0] = jnp.dot(s, w_ref[0], preferred_element_type=F32, precision=lax.Precision.HIGHEST) + b_ref[0]


def _modulation(cvec, w_ada, b_ada):
    depth, d, n = w_ada.shape
    rows = cvec.shape[0]
    tn = 1024
    return pl.pallas_call(
        _mod_kernel,
        grid=(depth, n // tn),
        in_specs=[pl.BlockSpec((rows, d), lambda l, j: (0, 0)),
                  pl.BlockSpec((1, d, tn), lambda l, j: (l, 0, j)),
                  pl.BlockSpec((1, 1, tn), lambda l, j: (l, 0, j))],
        out_specs=pl.BlockSpec((1, rows, tn), lambda l, j: (l, 0, j)),
        out_shape=jax.ShapeDtypeStruct((depth, rows, n), F32),
        compiler_params=_cparams(("parallel", "parallel")),
        name="modulation",
    )(cvec, w_ada, b_ada.reshape(depth, 1, n))


def _mod_row_map(n_ctx_tiles, tiles_per_lat):
    def index_map(i, *_):
        row = jnp.maximum(i - n_ctx_tiles, 0) // tiles_per_lat + (i >= n_ctx_tiles).astype(jnp.int32)
        return (row, 0, 0)
    return index_map


_IN_SEGS = (("lx", 256), ("ly", 256), ("q", 256), ("k", 256), ("v", 256), ("ga", 256), ("gg", 256),
            ("cq", 256), ("ckv", 128), ("krp", 128), ("krs", 128))
_IN_DTYPES = {"q": BF16}


class _TwoSource:
    def __init__(self, a, b, n_a, b_tile0):
        self.a, self.b, self.n_a, self.b_tile0 = a, b, n_a, b_tile0

    def specs(self, tm):
        n_a, b0 = self.n_a, self.b_tile0
        width = self.a.shape[1]
        return [pl.BlockSpec((tm, width), lambda i: (jnp.minimum(i, n_a - 1), 0)),
                pl.BlockSpec((tm, width), lambda i: (jnp.maximum(i - n_a, 0) + b0, 0))]


def _pick(n_a, a_ref, b_ref):
    return jnp.where(pl.program_id(0) < n_a, a_ref[...], b_ref[...])


def _in_kernel(n_a, ha_ref, hb_ref, mod_ref, g_ref, w_ref, *out_refs):
    mod = mod_ref[0]
    h = _pick(n_a, ha_ref, hb_ref)
    u = _norm_mod(h, g_ref[...], mod[:, D_MODEL:2 * D_MODEL], mod[:, 0:D_MODEL]).astype(BF16)
    off = 0
    for (_, width), o_ref in zip(_IN_SEGS, out_refs):
        o_ref[...] = _dot(u, w_ref[:, off:off + width]).astype(o_ref.dtype)
        off += width


def _in_proj(h, t, mods, g, w_small, n_ctx_tiles, tiles_per_lat):
    d = D_MODEL
    tm = TM_TOKEN
    n = w_small.shape[1]
    out_shape = [jax.ShapeDtypeStruct((t, width), _IN_DTYPES.get(name, F32)) for name, width in _IN_SEGS]
    out_specs = [pl.BlockSpec((tm, width), lambda i: (i, 0)) for _, width in _IN_SEGS]
    outs = pl.pallas_call(
        functools.partial(_in_kernel, h.n_a),
        grid=(t // tm,),
        in_specs=h.specs(tm) + [
            pl.BlockSpec((1, 1, mods.shape[-1]), _mod_row_map(n_ctx_tiles, tiles_per_lat)),
            pl.BlockSpec((1, d), lambda i: (0, 0)),
            pl.BlockSpec((d, n), lambda i: (0, 0))],
        out_specs=out_specs,
        out_shape=out_shape,
        compiler_params=_cparams(("parallel",)),
        name="in_proj",
    )(h.a, h.b, mods, g, w_small)
    return dict(zip([s for s, _ in _IN_SEGS], outs))


def _shifted_windows(xw, offsets, length):
    n = xw.shape[0]
    rolled = {0: xw}
    out = {}
    for o in offsets:
        r = o % 8
        if r not in rolled:
            rolled[r] = pltpu.roll(xw, n - r, 0)
        base = o - r
        out[o] = rolled[r][base:base + length]
    return out


def _lru_kernel(lx_ref, ly_ref, h0_ref, cw_ref, cb_ref, wa_ref, ba_ref, wx_ref, bx_ref, lam_ref,
                pa_ref, st_ref, pad_ref, af_ref, bf_ref, ab_ref, bb_ref):
    t = lx_ref.shape[0]
    ch = min(SEQ_CHUNK, t)
    halo = 8
    zeros = jnp.zeros((halo, LRU_WIDTH), F32)
    pad_ref[0:halo, :] = zeros
    pad_ref[halo + t:2 * halo + t, :] = zeros
    pad_ref[halo:halo + t, :] = lx_ref[...]
    pad_l = LRU_CONV // 2
    taps = [halo - pad_l + k for k in range(LRU_CONV)]
    a_refs = (af_ref, ab_ref)
    b_refs = (bf_ref, bb_ref)

    def gates(c, carry):
        r0 = pl.multiple_of(c * ch, ch)
        xw = pad_ref[pl.ds(r0, ch + 2 * halo), :]
        win = _shifted_windows(xw, taps, ch)
        xc = cb_ref[...] + sum(cw_ref[k:k + 1, :] * win[taps[k]] for k in range(LRU_CONV))
        xcb = xc.astype(BF16)
        for d in range(2):
            r = _sigmoid(_dot(xcb, wa_ref[d]) + ba_ref[d])
            i = _sigmoid(_dot(xcb, wx_ref[d]) + bx_ref[d])
            log_a = (-LRU_C) * r * _softplus(-lam_ref[d])
            a = jnp.exp(log_a)
            one_minus_a2 = -jnp.tanh(log_a) * (a * a + 1.0)
            a_refs[d][pl.ds(r0, ch), :] = a
            b_refs[d][pl.ds(r0, ch), :] = jnp.sqrt(one_minus_a2) * (i * xc)
        return carry

    lax.fori_loop(0, t // ch, gates, 0)

    sub = 8
    row = lax.broadcasted_iota(jnp.int32, (sub, LRU_WIDTH), 0)

    def bcast(x, j):
        return jnp.broadcast_to(x[j:j + 1, :], (sub, LRU_WIDTH))

    def scan(c, carry):
        hf, hb = carry
        rf = pl.multiple_of(c * sub, sub)
        rb = pl.multiple_of(t - sub - c * sub, sub)
        a_f, b_f = af_ref[pl.ds(rf, sub), :], bf_ref[pl.ds(rf, sub), :]
        a_b, b_b = ab_ref[pl.ds(rb, sub), :], bb_ref[pl.ds(rb, sub), :]
        out_f = jnp.zeros((sub, LRU_WIDTH), F32)
        out_b = jnp.zeros((sub, LRU_WIDTH), F32)
        for j in range(sub):
            hf = bcast(a_f, j) * hf + bcast(b_f, j)
            out_f = jnp.where(row == j, hf, out_f)
            jb = sub - 1 - j
            hb = bcast(a_b, jb) * hb + bcast(b_b, jb)
            out_b = jnp.where(row == jb, hb, out_b)
        bf_ref[pl.ds(rf, sub), :] = out_f
        bb_ref[pl.ds(rb, sub), :] = out_b
        return hf, hb

    h0 = (jnp.broadcast_to(h0_ref[0, 0:1, :], (sub, LRU_WIDTH)), jnp.broadcast_to(h0_ref[0, 1:2, :], (sub, LRU_WIDTH)))
    hf, hb = lax.fori_loop(0, t // sub, scan, h0)
    st_ref[0, 0:1, :] = hf[0:1, :]
    st_ref[0, 1:2, :] = hb[0:1, :]

    def emit(c, carry):
        r0 = pl.multiple_of(c * ch, ch)
        y = (bf_ref[pl.ds(r0, ch), :] + bb_ref[pl.ds(r0, ch), :]) * _gelu_tanh(ly_ref[pl.ds(r0, ch), :])
        pa_ref[pl.ds(r0, ch), :] = y.astype(pa_ref.dtype)
        return carry

    lax.fori_loop(0, t // ch, emit, 0)


def _lru_branch(lx, ly, h0, p, seq, blk0, nb):
    w = LRU_WIDTH
    const2 = lambda b: (0, 0)
    const3 = lambda b: (0, 0, 0)
    return pl.pallas_call(
        _lru_kernel,
        grid=(nb,),
        in_specs=[pl.BlockSpec((seq, w), lambda b: (blk0 + b, 0)),
                  pl.BlockSpec((seq, w), lambda b: (blk0 + b, 0)),
                  pl.BlockSpec((1, 2, w), lambda b: (b, 0, 0)),
                  pl.BlockSpec((LRU_CONV, w), const2),
                  pl.BlockSpec((1, w), const2),
                  pl.BlockSpec((2, w, w), const3),
                  pl.BlockSpec((2, 1, w), const3),
                  pl.BlockSpec((2, w, w), const3),
                  pl.BlockSpec((2, 1, w), const3),
                  pl.BlockSpec((2, 1, w), const3)],
        out_specs=[pl.BlockSpec((seq, w), lambda b: (b, 0)),
                   pl.BlockSpec((1, 2, w), lambda b: (b, 0, 0))],
        out_shape=[jax.ShapeDtypeStruct((nb * seq, w), BF16),
                   jax.ShapeDtypeStruct((nb, 2, w), F32)],
        scratch_shapes=[pltpu.VMEM((seq + 16, w), F32)] + [pltpu.VMEM((seq, w), F32)] * 4,
        compiler_params=_cparams(("parallel",)),
        name="rglru",
    )(lx, ly, h0, p["lru_conv_w"], p["lru_conv_b"], p["lru_wa"], p["lru_ba"], p["lru_wx"], p["lru_bx"],
      p["lru_lam"])


def _cm_kernel(ga_ref, gg_ref, w_ref, b_ref, lg_ref, lb_ref, pc_ref, pad_ref):
    t = ga_ref.shape[0]
    ch = min(SEQ_CHUNK, t)
    halo = 16
    zeros = jnp.zeros((halo, CM_WIDTH), F32)
    pad_ref[0:halo, :] = zeros
    pad_ref[halo + t:2 * halo + t, :] = zeros

    def glu(c, carry):
        r0 = pl.multiple_of(c * ch, ch)
        pad_ref[pl.ds(halo + r0, ch), :] = ga_ref[pl.ds(r0, ch), :] * _sigmoid(gg_ref[pl.ds(r0, ch), :])
        return carry

    lax.fori_loop(0, t // ch, glu, 0)
    taps = [halo - CM_KERNEL // 2 + k for k in range(CM_KERNEL)]

    def conv(c, carry):
        r0 = pl.multiple_of(c * ch, ch)
        xw = pad_ref[pl.ds(r0, ch + 2 * halo), :]
        win = _shifted_windows(xw, taps, ch)
        z = b_ref[...] + sum(w_ref[k:k + 1, :] * win[taps[k]] for k in range(CM_KERNEL))
        mu = jnp.mean(z, axis=-1, keepdims=True)
        zc = z - mu
        var = jnp.mean(zc * zc, axis=-1, keepdims=True)
        y = zc * lax.rsqrt(var + EPS) * lg_ref[...] + lb_ref[...]
        pc_ref[pl.ds(r0, ch), :] = _silu(y).astype(pc_ref.dtype)
        return carry

    lax.fori_loop(0, t // ch, conv, 0)


def _conformer_branch(ga, gg, p, seq, blk0, nb):
    w = CM_WIDTH
    const2 = lambda b: (0, 0)
    return pl.pallas_call(
        _cm_kernel,
        grid=(nb,),
        in_specs=[pl.BlockSpec((seq, w), lambda b: (blk0 + b, 0)),
                  pl.BlockSpec((seq, w), lambda b: (blk0 + b, 0)),
                  pl.BlockSpec((CM_KERNEL, w), const2),
                  pl.BlockSpec((1, w), const2),
                  pl.BlockSpec((1, w), const2),
                  pl.BlockSpec((1, w), const2)],
        out_specs=pl.BlockSpec((seq, w), lambda b: (b, 0)),
        out_shape=jax.ShapeDtypeStruct((nb * seq, w), BF16),
        scratch_shapes=[pltpu.VMEM((seq + 32, w), F32)],
        compiler_params=_cparams(("parallel",)),
        name="conformer",
    )(ga, gg, p["cm_dw_w"], p["cm_dw_b"], p["cm_ln_g"], p["cm_ln_b"])


def _softmax_pv(scores, values):
    m = functools.reduce(jnp.maximum, [jnp.max(s, axis=-1, keepdims=True) for s in scores])
    ps = [jnp.exp(s - m) for s in scores]
    l = sum(jnp.sum(p, axis=-1, keepdims=True) for p in ps)
    o = sum(_dot(p.astype(BF16), v) for p, v in zip(ps, values))
    return o / l


def _ctx_attn_kernel(q_ref, k_ref, v_ref, cq_ref, ckv_ref, krp_ref, qn_ref, wq_ref, kvn_ref, wkv_ref,
                     pb_ref, pd_ref, ckvn_ref):
    q = q_ref[...]
    k = k_ref[...].astype(BF16)
    v = v_ref[...].astype(BF16)
    na_scale = NA_HD ** -0.5
    for h in range(NA_HEADS):
        sl = slice(h * NA_HD, (h + 1) * NA_HD)
        s = _dot_nt(q[:, sl], k[:, sl]) * na_scale
        pb_ref[:, sl] = _softmax_pv([s], [v[:, sl]]).astype(pb_ref.dtype)

    ckv_n = _rms(ckv_ref[...], kvn_ref[...])
    ckvn_ref[...] = ckv_n
    mla_scale = (MLA_NOPE + MLA_ROPE) ** -0.5
    qf = _dot(_rms(cq_ref[...], qn_ref[...]).astype(BF16), wq_ref[...]).astype(BF16)
    kv = _dot(ckv_n.astype(BF16), wkv_ref[...]).astype(BF16)
    kr = krp_ref[:, 0:MLA_ROPE].astype(BF16)
    qd = MLA_NOPE + MLA_ROPE
    kd = MLA_NOPE + MLA_VD
    for h in range(MLA_HEADS):
        qn = qf[:, h * qd:h * qd + MLA_NOPE]
        qr = qf[:, h * qd + MLA_NOPE:(h + 1) * qd]
        kn = kv[:, h * kd:h * kd + MLA_NOPE]
        vm = kv[:, h * kd + MLA_NOPE:(h + 1) * kd]
        s = (_dot_nt(qn, kn) + _dot_nt(qr, kr)) * mla_scale
        pd_ref[:, h * MLA_VD:(h + 1) * MLA_VD] = _softmax_pv([s], [vm]).astype(pd_ref.dtype)


def _ctx_attention(z, p, seq, nb):
    const2 = lambda b: (0, 0)
    row = lambda b: (b, 0)
    return pl.pallas_call(
        _ctx_attn_kernel,
        grid=(nb,),
        in_specs=[pl.BlockSpec((seq, 256), row), pl.BlockSpec((seq, 256), row), pl.BlockSpec((seq, 256), row),
                  pl.BlockSpec((seq, MLA_Q_LORA), row), pl.BlockSpec((seq, MLA_KV_LORA), row),
                  pl.BlockSpec((seq, 128), row),
                  pl.BlockSpec((1, MLA_Q_LORA), const2),
                  pl.BlockSpec(p["mla_wq_b"].shape, const2),
                  pl.BlockSpec((1, MLA_KV_LORA), const2),
                  pl.BlockSpec(p["mla_wkv_b"].shape, const2)],
        out_specs=[pl.BlockSpec((seq, 256), row), pl.BlockSpec((seq, 256), row),
                   pl.BlockSpec((seq, MLA_KV_LORA), row)],
        out_shape=[jax.ShapeDtypeStruct((nb * seq, 256), BF16),
                   jax.ShapeDtypeStruct((nb * seq, 256), BF16),
                   jax.ShapeDtypeStruct((nb * seq, MLA_KV_LORA), F32)],
        compiler_params=_cparams(("parallel",)),
        name="ctx_attention",
    )(z["q"], z["k"], z["v"], z["cq"], z["ckv"], z["krp"], p["mla_q_norm"], p["mla_wq_b"],
      p["mla_kv_norm"], p["mla_wkv_b"])


def _na_bias_tables(rpb, rows):
    heads, n_rel_r, n_rel_c = rpb.shape
    w = GRID_W
    u = jnp.concatenate([rpb[:, :, NA_WC - 1:], jnp.zeros((heads, n_rel_r, 2 * w - n_rel_c), rpb.dtype),
                         rpb[:, :, :NA_WC - 1]], axis=-1)
    toep = jnp.tile(u, (1, 1, w))[:, :, :w * (2 * w - 1)].reshape(heads, n_rel_r, w, 2 * w - 1)[..., :w]
    qc = np.arange(w)
    cs = np.clip(qc - NA_WC // 2, 0, w - NA_WC)
    col_ok = (qc[None, :] >= cs[:, None]) & (qc[None, :] < cs[:, None] + NA_WC)
    toep = jnp.where(col_ok, toep, NEG)
    masked = jnp.full((heads, w, w), NEG, rpb.dtype)
    n_steps = rows // NA_QROWS
    tables = []
    for step in (0, 1, n_steps - 1):
        start = int(np.clip(NA_QROWS * step - NA_WR // 2, 0, rows - NA_KROWS))
        block_rows = []
        for a in range(NA_QROWS):
            qr = NA_QROWS * step + a
            rs = int(np.clip(qr - NA_WR // 2, 0, rows - NA_WR))
            blocks = []
            for b in range(NA_KROWS):
                kr = start + b
                blocks.append(toep[:, kr - qr + NA_WR - 1] if rs <= kr < rs + NA_WR else masked)
            block_rows.append(jnp.concatenate(blocks, axis=-1))
        tables.append(jnp.concatenate(block_rows, axis=1))
    return jnp.stack(tables, axis=0)


def _na_lat_kernel(q_ref, k_ref, v_ref, ck_ref, cv_ref, bias_ref, pb_ref):
    i = pl.program_id(1)
    rows = k_ref.shape[0] // GRID_W
    start = jnp.clip(NA_QROWS * i - NA_WR // 2, 0, rows - NA_KROWS) * GRID_W
    start = pl.multiple_of(start, GRID_W)
    nk = NA_KROWS * GRID_W
    q = q_ref[...]
    kw = k_ref[pl.ds(start, nk), :].astype(BF16)
    vw = v_ref[pl.ds(start, nk), :].astype(BF16)
    ck = ck_ref[0].astype(BF16)
    cv = cv_ref[0].astype(BF16)
    scale = NA_HD ** -0.5
    for h in range(NA_HEADS):
        sl = slice(h * NA_HD, (h + 1) * NA_HD)
        s_loc = _dot_nt(q[:, sl], kw[:, sl]) * scale + bias_ref[0, h]
        s_ctx = _dot_nt(q[:, sl], ck[:, sl]) * scale
        pb_ref[:, sl] = _softmax_pv([s_loc, s_ctx], [vw[:, sl], cv[:, sl]]).astype(pb_ref.dtype)


def _na_latent(z, cache_k, cache_v, bias, seq, blk0, nb):
    tq = NA_QROWS * GRID_W
    steps = seq // tq
    nk = NA_KROWS * GRID_W
    past = cache_k.shape[1]

    def bias_map(b, i):
        return (jnp.where(i == 0, 0, jnp.where(i == steps - 1, 2, 1)), 0, 0, 0)

    return pl.pallas_call(
        _na_lat_kernel,
        grid=(nb, steps),
        in_specs=[pl.BlockSpec((tq, 256), lambda b, i: (blk0 * steps + b * steps + i, 0)),
                  pl.BlockSpec((seq, 256), lambda b, i: (blk0 + b, 0)),
                  pl.BlockSpec((seq, 256), lambda b, i: (blk0 + b, 0)),
                  pl.BlockSpec((1, past, 256), lambda b, i: (b, 0, 0)),
                  pl.BlockSpec((1, past, 256), lambda b, i: (b, 0, 0)),
                  pl.BlockSpec((1, NA_HEADS, tq, nk), bias_map)],
        out_specs=pl.BlockSpec((tq, 256), lambda b, i: (b * steps + i, 0)),
        out_shape=jax.ShapeDtypeStruct((nb * seq, 256), BF16),
        compiler_params=_cparams(("parallel", "arbitrary")),
        name="na_latent",
    )(z["q"], z["k"], z["v"], cache_k, cache_v, bias)


def _rope_tables(seq):
    t = np.arange(seq)
    pos = np.stack([t // GRID_W, t % GRID_W], axis=-1).astype(np.float32)
    n_freq = MLA_ROPE // 4
    inv = jnp.asarray(ROPE_BASE, F32) ** (-jnp.arange(n_freq, dtype=F32) / n_freq)
    ang = jnp.asarray(pos)[:, :, None] * inv
    cos, sin = jnp.cos(ang), jnp.sin(ang)
    c = jnp.stack([cos, cos], axis=2).reshape(seq, MLA_ROPE)
    s = jnp.stack([-sin, sin], axis=2).reshape(seq, MLA_ROPE)
    pad = ((0, 0), (0, 128 - MLA_ROPE))
    return jnp.pad(c, pad), jnp.pad(s, pad)


def _swap_rope_halves(w):
    w4 = w.reshape(w.shape[:-1] + (2, 2, MLA_ROPE // 4))
    return w4[..., ::-1, :].reshape(w.shape)


def _mla_prep_kernel(cq_ref, ckv_ref, krp_ref, krs_ref, cos_ref, sin_ref, qn_ref, wqn_ref, wqr_ref, wqs_ref,
                     kvn_ref, wkt_ref, qm_ref, kl_ref, klt_ref):
    tq = cq_ref.shape[0]
    cos = cos_ref[...]
    sin = sin_ref[...]
    ckv_n = _rms(ckv_ref[...], kvn_ref[...])
    kl_ref[0, :, 0:128] = ckv_n.astype(kl_ref.dtype)
    kl_ref[0, :, 128:256] = (krp_ref[...] * cos + krs_ref[...] * sin).astype(kl_ref.dtype)
    klt_ref[0] = ckv_n.T.astype(klt_ref.dtype)
    cqn = _rms(cq_ref[...], qn_ref[...]).astype(BF16)
    qn = _dot(cqn, wqn_ref[...]).astype(BF16)
    scale = (MLA_NOPE + MLA_ROPE) ** -0.5 * math.log2(math.e)
    for h in range(MLA_HEADS):
        qa = _dot(qn[:, h * MLA_NOPE:(h + 1) * MLA_NOPE], wkt_ref[h])
        qr = _dot(cqn, wqr_ref[h]) * cos + _dot(cqn, wqs_ref[h]) * sin
        qm_ref[0, 0, 0:128, h * tq:(h + 1) * tq] = (qa * scale).T.astype(qm_ref.dtype)
        qm_ref[0, 0, 128:256, h * tq:(h + 1) * tq] = (qr * scale).T.astype(qm_ref.dtype)


def _mla_prep(z, cos, sin, p, seq, blk0, nb):
    tq = TQ_MLA
    steps = seq // tq
    tok = lambda b, i: (blk0 * steps + b * steps + i, 0)
    pos = lambda b, i: (i, 0)
    const2 = lambda b, i: (0, 0)
    const3 = lambda b, i: (0, 0, 0)
    return pl.pallas_call(
        _mla_prep_kernel,
        grid=(nb, steps),
        in_specs=[pl.BlockSpec((tq, MLA_Q_LORA), tok), pl.BlockSpec((tq, MLA_KV_LORA), tok),
                  pl.BlockSpec((tq, 128), tok), pl.BlockSpec((tq, 128), tok),
                  pl.BlockSpec((tq, 128), pos), pl.BlockSpec((tq, 128), pos),
                  pl.BlockSpec((1, MLA_Q_LORA), const2),
                  pl.BlockSpec((MLA_Q_LORA, MLA_HEADS * MLA_NOPE), const2),
                  pl.BlockSpec((MLA_HEADS, MLA_Q_LORA, 128), const3),
                  pl.BlockSpec((MLA_HEADS, MLA_Q_LORA, 128), const3),
                  pl.BlockSpec((1, MLA_KV_LORA), const2),
                  pl.BlockSpec((MLA_HEADS, MLA_NOPE, MLA_KV_LORA), const3)],
        out_specs=[pl.BlockSpec((1, 1, MLA_KDIM, MLA_HEADS * tq), lambda b, i: (b, i, 0, 0)),
                   pl.BlockSpec((1, tq, MLA_KDIM), lambda b, i: (b, i, 0)),
                   pl.BlockSpec((1, MLA_KV_LORA, tq), lambda b, i: (b, 0, i))],
        out_shape=[jax.ShapeDtypeStruct((nb, steps, MLA_KDIM, MLA_HEADS * tq), BF16),
                   jax.ShapeDtypeStruct((nb, seq, MLA_KDIM), BF16),
                   jax.ShapeDtypeStruct((nb, MLA_KV_LORA, seq), BF16)],
        compiler_params=_cparams(("parallel", "parallel")),
        name="mla_prep",
    )(z["cq"], z["ckv"], z["krp"], z["krs"], cos, sin, p["mla_q_norm"], p["mla_wqn"], p["mla_wqr"],
      p["mla_wqs"], p["mla_kv_norm"], p["mla_wkt"])


def _mla_lat_kernel(qt_ref, kl_ref, klt_ref, kc_ref, kct_ref, wvt_ref, pd_ref, m_ref, l_ref, acc_ref,
                    s_buf, p_buf, a_buf):
    qt = qt_ref[0, 0]
    cols = qt.shape[1]
    tq = cols // MLA_HEADS
    n_chunks = kl_ref.shape[1] // TK_MLA
    m_ref[...] = jnp.full((1, cols), NEG, F32)
    l_ref[...] = jnp.zeros((1, cols), F32)
    acc_ref[...] = jnp.zeros((MLA_KV_LORA, cols), F32)

    def softmax_stats(s):
        m_old = m_ref[...]
        m_new = jnp.maximum(m_old, jnp.max(s, axis=0, keepdims=True))
        alpha = jnp.exp2(m_old - m_new)
        p = jnp.exp2(s - m_new)
        l_ref[...] = alpha * l_ref[...] + jnp.sum(p, axis=0, keepdims=True)
        m_ref[...] = m_new
        return alpha, p.astype(BF16)

    alpha, p = softmax_stats(_dot(kc_ref[0], qt))
    acc_ref[...] = alpha * acc_ref[...] + _dot(kct_ref[0], p)

    def chunk(c):
        return pl.ds(pl.multiple_of(c * TK_MLA, TK_MLA), TK_MLA)

    def scores(c, slot):
        s_buf[slot] = _dot(kl_ref[0, chunk(c), :], qt)

    def softmax(slot):
        alpha, p = softmax_stats(s_buf[slot])
        a_buf[slot] = alpha
        p_buf[slot] = p

    def values(c, slot):
        acc_ref[...] = a_buf[slot] * acc_ref[...] + _dot(klt_ref[0, :, chunk(c)], p_buf[slot])

    scores(0, 0)
    scores(1, 1)
    softmax(0)

    def body(j, carry):
        c = 2 * j
        scores(c + 2, 0)
        softmax(1)
        values(c, 0)
        scores(c + 3, 1)
        softmax(0)
        values(c + 1, 1)
        return carry

    lax.fori_loop(0, (n_chunks - 2) // 2, body, 0)
    softmax(1)
    values(n_chunks - 2, 0)
    values(n_chunks - 1, 1)
    o = (acc_ref[...] / l_ref[...]).astype(BF16)
    out_t = jnp.concatenate([_dot(wvt_ref[h], o[:, h * tq:(h + 1) * tq]) for h in range(MLA_HEADS)], axis=0)
    pd_ref[...] = out_t.T.astype(pd_ref.dtype)


def _mla_latent(qm, kl, klt, kc, kct, wvt, seq, nb):
    tq = TQ_MLA
    steps = seq // tq
    cols = MLA_HEADS * tq
    past = kc.shape[1]
    return pl.pallas_call(
        _mla_lat_kernel,
        grid=(nb, steps),
        in_specs=[pl.BlockSpec((1, 1, MLA_KDIM, cols), lambda b, i: (b, i, 0, 0)),
                  pl.BlockSpec((1, seq, MLA_KDIM), lambda b, i: (b, 0, 0)),
                  pl.BlockSpec((1, MLA_KV_LORA, seq), lambda b, i: (b, 0, 0)),
                  pl.BlockSpec((1, past, MLA_KDIM), lambda b, i: (b, 0, 0)),
                  pl.BlockSpec((1, MLA_KV_LORA, past), lambda b, i: (b, 0, 0)),
                  pl.BlockSpec((MLA_HEADS, MLA_VD, MLA_KV_LORA), lambda b, i: (0, 0, 0))],
        out_specs=pl.BlockSpec((tq, MLA_HEADS * MLA_VD), lambda b, i: (b * steps + i, 0)),
        out_shape=jax.ShapeDtypeStruct((nb * seq, MLA_HEADS * MLA_VD), BF16),
        scratch_shapes=[pltpu.VMEM((1, cols), F32), pltpu.VMEM((1, cols), F32),
                        pltpu.VMEM((MLA_KV_LORA, cols), F32),
                        pltpu.VMEM((2, TK_MLA, cols), F32), pltpu.VMEM((2, TK_MLA, cols), BF16),
                        pltpu.VMEM((2, 1, cols), F32)],
        compiler_params=_cparams(("parallel", "arbitrary")),
        name="mla_latent",
    )(qm, kl, klt, kc, kct, wvt)


def _merge_kernel(n_a, ha_ref, hb_ref, mod_ref, g_ref, pa_c, pa_l, pb_c, pb_l, pc_c, pc_l, pd_c, pd_l,
                  wg_ref, wb_ref, wo_ref, o_ref):
    mod = mod_ref[0]
    h = _pick(n_a, ha_ref, hb_ref)
    u = _norm_mod(h, g_ref[...], mod[:, D_MODEL:2 * D_MODEL], mod[:, 0:D_MODEL]).astype(BF16)
    y = None
    for j, (c_ref, l_ref) in enumerate(((pa_c, pa_l), (pb_c, pb_l), (pc_c, pc_l), (pd_c, pd_l))):
        gate = _sigmoid(_dot(u, wg_ref[:, j * D_MODEL:(j + 1) * D_MODEL]))
        term = gate * _dot(_pick(n_a, c_ref, l_ref), wb_ref[j])
        y = term if y is None else y + term
    out = _dot(y.astype(BF16), wo_ref[...])
    o_ref[...] = h + mod[:, 2 * D_MODEL:3 * D_MODEL] * out


def _merge(h, t, mods, g, branches, p, n_ctx_tiles, tiles_per_lat):
    d = D_MODEL
    tm = TM_TOKEN
    const2 = lambda i: (0, 0)
    branch_specs, branch_args = [], []
    for br in branches:
        branch_specs += br.specs(tm)
        branch_args += [br.a, br.b]
    return pl.pallas_call(
        functools.partial(_merge_kernel, h.n_a),
        grid=(t // tm,),
        in_specs=h.specs(tm) + [
            pl.BlockSpec((1, 1, mods.shape[-1]), _mod_row_map(n_ctx_tiles, tiles_per_lat)),
            pl.BlockSpec((1, d), const2)] + branch_specs + [
            pl.BlockSpec((d, N_BRANCH * d), const2),
            pl.BlockSpec((N_BRANCH, 256, d), lambda i: (0, 0, 0)),
            pl.BlockSpec((d, d), const2)],
        out_specs=pl.BlockSpec((tm, d), lambda i: (i, 0)),
        out_shape=jax.ShapeDtypeStruct((t, d), F32),
        compiler_params=_cparams(("parallel",)),
        name="merge",
    )(h.a, h.b, mods, g, *branch_args, p["w_gates"], p["w_branch_out"], p["w_out"])


def _router_gates(logits):
    lane_i = lax.broadcasted_iota(jnp.int32, logits.shape, 1)
    lane = lane_i.astype(F32)
    ninf = jnp.float32(-jnp.inf)

    def first_argmax(x):
        m = jnp.max(x, axis=-1, keepdims=True)
        idx = jnp.min(jnp.where(x == m, lane, jnp.float32(1e9)), axis=-1, keepdims=True)
        return m, idx

    gl = jnp.where(lane_i < MOE_GROUPS, logits, ninf)
    gmax, gsel = first_argmax(gl)
    gp = 1.0 / jnp.sum(jnp.exp(gl - gmax), axis=-1, keepdims=True)
    e_idx = lane_i - MOE_GROUPS
    e_group = lax.shift_right_arithmetic(e_idx, jnp.full_like(e_idx, 2)).astype(F32)
    in_group = (e_idx >= 0) & (e_idx < MOE_EXPERTS) & (e_group == gsel)
    el = jnp.where(in_group, logits, ninf)
    m1, i1 = first_argmax(el)
    m2, i2 = first_argmax(jnp.where(lane == i1, ninf, el))
    e2 = jnp.exp(m2 - m1)
    w1 = gp / (1.0 + e2)
    w2 = gp * e2 / (1.0 + e2)
    return jnp.where(lane == i1, w1, 0.0) + jnp.where(lane == i2, w2, 0.0)


def _moe_kernel(n_ctx, is_final, h_ref, mod_ref, g_ref, *refs):
    if is_final:
        fg_ref, refs = refs[0], refs[1:]
        n_out = 2
    else:
        fg_ref, n_out = None, 1
    wr_ref, w13_ref, w2_ref = refs[:3]
    out_refs = refs[3:3 + n_out]
    u_ref, gate_ref, acc_ref = refs[3 + n_out:]
    e = pl.program_id(1)
    mod = mod_ref[0]

    @pl.when(e == 0)
    def _():
        u = _norm_mod(h_ref[...], g_ref[...], mod[:, 4 * D_MODEL:5 * D_MODEL], mod[:, 3 * D_MODEL:4 * D_MODEL])
        ub = u.astype(BF16)
        u_ref[...] = ub
        gate_ref[...] = _router_gates(_dot(ub, wr_ref[...]))
        acc_ref[...] = jnp.zeros_like(acc_ref)

    ub = u_ref[...]
    lane = lax.broadcasted_iota(jnp.int32, gate_ref.shape, 1)
    gate = jnp.sum(jnp.where(lane == e + MOE_GROUPS, gate_ref[...], 0.0), axis=-1, keepdims=True)
    h13 = _dot(ub, w13_ref[0])
    hid = _silu(h13[:, 0:MOE_FF]) * h13[:, MOE_FF:2 * MOE_FF] * gate
    acc_ref[...] += _dot(hid.astype(BF16), w2_ref[0])

    @pl.when(e == MOE_EXPERTS - 1)
    def _():
        h_out = h_ref[...] + mod[:, 5 * D_MODEL:6 * D_MODEL] * acc_ref[...]
        if fg_ref is None:
            out_refs[0][...] = h_out
        else:
            y = _rms(h_out, fg_ref[...])
            is_ctx = pl.program_id(0) < n_ctx

            @pl.when(is_ctx)
            def _():
                out_refs[0][...] = y

            @pl.when(jnp.logical_not(is_ctx))
            def _():
                out_refs[1][...] = y


def _moe(h, mods, g, p, n_ctx_tiles, tiles_per_lat, final_g=None):
    t, d = h.shape
    tm = TM_MOE
    nct = n_ctx_tiles * TM_TOKEN // tm
    tpl = tiles_per_lat * TM_TOKEN // tm
    in_specs = [pl.BlockSpec((tm, d), lambda i, e: (i, 0)),
                pl.BlockSpec((1, 1, mods.shape[-1]), _mod_row_map(nct, tpl)),
                pl.BlockSpec((1, d), lambda i, e: (0, 0))]
    args = [h, mods, g]
    if final_g is None:
        out_specs = [pl.BlockSpec((tm, d), lambda i, e: (i, 0))]
        out_shape = [jax.ShapeDtypeStruct((t, d), F32)]
    else:
        in_specs.append(pl.BlockSpec((1, d), lambda i, e: (0, 0)))
        args.append(final_g)
        out_specs = [pl.BlockSpec((tm, d), lambda i, e: (jnp.minimum(i, nct - 1), 0)),
                     pl.BlockSpec((tm, d), lambda i, e: (jnp.maximum(i - nct, 0), 0))]
        out_shape = [jax.ShapeDtypeStruct((nct * tm, d), F32), jax.ShapeDtypeStruct((t - nct * tm, d), F32)]
    in_specs += [pl.BlockSpec((d, ROUTER_LANES), lambda i, e: (0, 0)),
                 pl.BlockSpec((1, d, 2 * MOE_FF), lambda i, e: (e, 0, 0)),
                 pl.BlockSpec((1, MOE_FF, d), lambda i, e: (e, 0, 0))]
    args += [p["w_router"], p["w13"], p["w2"]]
    return pl.pallas_call(
        functools.partial(_moe_kernel, nct, final_g is not None),
        grid=(t // tm, MOE_EXPERTS),
        in_specs=in_specs,
        out_specs=out_specs,
        out_shape=out_shape,
        scratch_shapes=[pltpu.VMEM((tm, d), BF16), pltpu.VMEM((tm, ROUTER_LANES), F32),
                        pltpu.VMEM((tm, d), F32)],
        compiler_params=_cparams(("arbitrary", "arbitrary")),
        name="moe",
    )(*args)


def _block_diag(w):
    nd, nb, bw, _ = w.shape
    eye = jnp.eye(nb, dtype=w.dtype)
    return jnp.einsum("dnij,nm->dnimj", w, eye).reshape(nd, nb * bw, nb * bw)


def _layer_params(l, w_in, lru_conv_w, lru_conv_b, lru_wa, lru_ba, lru_wx, lru_bx, lru_lam, w_lru_out, w_na_out,
                  cm_dw_w, cm_dw_b, cm_ln_g, cm_ln_b, w_cm_out, mla_q_norm, mla_wq_b, mla_kv_norm, mla_wkv_b,
                  w_mla_out, w_out, moe_w_group, moe_w_expert, moe_w1, moe_w3, moe_w2):
    wi = w_in[l]
    n_small = 2 * LRU_WIDTH + 3 * NA_HEADS * NA_HD + 2 * CM_WIDTH + MLA_Q_LORA + MLA_KV_LORA
    kr_cols = wi[:, n_small:n_small + MLA_ROPE]
    zpad = jnp.zeros((D_MODEL, 128 - MLA_ROPE), wi.dtype)
    w_small = jnp.concatenate([wi[:, :n_small], kr_cols, zpad, _swap_rope_halves(kr_cols), zpad], axis=1)
    w_gates = wi[:, n_small + MLA_ROPE:]

    qd = MLA_NOPE + MLA_ROPE
    wq = mla_wq_b[l].reshape(MLA_Q_LORA, MLA_HEADS, qd)
    wqn = wq[:, :, :MLA_NOPE].reshape(MLA_Q_LORA, MLA_HEADS * MLA_NOPE)
    wqr = jnp.moveaxis(wq[:, :, MLA_NOPE:], 1, 0)
    rpad = ((0, 0), (0, 0), (0, 128 - MLA_ROPE))
    wkv = mla_wkv_b[l].reshape(MLA_KV_LORA, MLA_HEADS, MLA_NOPE + MLA_VD)
    wkt = jnp.transpose(wkv[:, :, :MLA_NOPE], (1, 2, 0))
    wvt = jnp.transpose(wkv[:, :, MLA_NOPE:], (1, 2, 0))

    router = jnp.concatenate([moe_w_group[l], moe_w_expert[l]], axis=1)
    router = jnp.pad(router, ((0, 0), (0, ROUTER_LANES - router.shape[1])))
    row = lambda a: a.reshape(1, -1)
    return dict(
        w_small=w_small.astype(BF16), w_gates=w_gates.astype(BF16),
        lru_conv_w=lru_conv_w[l], lru_conv_b=row(lru_conv_b[l]),
        lru_wa=_block_diag(lru_wa[l]).astype(BF16), lru_ba=lru_ba[l][:, None, :],
        lru_wx=_block_diag(lru_wx[l]).astype(BF16), lru_bx=lru_bx[l][:, None, :],
        lru_lam=lru_lam[l][:, None, :],
        cm_dw_w=cm_dw_w[l], cm_dw_b=row(cm_dw_b[l]), cm_ln_g=row(cm_ln_g[l]), cm_ln_b=row(cm_ln_b[l]),
        mla_q_norm=row(mla_q_norm[l]), mla_kv_norm=row(mla_kv_norm[l]),
        mla_wq_b=mla_wq_b[l].astype(BF16), mla_wkv_b=mla_wkv_b[l].astype(BF16),
        mla_wqn=wqn.astype(BF16), mla_wqr=jnp.pad(wqr, rpad).astype(BF16),
        mla_wqs=jnp.pad(_swap_rope_halves(wqr), rpad).astype(BF16),
        mla_wkt=wkt.astype(BF16), mla_wvt=wvt.astype(BF16),
        w_branch_out=jnp.stack([w_lru_out[l], w_na_out[l], w_cm_out[l], w_mla_out[l]], axis=0).astype(BF16),
        w_out=w_out[l].astype(BF16),
        w_router=router.astype(BF16),
        w13=jnp.concatenate([moe_w1[l], moe_w3[l]], axis=-1).astype(BF16),
        w2=moe_w2[l].astype(BF16),
    )


def kernel(x_prompt, x_sample, cache_na_k, cache_na_v, cache_mla_ckv, cache_mla_krope, state_lru, c, c_ctx,
           w_ada, b_ada, norm1_g, w_in, lru_conv_w, lru_conv_b, lru_wa, lru_ba, lru_wx, lru_bx, lru_lam,
           w_lru_out, na_rpb, w_na_out, cm_dw_w, cm_dw_b, cm_ln_g, cm_ln_b, w_cm_out, mla_q_norm, mla_wq_b,
           mla_kv_norm, mla_wkv_b, w_mla_out, w_out, norm2_g, moe_w_group, moe_w_expert, moe_w1, moe_w3,
           moe_w2, final_g):
    nb_c, seq_c, d = x_prompt.shape
    nb_l, seq_l, _ = x_sample.shape
    depth = w_in.shape[0]
    past = cache_na_k.shape[2]
    t_ctx = nb_c * seq_c
    t_lat = nb_l * seq_l
    assert d == D_MODEL and seq_l % (GRID_W * NA_QROWS) == 0 and seq_l % seq_c == 0
    assert t_ctx % TM_MOE == 0 and seq_l % TM_MOE == 0 and seq_c % SEQ_CHUNK == 0 and seq_l % TK_MLA == 0
    n_ctx_tiles = t_ctx // TM_TOKEN
    tiles_per_lat = seq_l // TM_TOKEN
    lat_blk0 = t_ctx // seq_l
    assert lat_blk0 * seq_l == t_ctx and (seq_l // TK_MLA) % 2 == 0

    t_all = t_ctx + t_lat
    h = _TwoSource(x_prompt.reshape(t_ctx, d), x_sample.reshape(t_lat, d), n_ctx_tiles, 0)
    n_cond = 1 + nb_l
    cvec = jnp.concatenate([c_ctx[None, :], c, jnp.zeros((-n_cond % 8, d), F32)], axis=0)
    mods = _modulation(cvec, w_ada, b_ada)
    cos, sin = _rope_tables(seq_l)
    zero_state = jnp.zeros((nb_c, 2, LRU_WIDTH), F32)

    st_k, st_v, st_ckv, st_kr, st_lru = [], [], [], [], []
    for l in range(depth):
        p = _layer_params(l, w_in, lru_conv_w, lru_conv_b, lru_wa, lru_ba, lru_wx, lru_bx, lru_lam, w_lru_out,
                          w_na_out, cm_dw_w, cm_dw_b, cm_ln_g, cm_ln_b, w_cm_out, mla_q_norm, mla_wq_b,
                          mla_kv_norm, mla_wkv_b, w_mla_out, w_out, moe_w_group, moe_w_expert, moe_w1, moe_w3,
                          moe_w2)
        mod_l = mods[l].reshape(mods.shape[1], 1, mods.shape[2])
        g1 = norm1_g[l].reshape(1, d)
        g2 = norm2_g[l].reshape(1, d)
        z = _in_proj(h, t_all, mod_l, g1, p["w_small"], n_ctx_tiles, tiles_per_lat)

        pa_c, lru_c = _lru_branch(z["lx"], z["ly"], zero_state, p, seq_c, 0, nb_c)
        pc_c = _conformer_branch(z["ga"], z["gg"], p, seq_c, 0, nb_c)
        pb_c, pd_c, ckvn_c = _ctx_attention(z, p, seq_c, nb_c)

        pa_l, _ = _lru_branch(z["lx"], z["ly"], state_lru[:, l], p, seq_l, lat_blk0, nb_l)
        pc_l = _conformer_branch(z["ga"], z["gg"], p, seq_l, lat_blk0, nb_l)
        bias = _na_bias_tables(na_rpb[l], seq_l // GRID_W)
        pb_l = _na_latent(z, cache_na_k[:, l].reshape(nb_l, past, NA_HEADS * NA_HD),
                          cache_na_v[:, l].reshape(nb_l, past, NA_HEADS * NA_HD), bias, seq_l, lat_blk0, nb_l)
        qm, kl, klt = _mla_prep(z, cos, sin, p, seq_l, lat_blk0, nb_l)
        kc = jnp.concatenate([cache_mla_ckv[:, l], cache_mla_krope[:, l],
                              jnp.zeros((nb_l, past, MLA_KDIM - MLA_KV_LORA - MLA_ROPE), F32)], axis=-1).astype(BF16)
        kct = jnp.swapaxes(cache_mla_ckv[:, l], 1, 2).astype(BF16)
        pd_l = _mla_latent(qm, kl, klt, kc, kct, p["mla_wvt"], seq_l, nb_l)

        branches = [_TwoSource(br_c, br_l, n_ctx_tiles, 0)
                    for br_c, br_l in ((pa_c, pa_l), (pb_c, pb_l), (pc_c, pc_l), (pd_c, pd_l))]
        h_mid = _merge(h, t_all, mod_l, g1, branches, p, n_ctx_tiles, tiles_per_lat)
        if l + 1 < depth:
            h_new, = _moe(h_mid, mod_l, g2, p, n_ctx_tiles, tiles_per_lat)
            h = _TwoSource(h_new, h_new, n_ctx_tiles, n_ctx_tiles)
        else:
            y_prompt, y_sample = _moe(h_mid, mod_l, g2, p, n_ctx_tiles, tiles_per_lat, final_g.reshape(1, d))

        st_k.append(z["k"][:t_ctx].reshape(nb_c, seq_c, NA_HEADS, NA_HD))
        st_v.append(z["v"][:t_ctx].reshape(nb_c, seq_c, NA_HEADS, NA_HD))
        st_ckv.append(ckvn_c.reshape(nb_c, seq_c, MLA_KV_LORA))
        st_kr.append(z["krp"][:t_ctx, :MLA_ROPE].reshape(nb_c, seq_c, MLA_ROPE))
        st_lru.append(lru_c)

    y_prompt = y_prompt.reshape(nb_c, seq_c, d)
    y_sample = y_sample.reshape(nb_l, seq_l, d)
    return (y_prompt, y_sample, jnp.stack(st_k, axis=1), jnp.stack(st_v, axis=1), jnp.stack(st_ckv, axis=1),
            jnp.stack(st_kr, axis=1), jnp.stack(st_lru, axis=1))
```

```python
import functools
import math

import numpy as np
import jax
import jax.numpy as jnp
from jax import lax
from jax.experimental import pallas as pl
from jax.experimental.pallas import tpu as pltpu

F32 = jnp.float32
BF16 = jnp.bfloat16

D_MODEL = 1024
GRID_W = 64
EPS = 1e-6
NEG = -1e30
N_BRANCH = 4
LRU_WIDTH = 256
LRU_BLOCKS = 4
LRU_CONV = 4
LRU_C = 8.0
NA_HEADS = 4
NA_HD = 64
NA_WR = 8
NA_WC = 16
NA_QROWS = 4
NA_KROWS = NA_QROWS + NA_WR
CM_WIDTH = 256
CM_KERNEL = 31
MLA_HEADS = 4
MLA_Q_LORA = 256
MLA_KV_LORA = 128
MLA_NOPE = 64
MLA_ROPE = 32
MLA_VD = 64
MLA_KDIM = 256
ROPE_BASE = 10000.0
MOE_GROUPS = 4
MOE_PER_GROUP = 4
MOE_EXPERTS = 16
MOE_FF = 256
ROUTER_LANES = 128

TM_TOKEN = 512
TM_MOE = 1024
TQ_MLA = 1024
TK_MLA = 512
SEQ_CHUNK = 256
VMEM_LIMIT = 56 * 1024 * 1024


def _cparams(sem, vmem=VMEM_LIMIT):
    return pltpu.CompilerParams(dimension_semantics=sem, vmem_limit_bytes=vmem)


def _sigmoid(x):
    return 0.5 * jnp.tanh(0.5 * x) + 0.5


def _silu(x):
    return x * _sigmoid(x)


def _gelu_tanh(x):
    c = math.sqrt(2.0 / math.pi)
    return x * (0.5 * (1.0 + jnp.tanh(c * (x + 0.044715 * (x * x * x)))))


def _softplus(x):
    return jnp.maximum(x, 0.0) + jnp.log1p(jnp.exp(-jnp.abs(x)))


def _rms(x, g):
    return x * lax.rsqrt(jnp.mean(x * x, axis=-1, keepdims=True) + EPS) * g


def _norm_mod(h, g, scale, shift):
    return _rms(h, g) * (1.0 + scale) + shift


def _dot(a, b):
    return jnp.dot(a, b, preferred_element_type=F32)


def _dot_nt(a, b):
    return lax.dot_general(a, b, (((1,), (1,)), ((), ())), preferred_element_type=F32)


def _mod_kernel(c_ref, w_ref, b_ref, o_ref):
    s = _silu(c_ref[...])
    o_ref[0] = _dot(s.astype(BF16), w_ref[0].astype(BF16)) + b_ref[0]


def _modulation(cvec, w_ada, b_ada):
    depth, d, n = w_ada.shape
    rows = cvec.shape[0]
    tn = 1024
    return pl.pallas_call(
        _mod_kernel,
        grid=(depth, n // tn),
        in_specs=[pl.BlockSpec((rows, d), lambda l, j: (0, 0)),
                  pl.BlockSpec((1, d, tn), lambda l, j: (l, 0, j)),
                  pl.BlockSpec((1, 1, tn), lambda l, j: (l, 0, j))],
        out_specs=pl.BlockSpec((1, rows, tn), lambda l, j: (l, 0, j)),
        out_shape=jax.ShapeDtypeStruct((depth, rows, n), F32),
        compiler_params=_cparams(("parallel", "parallel")),
        name="modulation",
    )(cvec, w_ada, b_ada.reshape(depth, 1, n))


def _mod_row_map(n_ctx_tiles, tiles_per_lat):
    def index_map(i, *_):
        row = jnp.maximum(i - n_ctx_tiles, 0) // tiles_per_lat + (i >= n_ctx_tiles).astype(jnp.int32)
        return (row, 0, 0)
    return index_map


_IN_SEGS = (("lx", 256), ("ly", 256), ("q", 256), ("k", 256), ("v", 256), ("ga", 256), ("gg", 256),
            ("cq", 256), ("ckv", 128), ("krp", 128), ("krs", 128))
_IN_DTYPES = {"q": BF16}


class _TwoSource:
    def __init__(self, a, b, n_a, b_tile0):
        self.a, self.b, self.n_a, self.b_tile0 = a, b, n_a, b_tile0

    def specs(self, tm):
        n_a, b0 = self.n_a, self.b_tile0
        width = self.a.shape[1]
        return [pl.BlockSpec((tm, width), lambda i: (jnp.minimum(i, n_a - 1), 0)),
                pl.BlockSpec((tm, width), lambda i: (jnp.maximum(i - n_a, 0) + b0, 0))]


def _pick(n_a, a_ref, b_ref):
    return jnp.where(pl.program_id(0) < n_a, a_ref[...], b_ref[...])


def _in_kernel(n_a, ha_ref, hb_ref, mod_ref, g_ref, w_ref, *out_refs):
    mod = mod_ref[0]
    h = _pick(n_a, ha_ref, hb_ref)
    u = _norm_mod(h, g_ref[...], mod[:, D_MODEL:2 * D_MODEL], mod[:, 0:D_MODEL]).astype(BF16)
    off = 0
    for (_, width), o_ref in zip(_IN_SEGS, out_refs):
        o_ref[...] = _dot(u, w_ref[:, off:off + width]).astype(o_ref.dtype)
        off += width


def _in_proj(h, t, mods, g, w_small, n_ctx_tiles, tiles_per_lat):
    d = D_MODEL
    tm = TM_TOKEN
    n = w_small.shape[1]
    out_shape = [jax.ShapeDtypeStruct((t, width), _IN_DTYPES.get(name, F32)) for name, width in _IN_SEGS]
    out_specs = [pl.BlockSpec((tm, width), lambda i: (i, 0)) for _, width in _IN_SEGS]
    outs = pl.pallas_call(
        functools.partial(_in_kernel, h.n_a),
        grid=(t // tm,),
        in_specs=h.specs(tm) + [
            pl.BlockSpec((1, 1, mods.shape[-1]), _mod_row_map(n_ctx_tiles, tiles_per_lat)),
            pl.BlockSpec((1, d), lambda i: (0, 0)),
            pl.BlockSpec((d, n), lambda i: (0, 0))],
        out_specs=out_specs,
        out_shape=out_shape,
        compiler_params=_cparams(("parallel",)),
        name="in_proj",
    )(h.a, h.b, mods, g, w_small)
    return dict(zip([s for s, _ in _IN_SEGS], outs))


def _shifted_windows(xw, offsets, length):
    n = xw.shape[0]
    rolled = {0: xw}
    out = {}
    for o in offsets:
        r = o % 8
        if r not in rolled:
            rolled[r] = pltpu.roll(xw, n - r, 0)
        base = o - r
        out[o] = rolled[r][base:base + length]
    return out


def _lru_kernel(lx_ref, ly_ref, h0_ref, cw_ref, cb_ref, wa_ref, ba_ref, wx_ref, bx_ref, lam_ref,
                pa_ref, st_ref, pad_ref, af_ref, bf_ref, ab_ref, bb_ref):
    t = lx_ref.shape[0]
    ch = min(SEQ_CHUNK, t)
    halo = 8
    zeros = jnp.zeros((halo, LRU_WIDTH), F32)
    pad_ref[0:halo, :] = zeros
    pad_ref[halo + t:2 * halo + t, :] = zeros
    pad_ref[halo:halo + t, :] = lx_ref[...]
    pad_l = LRU_CONV // 2
    taps = [halo - pad_l + k for k in range(LRU_CONV)]
    a_refs = (af_ref, ab_ref)
    b_refs = (bf_ref, bb_ref)

    def gates(c, carry):
        r0 = pl.multiple_of(c * ch, ch)
        xw = pad_ref[pl.ds(r0, ch + 2 * halo), :]
        win = _shifted_windows(xw, taps, ch)
        xc = cb_ref[...] + sum(cw_ref[k:k + 1, :] * win[taps[k]] for k in range(LRU_CONV))
        xcb = xc.astype(BF16)
        for d in range(2):
            r = _sigmoid(_dot(xcb, wa_ref[d]) + ba_ref[d])
            i = _sigmoid(_dot(xcb, wx_ref[d]) + bx_ref[d])
            log_a = (-LRU_C) * r * _softplus(-lam_ref[d])
            a = jnp.exp(log_a)
            one_minus_a2 = -jnp.tanh(log_a) * (a * a + 1.0)
            a_refs[d][pl.ds(r0, ch), :] = a
            b_refs[d][pl.ds(r0, ch), :] = jnp.sqrt(one_minus_a2) * (i * xc)
        return carry

    lax.fori_loop(0, t // ch, gates, 0)

    sub = 8
    row = lax.broadcasted_iota(jnp.int32, (sub, LRU_WIDTH), 0)

    def bcast(x, j):
        return jnp.broadcast_to(x[j:j + 1, :], (sub, LRU_WIDTH))

    def scan(c, carry):
        hf, hb = carry
        rf = pl.multiple_of(c * sub, sub)
        rb = pl.multiple_of(t - sub - c * sub, sub)
        a_f, b_f = af_ref[pl.ds(rf, sub), :], bf_ref[pl.ds(rf, sub), :]
        a_b, b_b = ab_ref[pl.ds(rb, sub), :], bb_ref[pl.ds(rb, sub), :]
        out_f = jnp.zeros((sub, LRU_WIDTH), F32)
        out_b = jnp.zeros((sub, LRU_WIDTH), F32)
        for j in range(sub):
            hf = bcast(a_f, j) * hf + bcast(b_f, j)
            out_f = jnp.where(row == j, hf, out_f)
            jb = sub - 1 - j
            hb = bcast(a_b, jb) * hb + bcast(b_b, jb)
            out_b = jnp.where(row == jb, hb, out_b)
        bf_ref[pl.ds(rf, sub), :] = out_f
        bb_ref[pl.ds(rb, sub), :] = out_b
        return hf, hb

    h0 = (jnp.broadcast_to(h0_ref[0, 0:1, :], (sub, LRU_WIDTH)), jnp.broadcast_to(h0_ref[0, 1:2, :], (sub, LRU_WIDTH)))
    hf, hb = lax.fori_loop(0, t // sub, scan, h0)
    st_ref[0, 0:1, :] = hf[0:1, :]
    st_ref[0, 1:2, :] = hb[0:1, :]

    def emit(c, carry):
        r0 = pl.multiple_of(c * ch, ch)
        y = (bf_ref[pl.ds(r0, ch), :] + bb_ref[pl.ds(r0, ch), :]) * _gelu_tanh(ly_ref[pl.ds(r0, ch), :])
        pa_ref[pl.ds(r0, ch), :] = y.astype(pa_ref.dtype)
        return carry

    lax.fori_loop(0, t // ch, emit, 0)


def _lru_branch(lx, ly, h0, p, seq, blk0, nb):
    w = LRU_WIDTH
    const2 = lambda b: (0, 0)
    const3 = lambda b: (0, 0, 0)
    return pl.pallas_call(
        _lru_kernel,
        grid=(nb,),
        in_specs=[pl.BlockSpec((seq, w), lambda b: (blk0 + b, 0)),
                  pl.BlockSpec((seq, w), lambda b: (blk0 + b, 0)),
                  pl.BlockSpec((1, 2, w), lambda b: (b, 0, 0)),
                  pl.BlockSpec((LRU_CONV, w), const2),
                  pl.BlockSpec((1, w), const2),
                  pl.BlockSpec((2, w, w), const3),
                  pl.BlockSpec((2, 1, w), const3),
                  pl.BlockSpec((2, w, w), const3),
                  pl.BlockSpec((2, 1, w), const3),
                  pl.BlockSpec((2, 1, w), const3)],
        out_specs=[pl.BlockSpec((seq, w), lambda b: (b, 0)),
                   pl.BlockSpec((1, 2, w), lambda b: (b, 0, 0))],
        out_shape=[jax.ShapeDtypeStruct((nb * seq, w), BF16),
                   jax.ShapeDtypeStruct((nb, 2, w), F32)],
        scratch_shapes=[pltpu.VMEM((seq + 16, w), F32)] + [pltpu.VMEM((seq, w), F32)] * 4,
        compiler_params=_cparams(("parallel",)),
        name="rglru",
    )(lx, ly, h0, p["lru_conv_w"], p["lru_conv_b"], p["lru_wa"], p["lru_ba"], p["lru_wx"], p["lru_bx"],
      p["lru_lam"])


def _cm_kernel(ga_ref, gg_ref, w_ref, b_ref, lg_ref, lb_ref, pc_ref, pad_ref):
    t = ga_ref.shape[0]
    ch = min(SEQ_CHUNK, t)
    halo = 16
    zeros = jnp.zeros((halo, CM_WIDTH), F32)
    pad_ref[0:halo, :] = zeros
    pad_ref[halo + t:2 * halo + t, :] = zeros

    def glu(c, carry):
        r0 = pl.multiple_of(c * ch, ch)
        pad_ref[pl.ds(halo + r0, ch), :] = ga_ref[pl.ds(r0, ch), :] * _sigmoid(gg_ref[pl.ds(r0, ch), :])
        return carry

    lax.fori_loop(0, t // ch, glu, 0)
    taps = [halo - CM_KERNEL // 2 + k for k in range(CM_KERNEL)]

    def conv(c, carry):
        r0 = pl.multiple_of(c * ch, ch)
        xw = pad_ref[pl.ds(r0, ch + 2 * halo), :]
        win = _shifted_windows(xw, taps, ch)
        z = b_ref[...] + sum(w_ref[k:k + 1, :] * win[taps[k]] for k in range(CM_KERNEL))
        mu = jnp.mean(z, axis=-1, keepdims=True)
        zc = z - mu
        var = jnp.mean(zc * zc, axis=-1, keepdims=True)
        y = zc * lax.rsqrt(var + EPS) * lg_ref[...] + lb_ref[...]
        pc_ref[pl.ds(r0, ch), :] = _silu(y).astype(pc_ref.dtype)
        return carry

    lax.fori_loop(0, t // ch, conv, 0)


def _conformer_branch(ga, gg, p, seq, blk0, nb):
    w = CM_WIDTH
    const2 = lambda b: (0, 0)
    return pl.pallas_call(
        _cm_kernel,
        grid=(nb,),
        in_specs=[pl.BlockSpec((seq, w), lambda b: (blk0 + b, 0)),
                  pl.BlockSpec((seq, w), lambda b: (blk0 + b, 0)),
                  pl.BlockSpec((CM_KERNEL, w), const2),
                  pl.BlockSpec((1, w), const2),
                  pl.BlockSpec((1, w), const2),
                  pl.BlockSpec((1, w), const2)],
        out_specs=pl.BlockSpec((seq, w), lambda b: (b, 0)),
        out_shape=jax.ShapeDtypeStruct((nb * seq, w), BF16),
        scratch_shapes=[pltpu.VMEM((seq + 32, w), F32)],
        compiler_params=_cparams(("parallel",)),
        name="conformer",
    )(ga, gg, p["cm_dw_w"], p["cm_dw_b"], p["cm_ln_g"], p["cm_ln_b"])


def _softmax_pv(scores, values):
    m = functools.reduce(jnp.maximum, [jnp.max(s, axis=-1, keepdims=True) for s in scores])
    ps = [jnp.exp(s - m) for s in scores]
    l = sum(jnp.sum(p, axis=-1, keepdims=True) for p in ps)
    o = sum(_dot(p.astype(BF16), v) for p, v in zip(ps, values))
    return o / l


def _ctx_attn_kernel(q_ref, k_ref, v_ref, cq_ref, ckv_ref, krp_ref, qn_ref, wq_ref, kvn_ref, wkv_ref,
                     pb_ref, pd_ref, ckvn_ref):
    q = q_ref[...]
    k = k_ref[...].astype(BF16)
    v = v_ref[...].astype(BF16)
    na_scale = NA_HD ** -0.5
    for h in range(NA_HEADS):
        sl = slice(h * NA_HD, (h + 1) * NA_HD)
        s = _dot_nt(q[:, sl], k[:, sl]) * na_scale
        pb_ref[:, sl] = _softmax_pv([s], [v[:, sl]]).astype(pb_ref.dtype)

    ckv_n = _rms(ckv_ref[...], kvn_ref[...])
    ckvn_ref[...] = ckv_n
    mla_scale = (MLA_NOPE + MLA_ROPE) ** -0.5
    qf = _dot(_rms(cq_ref[...], qn_ref[...]).astype(BF16), wq_ref[...]).astype(BF16)
    kv = _dot(ckv_n.astype(BF16), wkv_ref[...]).astype(BF16)
    kr = krp_ref[:, 0:MLA_ROPE].astype(BF16)
    qd = MLA_NOPE + MLA_ROPE
    kd = MLA_NOPE + MLA_VD
    for h in range(MLA_HEADS):
        qn = qf[:, h * qd:h * qd + MLA_NOPE]
        qr = qf[:, h * qd + MLA_NOPE:(h + 1) * qd]
        kn = kv[:, h * kd:h * kd + MLA_NOPE]
        vm = kv[:, h * kd + MLA_NOPE:(h + 1) * kd]
        s = (_dot_nt(qn, kn) + _dot_nt(qr, kr)) * mla_scale
        pd_ref[:, h * MLA_VD:(h + 1) * MLA_VD] = _softmax_pv([s], [vm]).astype(pd_ref.dtype)


def _ctx_attention(z, p, seq, nb):
    const2 = lambda b: (0, 0)
    row = lambda b: (b, 0)
    return pl.pallas_call(
        _ctx_attn_kernel,
        grid=(nb,),
        in_specs=[pl.BlockSpec((seq, 256), row), pl.BlockSpec((seq, 256), row), pl.BlockSpec((seq, 256), row),
                  pl.BlockSpec((seq, MLA_Q_LORA), row), pl.BlockSpec((seq, MLA_KV_LORA), row),
                  pl.BlockSpec((seq, 128), row),
                  pl.BlockSpec((1, MLA_Q_LORA), const2),
                  pl.BlockSpec(p["mla_wq_b"].shape, const2),
                  pl.BlockSpec((1, MLA_KV_LORA), const2),
                  pl.BlockSpec(p["mla_wkv_b"].shape, const2)],
        out_specs=[pl.BlockSpec((seq, 256), row), pl.BlockSpec((seq, 256), row),
                   pl.BlockSpec((seq, MLA_KV_LORA), row)],
        out_shape=[jax.ShapeDtypeStruct((nb * seq, 256), BF16),
                   jax.ShapeDtypeStruct((nb * seq, 256), BF16),
                   jax.ShapeDtypeStruct((nb * seq, MLA_KV_LORA), F32)],
        compiler_params=_cparams(("parallel",)),
        name="ctx_attention",
    )(z["q"], z["k"], z["v"], z["cq"], z["ckv"], z["krp"], p["mla_q_norm"], p["mla_wq_b"],
      p["mla_kv_norm"], p["mla_wkv_b"])


def _na_bias_tables(rpb, rows):
    heads, n_rel_r, n_rel_c = rpb.shape
    w = GRID_W
    u = jnp.concatenate([rpb[:, :, NA_WC - 1:], jnp.zeros((heads, n_rel_r, 2 * w - n_rel_c), rpb.dtype),
                         rpb[:, :, :NA_WC - 1]], axis=-1)
    toep = jnp.tile(u, (1, 1, w))[:, :, :w * (2 * w - 1)].reshape(heads, n_rel_r, w, 2 * w - 1)[..., :w]
    qc = np.arange(w)
    cs = np.clip(qc - NA_WC // 2, 0, w - NA_WC)
    col_ok = (qc[None, :] >= cs[:, None]) & (qc[None, :] < cs[:, None] + NA_WC)
    toep = jnp.where(col_ok, toep, NEG)
    masked = jnp.full((heads, w, w), NEG, rpb.dtype)
    n_steps = rows // NA_QROWS
    tables = []
    for step in (0, 1, n_steps - 1):
        start = int(np.clip(NA_QROWS * step - NA_WR // 2, 0, rows - NA_KROWS))
        block_rows = []
        for a in range(NA_QROWS):
            qr = NA_QROWS * step + a
            rs = int(np.clip(qr - NA_WR // 2, 0, rows - NA_WR))
            blocks = []
            for b in range(NA_KROWS):
                kr = start + b
                blocks.append(toep[:, kr - qr + NA_WR - 1] if rs <= kr < rs + NA_WR else masked)
            block_rows.append(jnp.concatenate(blocks, axis=-1))
        tables.append(jnp.concatenate(block_rows, axis=1))
    return jnp.stack(tables, axis=0)


def _na_lat_kernel(q_ref, k_ref, v_ref, ck_ref, cv_ref, bias_ref, pb_ref):
    i = pl.program_id(1)
    rows = k_ref.shape[0] // GRID_W
    start = jnp.clip(NA_QROWS * i - NA_WR // 2, 0, rows - NA_KROWS) * GRID_W
    start = pl.multiple_of(start, GRID_W)
    nk = NA_KROWS * GRID_W
    q = q_ref[...]
    kw = k_ref[pl.ds(start, nk), :].astype(BF16)
    vw = v_ref[pl.ds(start, nk), :].astype(BF16)
    ck = ck_ref[0].astype(BF16)
    cv = cv_ref[0].astype(BF16)
    scale = NA_HD ** -0.5
    for h in range(NA_HEADS):
        sl = slice(h * NA_HD, (h + 1) * NA_HD)
        s_loc = _dot_nt(q[:, sl], kw[:, sl]) * scale + bias_ref[0, h]
        s_ctx = _dot_nt(q[:, sl], ck[:, sl]) * scale
        pb_ref[:, sl] = _softmax_pv([s_loc, s_ctx], [vw[:, sl], cv[:, sl]]).astype(pb_ref.dtype)


def _na_latent(z, cache_k, cache_v, bias, seq, blk0, nb):
    tq = NA_QROWS * GRID_W
    steps = seq // tq
    nk = NA_KROWS * GRID_W
    past = cache_k.shape[1]

    def bias_map(b, i):
        return (jnp.where(i == 0, 0, jnp.where(i == steps - 1, 2, 1)), 0, 0, 0)

    return pl.pallas_call(
        _na_lat_kernel,
        grid=(nb, steps),
        in_specs=[pl.BlockSpec((tq, 256), lambda b, i: (blk0 * steps + b * steps + i, 0)),
                  pl.BlockSpec((seq, 256), lambda b, i: (blk0 + b, 0)),
                  pl.BlockSpec((seq, 256), lambda b, i: (blk0 + b, 0)),
                  pl.BlockSpec((1, past, 256), lambda b, i: (b, 0, 0)),
                  pl.BlockSpec((1, past, 256), lambda b, i: (b, 0, 0)),
                  pl.BlockSpec((1, NA_HEADS, tq, nk), bias_map)],
        out_specs=pl.BlockSpec((tq, 256), lambda b, i: (b * steps + i, 0)),
        out_shape=jax.ShapeDtypeStruct((nb * seq, 256), BF16),
        compiler_params=_cparams(("parallel", "arbitrary")),
        name="na_latent",
    )(z["q"], z["k"], z["v"], cache_k, cache_v, bias)


def _rope_tables(seq):
    t = np.arange(seq)
    pos = np.stack([t // GRID_W, t % GRID_W], axis=-1).astype(np.float32)
    n_freq = MLA_ROPE // 4
    inv = jnp.asarray(ROPE_BASE, F32) ** (-jnp.arange(n_freq, dtype=F32) / n_freq)
    ang = jnp.asarray(pos)[:, :, None] * inv
    cos, sin = jnp.cos(ang), jnp.sin(ang)
    c = jnp.stack([cos, cos], axis=2).reshape(seq, MLA_ROPE)
    s = jnp.stack([-sin, sin], axis=2).reshape(seq, MLA_ROPE)
    pad = ((0, 0), (0, 128 - MLA_ROPE))
    return jnp.pad(c, pad), jnp.pad(s, pad)


def _swap_rope_halves(w):
    w4 = w.reshape(w.shape[:-1] + (2, 2, MLA_ROPE // 4))
    return w4[..., ::-1, :].reshape(w.shape)


def _mla_prep_kernel(cq_ref, ckv_ref, krp_ref, krs_ref, cos_ref, sin_ref, qn_ref, wqn_ref, wqr_ref, wqs_ref,
                     kvn_ref, wkt_ref, qm_ref, kl_ref, klt_ref):
    tq = cq_ref.shape[0]
    cos = cos_ref[...]
    sin = sin_ref[...]
    ckv_n = _rms(ckv_ref[...], kvn_ref[...])
    kl_ref[0, :, 0:128] = ckv_n.astype(kl_ref.dtype)
    kl_ref[0, :, 128:256] = (krp_ref[...] * cos + krs_ref[...] * sin).astype(kl_ref.dtype)
    klt_ref[0] = ckv_n.T.astype(klt_ref.dtype)
    cqn = _rms(cq_ref[...], qn_ref[...]).astype(BF16)
    qn = _dot(cqn, wqn_ref[...]).astype(BF16)
    scale = (MLA_NOPE + MLA_ROPE) ** -0.5 * math.log2(math.e)
    for h in range(MLA_HEADS):
        qa = _dot(qn[:, h * MLA_NOPE:(h + 1) * MLA_NOPE], wkt_ref[h])
        qr = _dot(cqn, wqr_ref[h]) * cos + _dot(cqn, wqs_ref[h]) * sin
        qm_ref[0, 0, 0:128, h * tq:(h + 1) * tq] = (qa * scale).T.astype(qm_ref.dtype)
        qm_ref[0, 0, 128:256, h * tq:(h + 1) * tq] = (qr * scale).T.astype(qm_ref.dtype)


def _mla_prep(z, cos, sin, p, seq, blk0, nb):
    tq = TQ_MLA
    steps = seq // tq
    tok = lambda b, i: (blk0 * steps + b * steps + i, 0)
    pos = lambda b, i: (i, 0)
    const2 = lambda b, i: (0, 0)
    const3 = lambda b, i: (0, 0, 0)
    return pl.pallas_call(
        _mla_prep_kernel,
        grid=(nb, steps),
        in_specs=[pl.BlockSpec((tq, MLA_Q_LORA), tok), pl.BlockSpec((tq, MLA_KV_LORA), tok),
                  pl.BlockSpec((tq, 128), tok), pl.BlockSpec((tq, 128), tok),
                  pl.BlockSpec((tq, 128), pos), pl.BlockSpec((tq, 128), pos),
                  pl.BlockSpec((1, MLA_Q_LORA), const2),
                  pl.BlockSpec((MLA_Q_LORA, MLA_HEADS * MLA_NOPE), const2),
                  pl.BlockSpec((MLA_HEADS, MLA_Q_LORA, 128), const3),
                  pl.BlockSpec((MLA_HEADS, MLA_Q_LORA, 128), const3),
                  pl.BlockSpec((1, MLA_KV_LORA), const2),
                  pl.BlockSpec((MLA_HEADS, MLA_NOPE, MLA_KV_LORA), const3)],
        out_specs=[pl.BlockSpec((1, 1, MLA_KDIM, MLA_HEADS * tq), lambda b, i: (b, i, 0, 0)),
                   pl.BlockSpec((1, tq, MLA_KDIM), lambda b, i: (b, i, 0)),
                   pl.BlockSpec((1, MLA_KV_LORA, tq), lambda b, i: (b, 0, i))],
        out_shape=[jax.ShapeDtypeStruct((nb, steps, MLA_KDIM, MLA_HEADS * tq), BF16),
                   jax.ShapeDtypeStruct((nb, seq, MLA_KDIM), BF16),
                   jax.ShapeDtypeStruct((nb, MLA_KV_LORA, seq), BF16)],
        compiler_params=_cparams(("parallel", "parallel")),
        name="mla_prep",
    )(z["cq"], z["ckv"], z["krp"], z["krs"], cos, sin, p["mla_q_norm"], p["mla_wqn"], p["mla_wqr"],
      p["mla_wqs"], p["mla_kv_norm"], p["mla_wkt"])


def _mla_lat_kernel(qt_ref, kl_ref, klt_ref, kc_ref, kct_ref, wvt_ref, pd_ref, m_ref, l_ref, acc_ref,
                    s_buf, p_buf, a_buf):
    qt = qt_ref[0, 0]
    cols = qt.shape[1]
    tq = cols // MLA_HEADS
    n_chunks = kl_ref.shape[1] // TK_MLA
    m_ref[...] = jnp.full((1, cols), NEG, F32)
    l_ref[...] = jnp.zeros((1, cols), F32)
    acc_ref[...] = jnp.zeros((MLA_KV_LORA, cols), F32)

    def softmax_stats(s):
        m_old = m_ref[...]
        m_new = jnp.maximum(m_old, jnp.max(s, axis=0, keepdims=True))
        alpha = jnp.exp2(m_old - m_new)
        p = jnp.exp2(s - m_new)
        l_ref[...] = alpha * l_ref[...] + jnp.sum(p, axis=0, keepdims=True)
        m_ref[...] = m_new
        return alpha, p.astype(BF16)

    alpha, p = softmax_stats(_dot(kc_ref[0], qt))
    acc_ref[...] = alpha * acc_ref[...] + _dot(kct_ref[0], p)

    def chunk(c):
        return pl.ds(pl.multiple_of(c * TK_MLA, TK_MLA), TK_MLA)

    def scores(c, slot):
        s_buf[slot] = _dot(kl_ref[0, chunk(c), :], qt)

    def softmax(slot):
        alpha, p = softmax_stats(s_buf[slot])
        a_buf[slot] = alpha
        p_buf[slot] = p

    def values(c, slot):
        acc_ref[...] = a_buf[slot] * acc_ref[...] + _dot(klt_ref[0, :, chunk(c)], p_buf[slot])

    scores(0, 0)
    scores(1, 1)
    softmax(0)

    def body(j, carry):
        c = 2 * j
        scores(c + 2, 0)
        softmax(1)
        values(c, 0)
        scores(c + 3, 1)
        softmax(0)
        values(c + 1, 1)
        return carry

    lax.fori_loop(0, (n_chunks - 2) // 2, body, 0)
    softmax(1)
    values(n_chunks - 2, 0)
    values(n_chunks - 1, 1)
    o = (acc_ref[...] / l_ref[...]).astype(BF16)
    out_t = jnp.concatenate([_dot(wvt_ref[h], o[:, h * tq:(h + 1) * tq]) for h in range(MLA_HEADS)], axis=0)
    pd_ref[...] = out_t.T.astype(pd_ref.dtype)


def _mla_latent(qm, kl, klt, kc, kct, wvt, seq, nb):
    tq = TQ_MLA
    steps = seq // tq
    cols = MLA_HEADS * tq
    past = kc.shape[1]
    return pl.pallas_call(
        _mla_lat_kernel,
        grid=(nb, steps),
        in_specs=[pl.BlockSpec((1, 1, MLA_KDIM, cols), lambda b, i: (b, i, 0, 0)),
                  pl.BlockSpec((1, seq, MLA_KDIM), lambda b, i: (b, 0, 0)),
                  pl.BlockSpec((1, MLA_KV_LORA, seq), lambda b, i: (b, 0, 0)),
                  pl.BlockSpec((1, past, MLA_KDIM), lambda b, i: (b, 0, 0)),
                  pl.BlockSpec((1, MLA_KV_LORA, past), lambda b, i: (b, 0, 0)),
                  pl.BlockSpec((MLA_HEADS, MLA_VD, MLA_KV_LORA), lambda b, i: (0, 0, 0))],
        out_specs=pl.BlockSpec((tq, MLA_HEADS * MLA_VD), lambda b, i: (b * steps + i, 0)),
        out_shape=jax.ShapeDtypeStruct((nb * seq, MLA_HEADS * MLA_VD), BF16),
        scratch_shapes=[pltpu.VMEM((1, cols), F32), pltpu.VMEM((1, cols), F32),
                        pltpu.VMEM((MLA_KV_LORA, cols), F32),
                        pltpu.VMEM((2, TK_MLA, cols), F32), pltpu.VMEM((2, TK_MLA, cols), BF16),
                        pltpu.VMEM((2, 1, cols), F32)],
        compiler_params=_cparams(("parallel", "arbitrary")),
        name="mla_latent",
    )(qm, kl, klt, kc, kct, wvt)


def _merge_kernel(n_a, ha_ref, hb_ref, mod_ref, g_ref, pa_c, pa_l, pb_c, pb_l, pc_c, pc_l, pd_c, pd_l,
                  wg_ref, wb_ref, wo_ref, o_ref):
    mod = mod_ref[0]
    h = _pick(n_a, ha_ref, hb_ref)
    u = _norm_mod(h, g_ref[...], mod[:, D_MODEL:2 * D_MODEL], mod[:, 0:D_MODEL]).astype(BF16)
    y = None
    for j, (c_ref, l_ref) in enumerate(((pa_c, pa_l), (pb_c, pb_l), (pc_c, pc_l), (pd_c, pd_l))):
        gate = _sigmoid(_dot(u, wg_ref[:, j * D_MODEL:(j + 1) * D_MODEL]))
        term = gate * _dot(_pick(n_a, c_ref, l_ref), wb_ref[j])
        y = term if y is None else y + term
    out = _dot(y.astype(BF16), wo_ref[...])
    o_ref[...] = h + mod[:, 2 * D_MODEL:3 * D_MODEL] * out


def _merge(h, t, mods, g, branches, p, n_ctx_tiles, tiles_per_lat):
    d = D_MODEL
    tm = TM_TOKEN
    const2 = lambda i: (0, 0)
    branch_specs, branch_args = [], []
    for br in branches:
        branch_specs += br.specs(tm)
        branch_args += [br.a, br.b]
    return pl.pallas_call(
        functools.partial(_merge_kernel, h.n_a),
        grid=(t // tm,),
        in_specs=h.specs(tm) + [
            pl.BlockSpec((1, 1, mods.shape[-1]), _mod_row_map(n_ctx_tiles, tiles_per_lat)),
            pl.BlockSpec((1, d), const2)] + branch_specs + [
            pl.BlockSpec((d, N_BRANCH * d), const2),
            pl.BlockSpec((N_BRANCH, 256, d), lambda i: (0, 0, 0)),
            pl.BlockSpec((d, d), const2)],
        out_specs=pl.BlockSpec((tm, d), lambda i: (i, 0)),
        out_shape=jax.ShapeDtypeStruct((t, d), F32),
        compiler_params=_cparams(("parallel",)),
        name="merge",
    )(h.a, h.b, mods, g, *branch_args, p["w_gates"], p["w_branch_out"], p["w_out"])


def _router_gates(logits):
    lane_i = lax.broadcasted_iota(jnp.int32, logits.shape, 1)
    lane = lane_i.astype(F32)
    ninf = jnp.float32(-jnp.inf)

    def first_argmax(x):
        m = jnp.max(x, axis=-1, keepdims=True)
        idx = jnp.min(jnp.where(x == m, lane, jnp.float32(1e9)), axis=-1, keepdims=True)
        return m, idx

    gl = jnp.where(lane_i < MOE_GROUPS, logits, ninf)
    gmax, gsel = first_argmax(gl)
    gp = 1.0 / jnp.sum(jnp.exp(gl - gmax), axis=-1, keepdims=True)
    e_idx = lane_i - MOE_GROUPS
    e_group = lax.shift_right_arithmetic(e_idx, jnp.full_like(e_idx, 2)).astype(F32)
    in_group = (e_idx >= 0) & (e_idx < MOE_EXPERTS) & (e_group == gsel)
    el = jnp.where(in_group, logits, ninf)
    m1, i1 = first_argmax(el)
    m2, i2 = first_argmax(jnp.where(lane == i1, ninf, el))
    e2 = jnp.exp(m2 - m1)
    w1 = gp / (1.0 + e2)
    w2 = gp * e2 / (1.0 + e2)
    return jnp.where(lane == i1, w1, 0.0) + jnp.where(lane == i2, w2, 0.0)


def _moe_kernel(n_ctx, is_final, h_ref, mod_ref, g_ref, *refs):
    if is_final:
        fg_ref, refs = refs[0], refs[1:]
        n_out = 2
    else:
        fg_ref, n_out = None, 1
    wr_ref, w13_ref, w2_ref = refs[:3]
    out_refs = refs[3:3 + n_out]
    u_ref, gate_ref, acc_ref = refs[3 + n_out:]
    e = pl.program_id(1)
    mod = mod_ref[0]

    @pl.when(e == 0)
    def _():
        u = _norm_mod(h_ref[...], g_ref[...], mod[:, 4 * D_MODEL:5 * D_MODEL], mod[:, 3 * D_MODEL:4 * D_MODEL])
        ub = u.astype(BF16)
        u_ref[...] = ub
        gate_ref[...] = _router_gates(_dot(ub, wr_ref[...]))
        acc_ref[...] = jnp.zeros_like(acc_ref)

    ub = u_ref[...]
    lane = lax.broadcasted_iota(jnp.int32, gate_ref.shape, 1)
    gate = jnp.sum(jnp.where(lane == e + MOE_GROUPS, gate_ref[...], 0.0), axis=-1, keepdims=True)
    h13 = _dot(ub, w13_ref[0])
    hid = _silu(h13[:, 0:MOE_FF]) * h13[:, MOE_FF:2 * MOE_FF] * gate
    acc_ref[...] += _dot(hid.astype(BF16), w2_ref[0])

    @pl.when(e == MOE_EXPERTS - 1)
    def _():
        h_out = h_ref[...] + mod[:, 5 * D_MODEL:6 * D_MODEL] * acc_ref[...]
        if fg_ref is None:
            out_refs[0][...] = h_out
        else:
            y = _rms(h_out, fg_ref[...])
            is_ctx = pl.program_id(0) < n_ctx

            @pl.when(is_ctx)
            def _():
                out_refs[0][...] = y

            @pl.when(jnp.logical_not(is_ctx))
            def _():
                out_refs[1][...] = y


def _moe(h, mods, g, p, n_ctx_tiles, tiles_per_lat, final_g=None):
    t, d = h.shape
    tm = TM_MOE
    nct = n_ctx_tiles * TM_TOKEN // tm
    tpl = tiles_per_lat * TM_TOKEN // tm
    in_specs = [pl.BlockSpec((tm, d), lambda i, e: (i, 0)),
                pl.BlockSpec((1, 1, mods.shape[-1]), _mod_row_map(nct, tpl)),
                pl.BlockSpec((1, d), lambda i, e: (0, 0))]
    args = [h, mods, g]
    if final_g is None:
        out_specs = [pl.BlockSpec((tm, d), lambda i, e: (i, 0))]
        out_shape = [jax.ShapeDtypeStruct((t, d), F32)]
    else:
        in_specs.append(pl.BlockSpec((1, d), lambda i, e: (0, 0)))
        args.append(final_g)
        out_specs = [pl.BlockSpec((tm, d), lambda i, e: (jnp.minimum(i, nct - 1), 0)),
                     pl.BlockSpec((tm, d), lambda i, e: (jnp.maximum(i - nct, 0), 0))]
        out_shape = [jax.ShapeDtypeStruct((nct * tm, d), F32), jax.ShapeDtypeStruct((t - nct * tm, d), F32)]
    in_specs += [pl.BlockSpec((d, ROUTER_LANES), lambda i, e: (0, 0)),
                 pl.BlockSpec((1, d, 2 * MOE_FF), lambda i, e: (e, 0, 0)),
                 pl.BlockSpec((1, MOE_FF, d), lambda i, e: (e, 0, 0))]
    args += [p["w_router"], p["w13"], p["w2"]]
    return pl.pallas_call(
        functools.partial(_moe_kernel, nct, final_g is not None),
        grid=(t // tm, MOE_EXPERTS),
        in_specs=in_specs,
        out_specs=out_specs,
        out_shape=out_shape,
        scratch_shapes=[pltpu.VMEM((tm, d), BF16), pltpu.VMEM((tm, ROUTER_LANES), F32),
                        pltpu.VMEM((tm, d), F32)],
        compiler_params=_cparams(("arbitrary", "arbitrary")),
        name="moe",
    )(*args)


def _block_diag(w):
    nd, nb, bw, _ = w.shape
    eye = jnp.eye(nb, dtype=w.dtype)
    return jnp.einsum("dnij,nm->dnimj", w, eye).reshape(nd, nb * bw, nb * bw)


def _layer_params(l, w_in, lru_conv_w, lru_conv_b, lru_wa, lru_ba, lru_wx, lru_bx, lru_lam, w_lru_out, w_na_out,
                  cm_dw_w, cm_dw_b, cm_ln_g, cm_ln_b, w_cm_out, mla_q_norm, mla_wq_b, mla_kv_norm, mla_wkv_b,
                  w_mla_out, w_out, moe_w_group, moe_w_expert, moe_w1, moe_w3, moe_w2):
    wi = w_in[l]
    n_small = 2 * LRU_WIDTH + 3 * NA_HEADS * NA_HD + 2 * CM_WIDTH + MLA_Q_LORA + MLA_KV_LORA
    kr_cols = wi[:, n_small:n_small + MLA_ROPE]
    zpad = jnp.zeros((D_MODEL, 128 - MLA_ROPE), wi.dtype)
    w_small = jnp.concatenate([wi[:, :n_small], kr_cols, zpad, _swap_rope_halves(kr_cols), zpad], axis=1)
    w_gates = wi[:, n_small + MLA_ROPE:]

    qd = MLA_NOPE + MLA_ROPE
    wq = mla_wq_b[l].reshape(MLA_Q_LORA, MLA_HEADS, qd)
    wqn = wq[:, :, :MLA_NOPE].reshape(MLA_Q_LORA, MLA_HEADS * MLA_NOPE)
    wqr = jnp.moveaxis(wq[:, :, MLA_NOPE:], 1, 0)
    rpad = ((0, 0), (0, 0), (0, 128 - MLA_ROPE))
    wkv = mla_wkv_b[l].reshape(MLA_KV_LORA, MLA_HEADS, MLA_NOPE + MLA_VD)
    wkt = jnp.transpose(wkv[:, :, :MLA_NOPE], (1, 2, 0))
    wvt = jnp.transpose(wkv[:, :, MLA_NOPE:], (1, 2, 0))

    router = jnp.concatenate([moe_w_group[l], moe_w_expert[l]], axis=1)
    router = jnp.pad(router, ((0, 0), (0, ROUTER_LANES - router.shape[1])))
    row = lambda a: a.reshape(1, -1)
    return dict(
        w_small=w_small.astype(BF16), w_gates=w_gates.astype(BF16),
        lru_conv_w=lru_conv_w[l], lru_conv_b=row(lru_conv_b[l]),
        lru_wa=_block_diag(lru_wa[l]).astype(BF16), lru_ba=lru_ba[l][:, None, :],
        lru_wx=_block_diag(lru_wx[l]).astype(BF16), lru_bx=lru_bx[l][:, None, :],
        lru_lam=lru_lam[l][:, None, :],
        cm_dw_w=cm_dw_w[l], cm_dw_b=row(cm_dw_b[l]), cm_ln_g=row(cm_ln_g[l]), cm_ln_b=row(cm_ln_b[l]),
        mla_q_norm=row(mla_q_norm[l]), mla_kv_norm=row(mla_kv_norm[l]),
        mla_wq_b=mla_wq_b[l].astype(BF16), mla_wkv_b=mla_wkv_b[l].astype(BF16),
        mla_wqn=wqn.astype(BF16), mla_wqr=jnp.pad(wqr, rpad).astype(BF16),
        mla_wqs=jnp.pad(_swap_rope_halves(wqr), rpad).astype(BF16),
        mla_wkt=wkt.astype(BF16), mla_wvt=wvt.astype(BF16),
        w_branch_out=jnp.stack([w_lru_out[l], w_na_out[l], w_cm_out[l], w_mla_out[l]], axis=0).astype(BF16),
        w_out=w_out[l].astype(BF16),
        w_router=router.astype(BF16),
        w13=jnp.concatenate([moe_w1[l], moe_w3[l]], axis=-1).astype(BF16),
        w2=moe_w2[l].astype(BF16),
    )


def kernel(x_prompt, x_sample, cache_na_k, cache_na_v, cache_mla_ckv, cache_mla_krope, state_lru, c, c_ctx,
           w_ada, b_ada, norm1_g, w_in, lru_conv_w, lru_conv_b, lru_wa, lru_ba, lru_wx, lru_bx, lru_lam,
           w_lru_out, na_rpb, w_na_out, cm_dw_w, cm_dw_b, cm_ln_g, cm_ln_b, w_cm_out, mla_q_norm, mla_wq_b,
           mla_kv_norm, mla_wkv_b, w_mla_out, w_out, norm2_g, moe_w_group, moe_w_expert, moe_w1, moe_w3,
           moe_w2, final_g):
    nb_c, seq_c, d = x_prompt.shape
    nb_l, seq_l, _ = x_sample.shape
    depth = w_in.shape[0]
    past = cache_na_k.shape[2]
    t_ctx = nb_c * seq_c
    t_lat = nb_l * seq_l
    assert d == D_MODEL and seq_l % (GRID_W * NA_QROWS) == 0 and seq_l % seq_c == 0
    assert t_ctx % TM_MOE == 0 and seq_l % TM_MOE == 0 and seq_c % SEQ_CHUNK == 0 and seq_l % TK_MLA == 0
    n_ctx_tiles = t_ctx // TM_TOKEN
    tiles_per_lat = seq_l // TM_TOKEN
    lat_blk0 = t_ctx // seq_l
    assert lat_blk0 * seq_l == t_ctx and (seq_l // TK_MLA) % 2 == 0

    t_all = t_ctx + t_lat
    h = _TwoSource(x_prompt.reshape(t_ctx, d), x_sample.reshape(t_lat, d), n_ctx_tiles, 0)
    n_cond = 1 + nb_l
    cvec = jnp.concatenate([c_ctx[None, :], c, jnp.zeros((-n_cond % 8, d), F32)], axis=0)
    mods = _modulation(cvec, w_ada, b_ada)
    cos, sin = _rope_tables(seq_l)
    zero_state = jnp.zeros((nb_c, 2, LRU_WIDTH), F32)

    st_k, st_v, st_ckv, st_kr, st_lru = [], [], [], [], []
    for l in range(depth):
        p = _layer_params(l, w_in, lru_conv_w, lru_conv_b, lru_wa, lru_ba, lru_wx, lru_bx, lru_lam, w_lru_out,
                          w_na_out, cm_dw_w, cm_dw_b, cm_ln_g, cm_ln_b, w_cm_out, mla_q_norm, mla_wq_b,
                          mla_kv_norm, mla_wkv_b, w_mla_out, w_out, moe_w_group, moe_w_expert, moe_w1, moe_w3,
                          moe_w2)
        mod_l = mods[l].reshape(mods.shape[1], 1, mods.shape[2])
        g1 = norm1_g[l].reshape(1, d)
        g2 = norm2_g[l].reshape(1, d)
        z = _in_proj(h, t_all, mod_l, g1, p["w_small"], n_ctx_tiles, tiles_per_lat)

        pa_c, lru_c = _lru_branch(z["lx"], z["ly"], zero_state, p, seq_c, 0, nb_c)
        pc_c = _conformer_branch(z["ga"], z["gg"], p, seq_c, 0, nb_c)
        pb_c, pd_c, ckvn_c = _ctx_attention(z, p, seq_c, nb_c)

        pa_l, _ = _lru_branch(z["lx"], z["ly"], state_lru[:, l], p, seq_l, lat_blk0, nb_l)
        pc_l = _conformer_branch(z["ga"], z["gg"], p, seq_l, lat_blk0, nb_l)
        bias = _na_bias_tables(na_rpb[l], seq_l // GRID_W)
        pb_l = _na_latent(z, cache_na_k[:, l].reshape(nb_l, past, NA_HEADS * NA_HD),
                          cache_na_v[:, l].reshape(nb_l, past, NA_HEADS * NA_HD), bias, seq_l, lat_blk0, nb_l)
        qm, kl, klt = _mla_prep(z, cos, sin, p, seq_l, lat_blk0, nb_l)
        kc = jnp.concatenate([cache_mla_ckv[:, l], cache_mla_krope[:, l],
                              jnp.zeros((nb_l, past, MLA_KDIM - MLA_KV_LORA - MLA_ROPE), F32)], axis=-1).astype(BF16)
        kct = jnp.swapaxes(cache_mla_ckv[:, l], 1, 2).astype(BF16)
        pd_l = _mla_latent(qm, kl, klt, kc, kct, p["mla_wvt"], seq_l, nb_l)

        branches = [_TwoSource(br_c, br_l, n_ctx_tiles, 0)
                    for br_c, br_l in ((pa_c, pa_l), (pb_c, pb_l), (pc_c, pc_l), (pd_c, pd_l))]
        h_mid = _merge(h, t_all, mod_l, g1, branches, p, n_ctx_tiles, tiles_per_lat)
        if l + 1 < depth:
            h_new, = _moe(h_mid, mod_l, g2, p, n_ctx_tiles, tiles_per_lat)
            h = _TwoSource(h_new, h_new, n_ctx_tiles, n_ctx_tiles)
        else:
            y_prompt, y_sample = _moe(h_mid, mod_l, g2, p, n_ctx_tiles, tiles_per_lat, final_g.reshape(1, d))

        st_k.append(z["k"][:t_ctx].reshape(nb_c, seq_c, NA_HEADS, NA_HD))
        st_v.append(z["v"][:t_ctx].reshape(nb_c, seq_c, NA_HEADS, NA_HD))
        st_ckv.append(ckvn_c.reshape(nb_c, seq_c, MLA_KV_LORA))
        st_kr.append(z["krp"][:t_ctx, :MLA_ROPE].reshape(nb_c, seq_c, MLA_ROPE))
        st_lru.append(lru_c)

    y_prompt = y_prompt.reshape(nb_c, seq_c, d)
    y_sample = y_sample.reshape(nb_l, seq_l, d)
    return (y_prompt, y_sample, jnp.stack(st_k, axis=1), jnp.stack(st_v, axis=1), jnp.stack(st_ckv, axis=1),
            jnp.stack(st_kr, axis=1), jnp.stack(st_lru, axis=1))
```
